```python
import math
import jax, jax.numpy as jnp
from jax import lax
import numpy as np

D_MODEL = 1024
BATCH = 8
SEQ = 4096
DEPTH = 2

HEAD_DIM = 64
SB_HEADS = 8
SB_WIDTH = SB_HEADS * HEAD_DIM
DIFF_HEADS = 4
DIFF_QK_WIDTH = DIFF_HEADS * 2 * HEAD_DIM
DIFF_V_WIDTH = DIFF_HEADS * 2 * HEAD_DIM
CONV_CH = 512
CONV_WIDTH = 31
N_BRANCH = 3
XA_HEADS = 4
XA_HEAD_DIM = 128
XA_WIDTH = XA_HEADS * XA_HEAD_DIM
MEM_LEN = 256
D_FF = 4 * D_MODEL
NUM_BUCKETS = 32
MAX_EXACT = NUM_BUCKETS // 2
MAX_DISTANCE = 128
Q_BLOCK = 128
EPS = 1e-6
NEG_INF = -1e30

IN_SPLITS = (SB_WIDTH, SB_WIDTH, SB_WIDTH,
             DIFF_QK_WIDTH, DIFF_QK_WIDTH, DIFF_V_WIDTH,
             2 * CONV_CH,
             D_MODEL, D_MODEL, D_MODEL)
IN_COLS = sum(IN_SPLITS)
SPLIT_POINTS = tuple(sum(IN_SPLITS[:i + 1]) for i in range(len(IN_SPLITS) - 1))

kernel_name = "hybrid_gated_sb_diff_conformer_block"


def rmsnorm(x, g):
    xf = x.astype(jnp.float32)
    y = xf * lax.rsqrt(jnp.mean(xf * xf, axis=-1, keepdims=True) + EPS)
    return (y * g.astype(jnp.float32)).astype(x.dtype)


def layernorm(x, g, b):
    xf = x.astype(jnp.float32)
    mu = jnp.mean(xf, axis=-1, keepdims=True)
    var = jnp.mean(jnp.square(xf - mu), axis=-1, keepdims=True)
    y = (xf - mu) * lax.rsqrt(var + EPS)
    return (y * g.astype(jnp.float32) + b.astype(jnp.float32)).astype(x.dtype)


def rel_bucket(rel):
    n = jnp.maximum(rel, 0)
    is_small = n < MAX_EXACT
    nf = jnp.maximum(n, 1).astype(jnp.float32)
    large = MAX_EXACT + (jnp.log(nf / MAX_EXACT) / math.log(MAX_DISTANCE / MAX_EXACT)
                         * (NUM_BUCKETS - MAX_EXACT)).astype(jnp.int32)
    large = jnp.minimum(large, NUM_BUCKETS - 1)
    return jnp.where(is_small, n, large)


def to_query_blocks(q):
    B, S = q.shape[:2]
    nb = S // Q_BLOCK
    qb = q.reshape((B, nb, Q_BLOCK) + q.shape[2:])
    return jnp.moveaxis(qb, 1, 0), jnp.arange(nb, dtype=jnp.int32) * Q_BLOCK


def from_query_blocks(o):
    o = jnp.moveaxis(o, 0, 1)
    return o.reshape((o.shape[0], o.shape[1] * o.shape[2]) + o.shape[3:])


def stick_breaking_attention(q, k, v):
    S = q.shape[1]
    scale = q.shape[-1] ** -0.5
    key_pos = jnp.arange(S, dtype=jnp.int32)
    qb, starts = to_query_blocks(q)

    def block(args):
        q_blk, start = args
        z = jnp.einsum('bqhd,bkhd->bhqk', q_blk, k,
                       preferred_element_type=jnp.float32) * scale
        q_pos = start + jnp.arange(Q_BLOCK, dtype=jnp.int32)
        strict = key_pos[None, :] < q_pos[:, None]
        log_1m_beta = jnp.where(strict, -jax.nn.softplus(z), 0.0)
        after = lax.cumsum(log_1m_beta, axis=3, reverse=True) - log_1m_beta
        a = jnp.where(strict, jnp.exp(jax.nn.log_sigmoid(z) + after), 0.0)
        return jnp.einsum('bhqk,bkhd->bqhd', a.astype(v.dtype), v)

    return from_query_blocks(lax.map(block, (qb, starts)))


def diff_attention(q, k, v, lam, bias_table):
    S = q.shape[1]
    scale = q.shape[-1] ** -0.5
    key_pos = jnp.arange(S, dtype=jnp.int32)
    table = bias_table.astype(jnp.float32)
    qb, starts = to_query_blocks(q)

    def block(args):
        q_blk, start = args
        z = jnp.einsum('bqhmd,bkhmd->bhmqk', q_blk, k,
                       preferred_element_type=jnp.float32) * scale
        q_pos = start + jnp.arange(Q_BLOCK, dtype=jnp.int32)
        rel = q_pos[:, None] - key_pos[None, :]
        bias = jnp.transpose(table[rel_bucket(rel)], (2, 0, 1))
        z = jnp.where(rel >= 0, z + bias[:, None], NEG_INF)
        p = jax.nn.softmax(z, axis=-1)
        a = p[:, :, 0] - lam * p[:, :, 1]
        return jnp.einsum('bhqk,bkhe->bqhe', a.astype(v.dtype), v)

    return from_query_blocks(lax.map(block, (qb, starts)))


def conformer_conv(u, w_dw, b_dw, ln_g, ln_b):
    a, g = jnp.split(u, 2, axis=-1)
    h = a * jax.nn.sigmoid(g)
    h = lax.conv_general_dilated(
        h, w_dw.astype(h.dtype), window_strides=(1,), padding=[(CONV_WIDTH - 1, 0)],
        dimension_numbers=('NWC', 'WIO', 'NWC'), feature_group_count=CONV_CH) + b_dw
    h = layernorm(h, ln_g, ln_b)
    return jax.nn.silu(h)


def cross_attention(xn, memn, w_q, w_kv, w_o):
    B, S, _ = xn.shape
    q = (xn @ w_q).reshape(B, S, XA_HEADS, XA_HEAD_DIM)
    k, v = jnp.split(memn @ w_kv, 2, axis=-1)
    k = k.reshape(B, MEM_LEN, XA_HEADS, XA_HEAD_DIM)
    v = v.reshape(B, MEM_LEN, XA_HEADS, XA_HEAD_DIM)
    s = jnp.einsum('bqhd,bmhd->bhqm', q, k,
                   preferred_element_type=jnp.float32) * (XA_HEAD_DIM ** -0.5)
    p = jax.nn.softmax(s, axis=-1)
    o = jnp.einsum('bhqm,bmhd->bqhd', p.astype(v.dtype), v).reshape(B, S, XA_WIDTH)
    return o @ w_o


def sqrelu_mlp(xn, w_up, w_down):
    return jnp.square(jax.nn.relu(xn @ w_up)) @ w_down


def setup_inputs(seed: int = 0) -> dict:
    key = jax.random.key(seed)
    ks = jax.random.split(key, 24)
    f32 = jnp.float32

    def w(k, shape, fan_in):
        return jax.random.normal(k, shape, f32) * (fan_in ** -0.5)

    def gain(k, shape):
        return 1.0 + 0.1 * jax.random.normal(k, shape, f32)

    return {
        "x": jax.random.normal(ks[0], (BATCH, SEQ, D_MODEL), f32),
        "mem": jax.random.normal(ks[1], (BATCH, MEM_LEN, D_MODEL), f32),
        "rel_bias": 0.5 * jax.random.normal(ks[2], (NUM_BUCKETS, DIFF_HEADS), f32),
        "ln_mix": gain(ks[3], (DEPTH, D_MODEL)),
        "w_in": w(ks[4], (DEPTH, D_MODEL, IN_COLS), D_MODEL),
        "diff_lambda": 0.1 * jax.random.normal(ks[5], (DEPTH, 4, HEAD_DIM), f32),
        "diff_subln": gain(ks[6], (DEPTH, 2 * HEAD_DIM)),
        "conv_w": w(ks[7], (DEPTH, CONV_WIDTH, 1, CONV_CH), CONV_WIDTH),
        "conv_b": 0.02 * jax.random.normal(ks[8], (DEPTH, CONV_CH), f32),
        "conv_ln_g": gain(ks[9], (DEPTH, CONV_CH)),
        "conv_ln_b": 0.02 * jax.random.normal(ks[10], (DEPTH, CONV_CH), f32),
        "w_sb_proj": w(ks[11], (DEPTH, SB_WIDTH, D_MODEL), SB_WIDTH),
        "w_diff_proj": w(ks[12], (DEPTH, DIFF_V_WIDTH, D_MODEL), DIFF_V_WIDTH),
        "w_conv_proj": w(ks[13], (DEPTH, CONV_CH, D_MODEL), CONV_CH),
        "w_out": w(ks[14], (DEPTH, D_MODEL, D_MODEL), D_MODEL),
        "ln_xattn": gain(ks[15], (DEPTH, D_MODEL)),
        "ln_mem": gain(ks[16], (DEPTH, D_MODEL)),
        "xa_w_q": w(ks[17], (DEPTH, D_MODEL, XA_WIDTH), D_MODEL),
        "xa_w_kv": w(ks[18], (DEPTH, D_MODEL, 2 * XA_WIDTH), D_MODEL),
        "xa_w_o": w(ks[19], (DEPTH, XA_WIDTH, D_MODEL), XA_WIDTH),
        "ln_mlp": gain(ks[20], (DEPTH, D_MODEL)),
        "w_up": w(ks[21], (DEPTH, D_MODEL, D_FF), D_MODEL),
        "w_down": w(ks[22], (DEPTH, D_FF, D_MODEL), D_FF),
        "ln_final": gain(ks[23], (D_MODEL,)),
    }


def reference(x, mem, rel_bias, ln_mix, w_in, diff_lambda, diff_subln, conv_w, conv_b,
              conv_ln_g, conv_ln_b, w_sb_proj, w_diff_proj, w_conv_proj, w_out,
              ln_xattn, ln_mem, xa_w_q, xa_w_kv, xa_w_o, ln_mlp, w_up, w_down, ln_final):
    B, S, _ = x.shape
    for l in range(DEPTH):
        xn = rmsnorm(x, ln_mix[l])
        proj = xn @ w_in[l]
        (sb_q, sb_k, sb_v, df_q, df_k, df_v, cv_u,
         g_sb, g_df, g_cv) = jnp.split(proj, SPLIT_POINTS, axis=-1)

        o_sb = stick_breaking_attention(
            sb_q.reshape(B, S, SB_HEADS, HEAD_DIM),
            sb_k.reshape(B, S, SB_HEADS, HEAD_DIM),
            sb_v.reshape(B, S, SB_HEADS, HEAD_DIM)).reshape(B, S, SB_WIDTH)

        lam_init = 0.8 - 0.6 * math.exp(-0.3 * l)
        lam_p = diff_lambda[l].astype(jnp.float32)
        lam = (jnp.exp(jnp.sum(lam_p[0] * lam_p[1]))
               - jnp.exp(jnp.sum(lam_p[2] * lam_p[3])) + lam_init)
        o_df = diff_attention(
            df_q.reshape(B, S, DIFF_HEADS, 2, HEAD_DIM),
            df_k.reshape(B, S, DIFF_HEADS, 2, HEAD_DIM),
            df_v.reshape(B, S, DIFF_HEADS, 2 * HEAD_DIM), lam, rel_bias)
        o_df = (rmsnorm(o_df, diff_subln[l]) * (1.0 - lam_init)).reshape(B, S, DIFF_V_WIDTH)

        o_cv = conformer_conv(cv_u, conv_w[l], conv_b[l], conv_ln_g[l], conv_ln_b[l])

        y = (jax.nn.sigmoid(g_sb) * (o_sb @ w_sb_proj[l])
             + jax.nn.sigmoid(g_df) * (o_df @ w_diff_proj[l])
             + jax.nn.sigmoid(g_cv) * (o_cv @ w_conv_proj[l]))
        x = x + y @ w_out[l]

        x = x + cross_attention(rmsnorm(x, ln_xattn[l]), rmsnorm(mem, ln_mem[l]),
                                xa_w_q[l], xa_w_kv[l], xa_w_o[l])

        x = x + sqrelu_mlp(rmsnorm(x, ln_mlp[l]), w_up[l], w_down[l])
    return rmsnorm(x, ln_final)
```

```python
import functools
import math

import numpy as np
import jax
import jax.numpy as jnp
from jax import lax
from jax.experimental import pallas as pl
from jax.experimental.pallas import tpu as pltpu

F32 = jnp.float32
BF16 = jnp.bfloat16

HEAD_DIM = 64
SB_WIDTH = 512
DIFF_WIDTH = 512
DIFF_HEADS = 4
CONV_CH = 512
CONV_WIDTH = 31
XA_HEADS = 4
XA_HEAD_DIM = 128
XA_WIDTH = XA_HEADS * XA_HEAD_DIM
NUM_BUCKETS = 32
MAX_EXACT = NUM_BUCKETS // 2
MAX_DISTANCE = 128
EPS = 1e-6
NEG_INF = -1e30

LANES = 128
ATT_BLOCK = 256
CONV_HALO = 32
CONV_ROWS = 64
VMEM_LIMIT = 56 * 1024 * 1024

ROW_SBQ, ROW_SBV, ROW_DFQ, ROW_DFV = 0, 4, 8, 12
ROW_COLS = 4 * 512 + 2 * CONV_CH
KT_ROWS = 2 * 512


def _params(*sem):
    return pltpu.CompilerParams(dimension_semantics=sem, vmem_limit_bytes=VMEM_LIMIT)


def _const_spec(shape):
    nd = len(shape)
    return pl.BlockSpec(shape, lambda *_: (0,) * nd)


def _rmsnorm(x, g):
    return x * lax.rsqrt(jnp.mean(x * x, axis=-1, keepdims=True) + EPS) * g


def _dot(a, b):
    return jnp.dot(a, b, preferred_element_type=F32)


def _dot_nt(a, b):
    return lax.dot_general(a, b, (((1,), (1,)), ((), ())), preferred_element_type=F32)


def _inproj_kernel(x_ref, g_ref, wrow_ref, wkt_ref, row_ref, kt_ref):
    xb = _rmsnorm(x_ref[0], g_ref[...]).astype(BF16)
    for c in range(ROW_COLS // 512):
        cs = slice(c * 512, (c + 1) * 512)
        row_ref[0, :, cs] = _dot(xb, wrow_ref[:, cs]).astype(BF16)
    for c in range(KT_ROWS // 512):
        cs = slice(c * 512, (c + 1) * 512)
        kt_ref[0, cs, :] = _dot_nt(wkt_ref[cs, :], xb).astype(BF16)


def _inproj(x, g, wrow, wkt, tm):
    B, S, D = x.shape
    return pl.pallas_call(
        _inproj_kernel,
        grid=(B, S // tm),
        in_specs=[
            pl.BlockSpec((1, tm, D), lambda b, i: (b, i, 0)),
            _const_spec((1, D)),
            _const_spec((D, ROW_COLS)),
            _const_spec((KT_ROWS, D)),
        ],
        out_specs=[
            pl.BlockSpec((1, tm, ROW_COLS), lambda b, i: (b, i, 0)),
            pl.BlockSpec((1, KT_ROWS, tm), lambda b, i: (b, 0, i)),
        ],
        out_shape=[
            jax.ShapeDtypeStruct((B, S, ROW_COLS), BF16),
            jax.ShapeDtypeStruct((B, KT_ROWS, S), BF16),
        ],
        compiler_params=_params("parallel", "parallel"),
        name="inproj",
    )(x, g, wrow, wkt)


def _sb_kernel(q_ref, kt_ref, v_ref, tri_ref, o_ref):
    blk = ATT_BLOCK
    i = pl.program_id(2)
    q = q_ref[0]
    lane = lax.broadcasted_iota(jnp.int32, q.shape, 1)
    zero = jnp.zeros_like(q)
    qm = (jnp.where(lane < HEAD_DIM, q, zero), jnp.where(lane >= HEAD_DIM, q, zero))
    row = lax.broadcasted_iota(jnp.int32, (blk, blk), 0)
    col = lax.broadcasted_iota(jnp.int32, (blk, blk), 1)
    strict = col < row

    def tile(h, j, carry, acc, diag):
        ks = pl.ds(pl.multiple_of(j * blk, blk), blk)
        z = _dot(qm[h], kt_ref[0, :, ks])
        sp = jnp.maximum(z, 0.0) + jnp.log(1.0 + jnp.exp(-jnp.abs(z)))
        if diag:
            sp = jnp.where(strict, sp, 0.0)
        hi = sp.astype(BF16)
        lo = (sp - hi.astype(F32)).astype(BF16)
        tri = tri_ref[...]
        c = _dot(hi, tri) + _dot(lo, tri) + carry
        a = jnp.exp(z - c)
        if diag:
            a = jnp.where(strict, a, 0.0)
        acc = acc + _dot(a.astype(BF16), v_ref[0, ks, :])
        return c[:, 0:1], acc

    zc = jnp.zeros((blk, 1), F32)
    za = jnp.zeros((blk, LANES), F32)
    c0, a0 = tile(0, i, zc, za, True)
    c1, a1 = tile(1, i, zc, za, True)

    def body(jj, st):
        j = i - 1 - jj
        c0, a0, c1, a1 = st
        c0, a0 = tile(0, j, c0, a0, False)
        c1, a1 = tile(1, j, c1, a1, False)
        return c0, a0, c1, a1

    _, a0, _, a1 = lax.fori_loop(0, i, body, (c0, a0, c1, a1))
    o_ref[0] = jnp.where(lane < HEAD_DIM, a0, a1).astype(o_ref.dtype)


def _sb_attention(row, kt, tri):
    B, S, _ = row.shape
    blk = ATT_BLOCK
    return pl.pallas_call(
        _sb_kernel,
        grid=(B, SB_WIDTH // LANES, S // blk),
        in_specs=[
            pl.BlockSpec((1, blk, LANES), lambda b, p, i: (b, i, ROW_SBQ + p)),
            pl.BlockSpec((1, LANES, S), lambda b, p, i: (b, p, 0)),
            pl.BlockSpec((1, S, LANES), lambda b, p, i: (b, 0, ROW_SBV + p)),
            _const_spec((blk, blk)),
        ],
        out_specs=pl.BlockSpec((1, blk, LANES), lambda b, p, i: (b, i, p)),
        out_shape=jax.ShapeDtypeStruct((B, S, SB_WIDTH), BF16),
        compiler_params=_params("parallel", "parallel", "arbitrary"),
        name="sb_attention",
    )(row, kt, row, tri)


def _bucket_tiles(blk):
    r = np.arange(blk)[:, None] - np.arange(blk)[None, :]
    rel = np.stack([r, r + blk, r + 2 * blk]).astype(np.int64)
    n = np.maximum(rel, 0)
    nf = np.maximum(n, 1).astype(np.float32)
    large = MAX_EXACT + (np.log(nf / np.float32(MAX_EXACT)) / np.float32(math.log(MAX_DISTANCE / MAX_EXACT))
                         * np.float32(NUM_BUCKETS - MAX_EXACT)).astype(np.int32)
    large = np.minimum(large, NUM_BUCKETS - 1)
    idx = np.where(n < MAX_EXACT, n, large).astype(np.int32)
    assert (idx[2] == NUM_BUCKETS - 1).all() and blk >= MAX_DISTANCE
    return idx


def _bias_kernel(table_ref, idx_ref, o_ref):
    h = pl.program_id(0)
    idx = idx_ref[0]
    acc = jnp.zeros(idx.shape, F32)
    for b in range(NUM_BUCKETS):
        acc = jnp.where(idx == b, table_ref[b, h], acc)
    o_ref[0, 0] = acc


def _bias_tiles(rel_bias, blk):
    idx = jnp.asarray(_bucket_tiles(blk))
    return pl.pallas_call(
        _bias_kernel,
        grid=(DIFF_HEADS, 3),
        in_specs=[
            pl.BlockSpec(memory_space=pltpu.SMEM),
            pl.BlockSpec((1, blk, blk), lambda h, d: (d, 0, 0)),
        ],
        out_specs=pl.BlockSpec((1, 1, blk, blk), lambda h, d: (h, d, 0, 0)),
        out_shape=jax.ShapeDtypeStruct((DIFF_HEADS, 3, blk, blk), F32),
        name="bias_tiles",
    )(rel_bias.astype(F32), idx)


def _diff_kernel(q_ref, kt_ref, v_ref, bias_ref, lam_ref, subln_ref, o_ref, *, lam_init):
    blk = ATT_BLOCK
    i = pl.program_id(2)
    q = q_ref[0]
    lane = lax.broadcasted_iota(jnp.int32, q.shape, 1)
    zero = jnp.zeros_like(q)
    qm = (jnp.where(lane < HEAD_DIM, q, zero), jnp.where(lane >= HEAD_DIM, q, zero))
    row = lax.broadcasted_iota(jnp.int32, (blk, blk), 0)
    col = lax.broadcasted_iota(jnp.int32, (blk, blk), 1)
    causal = col <= row

    def tile(j, st, diag):
        ks = pl.ds(pl.multiple_of(j * blk, blk), blk)
        kt = kt_ref[0, :, ks]
        v = v_ref[0, ks, :]
        bias = bias_ref[0, jnp.minimum(i - j, 2)]
        out = []
        for m in range(2):
            mx, l, acc = st[m]
            z = _dot(qm[m], kt) + bias
            if diag:
                z = jnp.where(causal, z, NEG_INF)
            mnew = jnp.maximum(mx, jnp.max(z, axis=1, keepdims=True))
            alpha = jnp.exp(mx - mnew)
            p = jnp.exp(z - mnew)
            l = alpha * l + jnp.sum(p, axis=1, keepdims=True)
            acc = alpha * acc + _dot(p.astype(BF16), v)
            out.append((mnew, l, acc))
        return tuple(out)

    init = (jnp.full((blk, 1), NEG_INF, F32), jnp.zeros((blk, 1), F32), jnp.zeros((blk, LANES), F32))
    st = tile(i, (init, init), True)
    st = lax.fori_loop(0, i, lambda jj, s: tile(i - 1 - jj, s, False), st)

    lp = lam_ref[...]
    lam = (jnp.exp(jnp.sum(lp[0:1] * lp[1:2], axis=1, keepdims=True))
           - jnp.exp(jnp.sum(lp[2:3] * lp[3:4], axis=1, keepdims=True)) + lam_init)
    (_, l0, acc0), (_, l1, acc1) = st
    o = acc0 / l0 - lam * (acc1 / l1)
    o_ref[0] = (_rmsnorm(o, subln_ref[...]) * (1.0 - lam_init)).astype(o_ref.dtype)


def _diff_attention(row, kt, bias, lam_p, subln, lam_init):
    B, S, _ = row.shape
    blk = ATT_BLOCK
    return pl.pallas_call(
        functools.partial(_diff_kernel, lam_init=lam_init),
        grid=(B, DIFF_HEADS, S // blk),
        in_specs=[
            pl.BlockSpec((1, blk, LANES), lambda b, h, i: (b, i, ROW_DFQ + h)),
            pl.BlockSpec((1, LANES, S), lambda b, h, i: (b, SB_WIDTH // LANES + h, 0)),
            pl.BlockSpec((1, S, LANES), lambda b, h, i: (b, 0, ROW_DFV + h)),
            pl.BlockSpec((1, 3, blk, blk), lambda b, h, i: (h, 0, 0, 0)),
            _const_spec((4, HEAD_DIM)),
            _const_spec((1, 2 * HEAD_DIM)),
        ],
        out_specs=pl.BlockSpec((1, blk, LANES), lambda b, h, i: (b, i, h)),
        out_shape=jax.ShapeDtypeStruct((B, S, DIFF_WIDTH), BF16),
        compiler_params=_params("parallel", "parallel", "arbitrary"),
        name="diff_attention",
    )(row, kt, row, bias, lam_p, subln)


def _conv_kernel(u_ref, uprev_ref, cw_ref, cb_ref, lg_ref, lb_ref, o_ref, h_ref, *, tc):
    i = pl.program_id(1)

    def glu(u):
        u = u.astype(F32)
        return u[:, :CONV_CH] * jax.nn.sigmoid(u[:, CONV_CH:])

    h_ref[0:CONV_HALO, :] = jnp.where(i == 0, 0.0, glu(uprev_ref[0]))
    h_ref[CONV_HALO:CONV_HALO + tc, :] = glu(u_ref[0])
    first = CONV_HALO - (CONV_WIDTH - 1)
    for c in range(tc // CONV_ROWS):
        acc = jnp.zeros((CONV_ROWS, CONV_CH), F32)
        for w in range(CONV_WIDTH):
            r0 = c * CONV_ROWS + first + w
            acc = acc + h_ref[r0:r0 + CONV_ROWS, :] * cw_ref[w:w + 1, :]
        y = acc + cb_ref[...]
        mu = jnp.mean(y, axis=-1, keepdims=True)
        var = jnp.mean(jnp.square(y - mu), axis=-1, keepdims=True)
        y = (y - mu) * lax.rsqrt(var + EPS) * lg_ref[...] + lb_ref[...]
        o_ref[0, c * CONV_ROWS:(c + 1) * CONV_ROWS, :] = (y * jax.nn.sigmoid(y)).astype(o_ref.dtype)


def _conv(row, cw, cb, lg, lb, tc):
    B, S, _ = row.shape
    ucol = (ROW_COLS - 2 * CONV_CH) // (2 * CONV_CH)
    per = tc // CONV_HALO
    return pl.pallas_call(
        functools.partial(_conv_kernel, tc=tc),
        grid=(B, S // tc),
        in_specs=[
            pl.BlockSpec((1, tc, 2 * CONV_CH), lambda b, i: (b, i, ucol)),
            pl.BlockSpec((1, CONV_HALO, 2 * CONV_CH), lambda b, i: (b, jnp.maximum(i * per - 1, 0), ucol)),
            _const_spec((CONV_WIDTH, CONV_CH)),
            _const_spec((1, CONV_CH)),
            _const_spec((1, CONV_CH)),
            _const_spec((1, CONV_CH)),
        ],
        out_specs=pl.BlockSpec((1, tc, CONV_CH), lambda b, i: (b, i, 0)),
        out_shape=jax.ShapeDtypeStruct((B, S, CONV_CH), BF16),
        scratch_shapes=[pltpu.VMEM((CONV_HALO + tc, CONV_CH), F32)],
        compiler_params=_params("parallel", "parallel"),
        name="conformer_conv",
    )(row, row, cw, cb, lg, lb)


def _merge_kernel(x_ref, g_ref, wg_ref, osb_ref, odf_ref, ocv_ref, wsb_ref, wdf_ref, wcv_ref, wout_ref, o_ref):
    x = x_ref[...]
    D = x.shape[1]
    xb = _rmsnorm(x, g_ref[...]).astype(BF16)
    y = None
    for n, (o_br, w_br) in enumerate(((osb_ref, wsb_ref), (odf_ref, wdf_ref), (ocv_ref, wcv_ref))):
        gate = jax.nn.sigmoid(_dot(xb, wg_ref[:, n * D:(n + 1) * D]))
        t = gate * _dot(o_br[...], w_br[...])
        y = t if y is None else y + t
    o_ref[...] = x + _dot(y.astype(BF16), wout_ref[...])


def _merge(x2, g, wg, osb, odf, ocv, wsb, wdf, wcv, wout, tm):
    N, D = x2.shape
    W = osb.shape[1]
    tok = lambda w: pl.BlockSpec((tm, w), lambda i: (i, 0))
    return pl.pallas_call(
        _merge_kernel,
        grid=(N // tm,),
        in_specs=[tok(D), _const_spec((1, D)), _const_spec((D, 3 * D)), tok(W), tok(W), tok(W),
                  _const_spec((W, D)), _const_spec((W, D)), _const_spec((W, D)), _const_spec((D, D))],
        out_specs=tok(D),
        out_shape=jax.ShapeDtypeStruct((N, D), F32),
        compiler_params=_params("parallel"),
        name="gated_merge",
    )(x2, g, wg, osb, odf, ocv, wsb, wdf, wcv, wout)


def _memkv_kernel(mem_ref, g_ref, wkv_ref, o_ref):
    mb = _rmsnorm(mem_ref[0], g_ref[...]).astype(BF16)
    o_ref[0] = _dot(mb, wkv_ref[...]).astype(o_ref.dtype)


def _memkv(mem, g, wkv):
    B, M, D = mem.shape
    return pl.pallas_call(
        _memkv_kernel,
        grid=(B,),
        in_specs=[pl.BlockSpec((1, M, D), lambda b: (b, 0, 0)), _const_spec((1, D)), _const_spec((D, 2 * XA_WIDTH))],
        out_specs=pl.BlockSpec((1, M, 2 * XA_WIDTH), lambda b: (b, 0, 0)),
        out_shape=jax.ShapeDtypeStruct((B, M, 2 * XA_WIDTH), BF16),
        compiler_params=_params("parallel"),
        name="mem_kv",
    )(mem, g, wkv)


def _xattn_kernel(x_ref, g_ref, wq_ref, kv_ref, wo_ref, o_ref):
    x = x_ref[0]
    xb = _rmsnorm(x, g_ref[...]).astype(BF16)
    q = _dot(xb, wq_ref[...]).astype(BF16)
    heads = []
    for h in range(XA_HEADS):
        hs = slice(h * XA_HEAD_DIM, (h + 1) * XA_HEAD_DIM)
        k = kv_ref[0, :, hs]
        v = kv_ref[0, :, XA_WIDTH + h * XA_HEAD_DIM:XA_WIDTH + (h + 1) * XA_HEAD_DIM]
        s = _dot_nt(q[:, hs], k) * (XA_HEAD_DIM ** -0.5)
        p = jnp.exp(s - jnp.max(s, axis=1, keepdims=True))
        p = p / jnp.sum(p, axis=1, keepdims=True)
        heads.append(_dot(p.astype(BF16), v).astype(BF16))
    o = jnp.concatenate(heads, axis=1)
    o_ref[0] = x + _dot(o, wo_ref[...])


def _xattn(x, g, wq, kv, wo, tm):
    B, S, D = x.shape
    M = kv.shape[1]
    return pl.pallas_call(
        _xattn_kernel,
        grid=(B, S // tm),
        in_specs=[
            pl.BlockSpec((1, tm, D), lambda b, i: (b, i, 0)),
            _const_spec((1, D)),
            _const_spec((D, XA_WIDTH)),
            pl.BlockSpec((1, M, 2 * XA_WIDTH), lambda b, i: (b, 0, 0)),
            _const_spec((XA_WIDTH, D)),
        ],
        out_specs=pl.BlockSpec((1, tm, D), lambda b, i: (b, i, 0)),
        out_shape=jax.ShapeDtypeStruct((B, S, D), F32),
        compiler_params=_params("parallel", "parallel"),
        name="cross_attention",
    )(x, g, wq, kv, wo)


def _mlp_kernel(x_ref, g_ref, wup_ref, wdown_ref, gf_ref, o_ref, *, fc, final):
    x = x_ref[...]
    xb = _rmsnorm(x, g_ref[...]).astype(BF16)
    acc = x
    for c in range(wup_ref.shape[1] // fc):
        cs = slice(c * fc, (c + 1) * fc)
        h = jnp.square(jnp.maximum(_dot(xb, wup_ref[:, cs]), 0.0)).astype(BF16)
        acc = acc + _dot(h, wdown_ref[cs, :])
    o_ref[...] = _rmsnorm(acc, gf_ref[...]) if final else acc


def _mlp(x2, g, wup, wdown, gf, tm, final):
    N, D = x2.shape
    F = wup.shape[1]
    return pl.pallas_call(
        functools.partial(_mlp_kernel, fc=1024, final=final),
        grid=(N // tm,),
        in_specs=[pl.BlockSpec((tm, D), lambda i: (i, 0)), _const_spec((1, D)),
                  _const_spec((D, F)), _const_spec((F, D)), _const_spec((1, D))],
        out_specs=pl.BlockSpec((tm, D), lambda i: (i, 0)),
        out_shape=jax.ShapeDtypeStruct((N, D), F32),
        compiler_params=_params("parallel"),
        name="sqrelu_mlp",
    )(x2, g, wup, wdown, gf)


def kernel(x, mem, rel_bias, ln_mix, w_in, diff_lambda, diff_subln, conv_w, conv_b, conv_ln_g, conv_ln_b,
           w_sb_proj, w_diff_proj, w_conv_proj, w_out, ln_xattn, ln_mem, xa_w_q, xa_w_kv, xa_w_o,
           ln_mlp, w_up, w_down, ln_final):
    B, S, D = x.shape
    depth = w_in.shape[0]
    blk = ATT_BLOCK
    tm = min(512, S)
    tc = min(256, S)
    assert S % blk == 0 and S % tm == 0 and S % tc == 0 and D == 1024

    row2 = lambda a: a.reshape(1, -1).astype(F32)
    tri = jnp.asarray(np.tril(np.ones((blk, blk), np.float32)), BF16)
    bias = _bias_tiles(rel_bias, blk)
    scale = HEAD_DIM ** -0.5

    for l in range(depth):
        w = w_in[l]
        sbq, sbk, sbv = w[:, 0:512] * scale, w[:, 512:1024], w[:, 1024:1536]
        dfq, dfk, dfv = w[:, 1536:2048] * scale, w[:, 2048:2560], w[:, 2560:3072]
        wrow = jnp.concatenate([sbq, sbv, dfq, dfv, w[:, 3072:4096]], axis=1).astype(BF16)
        wkt = jnp.concatenate([sbk, dfk], axis=1).T.astype(BF16)
        wg = w[:, 4096:].astype(BF16)

        row, kt = _inproj(x, row2(ln_mix[l]), wrow, wkt, tm)
        o_sb = _sb_attention(row, kt, tri)
        lam_init = 0.8 - 0.6 * math.exp(-0.3 * l)
        o_df = _diff_attention(row, kt, bias, diff_lambda[l].astype(F32), row2(diff_subln[l]), lam_init)
        o_cv = _conv(row, conv_w[l].reshape(CONV_WIDTH, CONV_CH).astype(F32), row2(conv_b[l]),
                     row2(conv_ln_g[l]), row2(conv_ln_b[l]), tc)

        N = B * S
        x2 = _merge(x.reshape(N, D), row2(ln_mix[l]), wg,
                    o_sb.reshape(N, -1), o_df.reshape(N, -1), o_cv.reshape(N, -1),
                    w_sb_proj[l].astype(BF16), w_diff_proj[l].astype(BF16), w_conv_proj[l].astype(BF16),
                    w_out[l].astype(BF16), tm)

        kv = _memkv(mem, row2(ln_mem[l]), xa_w_kv[l].astype(BF16))
        x3 = _xattn(x2.reshape(B, S, D), row2(ln_xattn[l]),
                    xa_w_q[l].astype(BF16), kv, xa_w_o[l].astype(BF16), tm)

        x = _mlp(x3.reshape(N, D), row2(ln_mlp[l]), w_up[l].astype(BF16), w_down[l].astype(BF16),
                 row2(ln_final), tm, final=(l == depth - 1)).reshape(B, S, D)
    return x
```

```python
import functools
import math

import numpy as np
import jax
import jax.numpy as jnp
from jax import lax
from jax.experimental import pallas as pl
from jax.experimental.pallas import tpu as pltpu

F32 = jnp.float32
BF16 = jnp.bfloat16

HEAD_DIM = 64
SB_WIDTH = 512
DIFF_WIDTH = 512
DIFF_HEADS = 4
CONV_CH = 512
CONV_WIDTH = 31
XA_HEADS = 4
XA_HEAD_DIM = 128
XA_WIDTH = XA_HEADS * XA_HEAD_DIM
NUM_BUCKETS = 32
MAX_EXACT = NUM_BUCKETS // 2
MAX_DISTANCE = 128
EPS = 1e-6
NEG_INF = -1e30

LANES = 128
ATT_BLOCK = 256
CONV_HALO = 32
CONV_ROWS = 64
VMEM_LIMIT = 56 * 1024 * 1024

ATT_QBLOCK = 512

ROW_COLS = 2 * CONV_CH + 2 * 512
ROW_SBK, ROW_DFK = 8, 12
TR_ROWS = 4 * 512
TR_SBQ, TR_SBV, TR_DFQ, TR_DFV = 0, 4, 8, 12


def _params(*sem):
    return pltpu.CompilerParams(dimension_semantics=sem, vmem_limit_bytes=VMEM_LIMIT)


def _const_spec(shape):
    nd = len(shape)
    return pl.BlockSpec(shape, lambda *_: (0,) * nd)


def _rmsnorm(x, g):
    return x * lax.rsqrt(jnp.mean(x * x, axis=-1, keepdims=True) + EPS) * g


def _dot(a, b):
    return jnp.dot(a, b, preferred_element_type=F32)


def _dot_nt(a, b):
    return lax.dot_general(a, b, (((1,), (1,)), ((), ())), preferred_element_type=F32)


def _inproj_kernel(x_ref, g_ref, wrow_ref, wtr_ref, row_ref, tr_ref):
    xb = _rmsnorm(x_ref[0], g_ref[...]).astype(BF16)
    for c in range(ROW_COLS // 512):
        cs = slice(c * 512, (c + 1) * 512)
        row_ref[0, :, cs] = _dot(xb, wrow_ref[:, cs]).astype(BF16)
    for c in range(TR_ROWS // 512):
        cs = slice(c * 512, (c + 1) * 512)
        tr_ref[0, cs, :] = _dot_nt(wtr_ref[cs, :], xb).astype(BF16)


def _inproj(x, g, wrow, wtr, tm):
    B, S, D = x.shape
    return pl.pallas_call(
        _inproj_kernel,
        grid=(B, S // tm),
        in_specs=[
            pl.BlockSpec((1, tm, D), lambda b, i: (b, i, 0)),
            _const_spec((1, D)),
            _const_spec((D, ROW_COLS)),
            _const_spec((TR_ROWS, D)),
        ],
        out_specs=[
            pl.BlockSpec((1, tm, ROW_COLS), lambda b, i: (b, i, 0)),
            pl.BlockSpec((1, TR_ROWS, tm), lambda b, i: (b, 0, i)),
        ],
        out_shape=[
            jax.ShapeDtypeStruct((B, S, ROW_COLS), BF16),
            jax.ShapeDtypeStruct((B, TR_ROWS, S), BF16),
        ],
        compiler_params=_params("parallel", "parallel"),
        name="inproj",
    )(x, g, wrow, wtr)


def _sb_kernel(qt_ref, k_ref, vt_ref, tri_ref, o_ref):
    kb, qb = ATT_BLOCK, ATT_QBLOCK
    nq = qb // kb
    i = pl.program_id(2)
    qt = qt_ref[0]
    sub = lax.broadcasted_iota(jnp.int32, qt.shape, 0)
    zero = jnp.zeros_like(qt)
    qtm = (jnp.where(sub < HEAD_DIM, qt, zero), jnp.where(sub >= HEAD_DIM, qt, zero))
    tri = tri_ref[...]

    def tiles(j, st, lo, diag):
        n = qb - lo
        ks = pl.ds(pl.multiple_of(j * kb, kb), kb)
        k = k_ref[0, ks, :]
        vt = vt_ref[0, :, ks]
        carry, acc = (st[0], st[2]), (st[1], st[3])
        z = [_dot(k, qtm[h][:, lo:]) for h in range(2)]
        sp = [jnp.maximum(z[h], 0.0) + jnp.log(1.0 + jnp.exp(-jnp.abs(z[h]))) for h in range(2)]
        if diag:
            strict = (lax.broadcasted_iota(jnp.int32, (kb, n), 0) < lax.broadcasted_iota(jnp.int32, (kb, n), 1))
            sp = [jnp.where(strict, s, 0.0) for s in sp]
        c = [_dot(tri, sp[h].astype(BF16)) + carry[h][:, lo:] for h in range(2)]
        a = [jnp.exp(z[h] - c[h]) for h in range(2)]
        if diag:
            a = [jnp.where(strict, x, 0.0) for x in a]
        pv = [_dot(vt[h * HEAD_DIM:(h + 1) * HEAD_DIM, :], a[h].astype(BF16)) for h in range(2)]
        out = []
        for h in range(2):
            cn, an = c[h][0:1, :], acc[h][:, lo:] + pv[h]
            if lo:
                cn = jnp.concatenate([carry[h][:, :lo], cn], axis=1)
                an = jnp.concatenate([acc[h][:, :lo], an], axis=1)
            out += [cn, an]
        return tuple(out)

    zc = jnp.zeros((1, qb), F32)
    za = jnp.zeros((HEAD_DIM, qb), F32)
    st = (zc, za, zc, za)
    for r in reversed(range(nq)):
        st = tiles(nq * i + r, st, r * kb, True)
    st = lax.fori_loop(0, nq * i, lambda jj, s: tiles(nq * i - 1 - jj, s, 0, False), st)
    o_ref[0] = jnp.concatenate([st[1], st[3]], axis=0).T.astype(o_ref.dtype)


def _sb_attention(row, tr, tri):
    B, S, _ = row.shape
    qb = ATT_QBLOCK
    return pl.pallas_call(
        _sb_kernel,
        grid=(B, SB_WIDTH // LANES, S // qb),
        in_specs=[
            pl.BlockSpec((1, LANES, qb), lambda b, p, i: (b, TR_SBQ + p, i)),
            pl.BlockSpec((1, S, LANES), lambda b, p, i: (b, 0, ROW_SBK + p)),
            pl.BlockSpec((1, LANES, S), lambda b, p, i: (b, TR_SBV + p, 0)),
            _const_spec((ATT_BLOCK, ATT_BLOCK)),
        ],
        out_specs=pl.BlockSpec((1, qb, LANES), lambda b, p, i: (b, i, p)),
        out_shape=jax.ShapeDtypeStruct((B, S, SB_WIDTH), BF16),
        compiler_params=_params("parallel", "parallel", "arbitrary"),
        name="sb_attention",
    )(tr, row, tr, tri)


def _bucket_tiles(blk):
    r = np.arange(blk)[None, :] - np.arange(blk)[:, None]
    rel = np.stack([r, r + blk, r + 2 * blk]).astype(np.int64)
    n = np.maximum(rel, 0)
    nf = np.maximum(n, 1).astype(np.float32)
    large = MAX_EXACT + (np.log(nf / np.float32(MAX_EXACT)) / np.float32(math.log(MAX_DISTANCE / MAX_EXACT))
                         * np.float32(NUM_BUCKETS - MAX_EXACT)).astype(np.int32)
    large = np.minimum(large, NUM_BUCKETS - 1)
    idx = np.where(n < MAX_EXACT, n, large).astype(np.int32)
    assert (idx[2] == NUM_BUCKETS - 1).all() and blk >= MAX_DISTANCE
    return idx


def _bias_kernel(table_ref, idx_ref, o_ref):
    h = pl.program_id(0)
    idx = idx_ref[0]
    acc = jnp.zeros(idx.shape, F32)
    for b in range(NUM_BUCKETS):
        acc = jnp.where(idx == b, table_ref[b, h], acc)
    o_ref[0, 0] = acc


def _bias_tiles(rel_bias, blk):
    idx = jnp.asarray(_bucket_tiles(blk))
    return pl.pallas_call(
        _bias_kernel,
        grid=(DIFF_HEADS, 3),
        in_specs=[
            pl.BlockSpec(memory_space=pltpu.SMEM),
            pl.BlockSpec((1, blk, blk), lambda h, d: (d, 0, 0)),
        ],
        out_specs=pl.BlockSpec((1, 1, blk, blk), lambda h, d: (h, d, 0, 0)),
        out_shape=jax.ShapeDtypeStruct((DIFF_HEADS, 3, blk, blk), F32),
        name="bias_tiles",
    )(rel_bias.astype(F32), idx)


def _diff_kernel(qt_ref, k_ref, vt_ref, bias_ref, lam_ref, subln_ref, o_ref, *, lam_init):
    kb, qb = ATT_BLOCK, ATT_QBLOCK
    nq = qb // kb
    i = pl.program_id(2)
    qt = qt_ref[0]
    sub = lax.broadcasted_iota(jnp.int32, qt.shape, 0)
    zero = jnp.zeros_like(qt)
    qtm = (jnp.where(sub < HEAD_DIM, qt, zero), jnp.where(sub >= HEAD_DIM, qt, zero))

    def tiles(j, st, lo, bias, diag):
        n = qb - lo
        ks = pl.ds(pl.multiple_of(j * kb, kb), kb)
        k = k_ref[0, ks, :]
        vt = vt_ref[0, :, ks]
        z = [_dot(k, qtm[m][:, lo:]) + bias for m in range(2)]
        if diag:
            causal = (lax.broadcasted_iota(jnp.int32, (kb, n), 0) <= lax.broadcasted_iota(jnp.int32, (kb, n), 1))
            z = [jnp.where(causal, x, NEG_INF) for x in z]
        mold = [st[m][0][:, lo:] for m in range(2)]
        mnew = [jnp.maximum(mold[m], jnp.max(z[m], axis=0, keepdims=True)) for m in range(2)]
        alpha = [jnp.exp(mold[m] - mnew[m]) for m in range(2)]
        p = [jnp.exp(z[m] - mnew[m]) for m in range(2)]
        lnew = [alpha[m] * st[m][1][:, lo:] + jnp.sum(p[m], axis=0, keepdims=True) for m in range(2)]
        pv = [_dot(vt, p[m].astype(BF16)) for m in range(2)]
        out = []
        for m in range(2):
            new = (mnew[m], lnew[m], alpha[m] * st[m][2][:, lo:] + pv[m])
            if lo:
                new = tuple(jnp.concatenate([old[:, :lo], x], axis=1) for old, x in zip(st[m], new))
            out.append(new)
        return tuple(out)

    init = (jnp.full((1, qb), NEG_INF, F32), jnp.zeros((1, qb), F32), jnp.zeros((LANES, qb), F32))
    st = (init, init)
    for r in reversed(range(nq)):
        bias = jnp.concatenate([bias_ref[0, min(c - r, 2)] for c in range(r, nq)], axis=1)
        st = tiles(nq * i + r, st, r * kb, bias, True)

    def body(jj, s):
        bias = jnp.concatenate([bias_ref[0, jnp.minimum(jj + 1 + c, 2)] for c in range(nq)], axis=1)
        return tiles(nq * i - 1 - jj, s, 0, bias, False)

    st = lax.fori_loop(0, nq * i, body, st)

    lp = lam_ref[...]
    lam = (jnp.exp(jnp.sum(lp[0:1] * lp[1:2], axis=1, keepdims=True))
           - jnp.exp(jnp.sum(lp[2:3] * lp[3:4], axis=1, keepdims=True)) + lam_init)
    (_, l0, acc0), (_, l1, acc1) = st
    o = (acc0 / l0 - lam * (acc1 / l1)).T
    o_ref[0] = (_rmsnorm(o, subln_ref[...]) * (1.0 - lam_init)).astype(o_ref.dtype)


def _diff_attention(row, tr, bias, lam_p, subln, lam_init):
    B, S, _ = row.shape
    kb, qb = ATT_BLOCK, ATT_QBLOCK
    return pl.pallas_call(
        functools.partial(_diff_kernel, lam_init=lam_init),
        grid=(B, DIFF_HEADS, S // qb),
        in_specs=[
            pl.BlockSpec((1, LANES, qb), lambda b, h, i: (b, TR_DFQ + h, i)),
            pl.BlockSpec((1, S, LANES), lambda b, h, i: (b, 0, ROW_DFK + h)),
            pl.BlockSpec((1, LANES, S), lambda b, h, i: (b, TR_DFV + h, 0)),
            pl.BlockSpec((1, 3, kb, kb), lambda b, h, i: (h, 0, 0, 0)),
            _const_spec((4, HEAD_DIM)),
            _const_spec((1, 2 * HEAD_DIM)),
        ],
        out_specs=pl.BlockSpec((1, qb, LANES), lambda b, h, i: (b, i, h)),
        out_shape=jax.ShapeDtypeStruct((B, S, DIFF_WIDTH), BF16),
        compiler_params=_params("parallel", "parallel", "arbitrary"),
        name="diff_attention",
    )(tr, row, tr, bias, lam_p, subln)


def _conv_kernel(u_ref, uprev_ref, cw_ref, cb_ref, lg_ref, lb_ref, o_ref, h_ref, *, tc):
    i = pl.program_id(1)

    def glu(u):
        u = u.astype(F32)
        return u[:, :CONV_CH] * jax.nn.sigmoid(u[:, CONV_CH:])

    h_ref[0:CONV_HALO, :] = jnp.where(i == 0, 0.0, glu(uprev_ref[0]))
    h_ref[CONV_HALO:CONV_HALO + tc, :] = glu(u_ref[0])
    first = CONV_HALO - (CONV_WIDTH - 1)
    for c in range(tc // CONV_ROWS):
        acc = jnp.zeros((CONV_ROWS, CONV_CH), F32)
        for w in range(CONV_WIDTH):
            r0 = c * CONV_ROWS + first + w
            acc = acc + h_ref[r0:r0 + CONV_ROWS, :] * cw_ref[w:w + 1, :]
        y = acc + cb_ref[...]
        mu = jnp.mean(y, axis=-1, keepdims=True)
        var = jnp.mean(jnp.square(y - mu), axis=-1, keepdims=True)
        y = (y - mu) * lax.rsqrt(var + EPS) * lg_ref[...] + lb_ref[...]
        o_ref[0, c * CONV_ROWS:(c + 1) * CONV_ROWS, :] = (y * jax.nn.sigmoid(y)).astype(o_ref.dtype)


def _conv(row, cw, cb, lg, lb, tc):
    B, S, _ = row.shape
    ucol = 0
    per = tc // CONV_HALO
    return pl.pallas_call(
        functools.partial(_conv_kernel, tc=tc),
        grid=(B, S // tc),
        in_specs=[
            pl.BlockSpec((1, tc, 2 * CONV_CH), lambda b, i: (b, i, ucol)),
            pl.BlockSpec((1, CONV_HALO, 2 * CONV_CH), lambda b, i: (b, jnp.maximum(i * per - 1, 0), ucol)),
            _const_spec((CONV_WIDTH, CONV_CH)),
            _const_spec((1, CONV_CH)),
            _const_spec((1, CONV_CH)),
            _const_spec((1, CONV_CH)),
        ],
        out_specs=pl.BlockSpec((1, tc, CONV_CH), lambda b, i: (b, i, 0)),
        out_shape=jax.ShapeDtypeStruct((B, S, CONV_CH), BF16),
        scratch_shapes=[pltpu.VMEM((CONV_HALO + tc, CONV_CH), F32)],
        compiler_params=_params("parallel", "parallel"),
        name="conformer_conv",
    )(row, row, cw, cb, lg, lb)


def _merge_kernel(x_ref, g_ref, wg_ref, osb_ref, odf_ref, ocv_ref, wsb_ref, wdf_ref, wcv_ref, wout_ref, o_ref):
    x = x_ref[...]
    D = x.shape[1]
    xb = _rmsnorm(x, g_ref[...]).astype(BF16)
    y = None
    for n, (o_br, w_br) in enumerate(((osb_ref, wsb_ref), (odf_ref, wdf_ref), (ocv_ref, wcv_ref))):
        gate = jax.nn.sigmoid(_dot(xb, wg_ref[:, n * D:(n + 1) * D]))
        t = gate * _dot(o_br[...], w_br[...])
        y = t if y is None else y + t
    o_ref[...] = x + _dot(y.astype(BF16), wout_ref[...])


def _merge(x2, g, wg, osb, odf, ocv, wsb, wdf, wcv, wout, tm):
    N, D = x2.shape
    W = osb.shape[1]
    tok = lambda w: pl.BlockSpec((tm, w), lambda i: (i, 0))
    return pl.pallas_call(
        _merge_kernel,
        grid=(N // tm,),
        in_specs=[tok(D), _const_spec((1, D)), _const_spec((D, 3 * D)), tok(W), tok(W), tok(W),
                  _const_spec((W, D)), _const_spec((W, D)), _const_spec((W, D)), _const_spec((D, D))],
        out_specs=tok(D),
        out_shape=jax.ShapeDtypeStruct((N, D), F32),
        compiler_params=_params("parallel"),
        name="gated_merge",
    )(x2, g, wg, osb, odf, ocv, wsb, wdf, wcv, wout)


def _memkv_kernel(mem_ref, g_ref, wkv_ref, o_ref):
    mb = _rmsnorm(mem_ref[0], g_ref[...]).astype(BF16)
    o_ref[0] = _dot(mb, wkv_ref[...]).astype(o_ref.dtype)


def _memkv(mem, g, wkv):
    B, M, D = mem.shape
    return pl.pallas_call(
        _memkv_kernel,
        grid=(B,),
        in_specs=[pl.BlockSpec((1, M, D), lambda b: (b, 0, 0)), _const_spec((1, D)), _const_spec((D, 2 * XA_WIDTH))],
        out_specs=pl.BlockSpec((1, M, 2 * XA_WIDTH), lambda b: (b, 0, 0)),
        out_shape=jax.ShapeDtypeStruct((B, M, 2 * XA_WIDTH), BF16),
        compiler_params=_params("parallel"),
        name="mem_kv",
    )(mem, g, wkv)


def _xattn_kernel(x_ref, g_ref, wq_ref, kv_ref, wo_ref, o_ref):
    x = x_ref[0]
    xb = _rmsnorm(x, g_ref[...]).astype(BF16)
    q = _dot(xb, wq_ref[...]).astype(BF16)
    heads = []
    for h in range(XA_HEADS):
        hs = slice(h * XA_HEAD_DIM, (h + 1) * XA_HEAD_DIM)
        k = kv_ref[0, :, hs]
        v = kv_ref[0, :, XA_WIDTH + h * XA_HEAD_DIM:XA_WIDTH + (h + 1) * XA_HEAD_DIM]
        s = _dot_nt(q[:, hs], k) * (XA_HEAD_DIM ** -0.5)
        p = jnp.exp(s - jnp.max(s, axis=1, keepdims=True))
        p = p / jnp.sum(p, axis=1, keepdims=True)
        heads.append(_dot(p.astype(BF16), v).astype(BF16))
    o = jnp.concatenate(heads, axis=1)
    o_ref[0] = x + _dot(o, wo_ref[...])


def _xattn(x, g, wq, kv, wo, tm):
    B, S, D = x.shape
    M = kv.shape[1]
    return pl.pallas_call(
        _xattn_kernel,
        grid=(B, S // tm),
        in_specs=[
            pl.BlockSpec((1, tm, D), lambda b, i: (b, i, 0)),
            _const_spec((1, D)),
            _const_spec((D, XA_WIDTH)),
            pl.BlockSpec((1, M, 2 * XA_WIDTH), lambda b, i: (b, 0, 0)),
            _const_spec((XA_WIDTH, D)),
        ],
        out_specs=pl.BlockSpec((1, tm, D), lambda b, i: (b, i, 0)),
        out_shape=jax.ShapeDtypeStruct((B, S, D), F32),
        compiler_params=_params("parallel", "parallel"),
        name="cross_attention",
    )(x, g, wq, kv, wo)


def _mlp_kernel(x_ref, g_ref, wup_ref, wdown_ref, gf_ref, o_ref, *, fc, final):
    x = x_ref[...]
    xb = _rmsnorm(x, g_ref[...]).astype(BF16)
    acc = x
    for c in range(wup_ref.shape[1] // fc):
        cs = slice(c * fc, (c + 1) * fc)
        h = jnp.square(jnp.maximum(_dot(xb, wup_ref[:, cs]), 0.0)).astype(BF16)
        acc = acc + _dot(h, wdown_ref[cs, :])
    o_ref[...] = _rmsnorm(acc, gf_ref[...]) if final else acc


def _mlp(x2, g, wup, wdown, gf, tm, final):
    N, D = x2.shape
    F = wup.shape[1]
    return pl.pallas_call(
        functools.partial(_mlp_kernel, fc=1024, final=final),
        grid=(N // tm,),
        in_specs=[pl.BlockSpec((tm, D), lambda i: (i, 0)), _const_spec((1, D)),
                  _const_spec((D, F)), _const_spec((F, D)), _const_spec((1, D))],
        out_specs=pl.BlockSpec((tm, D), lambda i: (i, 0)),
        out_shape=jax.ShapeDtypeStruct((N, D), F32),
        compiler_params=_params("parallel"),
        name="sqrelu_mlp",
    )(x2, g, wup, wdown, gf)


def kernel(x, mem, rel_bias, ln_mix, w_in, diff_lambda, diff_subln, conv_w, conv_b, conv_ln_g, conv_ln_b,
           w_sb_proj, w_diff_proj, w_conv_proj, w_out, ln_xattn, ln_mem, xa_w_q, xa_w_kv, xa_w_o,
           ln_mlp, w_up, w_down, ln_final):
    B, S, D = x.shape
    depth = w_in.shape[0]
    blk = ATT_BLOCK
    tm = min(512, S)
    tc = min(256, S)
    assert S % ATT_QBLOCK == 0 and S % blk == 0 and S % tm == 0 and S % tc == 0 and D == 1024

    row2 = lambda a: a.reshape(1, -1).astype(F32)
    tri = jnp.asarray(np.triu(np.ones((blk, blk), np.float32)), BF16)
    bias = _bias_tiles(rel_bias, blk)
    scale = HEAD_DIM ** -0.5

    for l in range(depth):
        w = w_in[l]
        sbq, sbk, sbv = w[:, 0:512] * scale, w[:, 512:1024], w[:, 1024:1536]
        dfq, dfk, dfv = w[:, 1536:2048] * scale, w[:, 2048:2560], w[:, 2560:3072]
        wrow = jnp.concatenate([w[:, 3072:4096], sbk, dfk], axis=1).astype(BF16)
        wtr = jnp.concatenate([sbq, sbv, dfq, dfv], axis=1).T.astype(BF16)
        wg = w[:, 4096:].astype(BF16)

        row, tr = _inproj(x, row2(ln_mix[l]), wrow, wtr, tm)
        o_sb = _sb_attention(row, tr, tri)
        lam_init = 0.8 - 0.6 * math.exp(-0.3 * l)
        o_df = _diff_attention(row, tr, bias, diff_lambda[l].astype(F32), row2(diff_subln[l]), lam_init)
        o_cv = _conv(row, conv_w[l].reshape(CONV_WIDTH, CONV_CH).astype(F32), row2(conv_b[l]),
                     row2(conv_ln_g[l]), row2(conv_ln_b[l]), tc)

        N = B * S
        x2 = _merge(x.reshape(N, D), row2(ln_mix[l]), wg,
                    o_sb.reshape(N, -1), o_df.reshape(N, -1), o_cv.reshape(N, -1),
                    w_sb_proj[l].astype(BF16), w_diff_proj[l].astype(BF16), w_conv_proj[l].astype(BF16),
                    w_out[l].astype(BF16), tm)

        kv = _memkv(mem, row2(ln_mem[l]), xa_w_kv[l].astype(BF16))
        x3 = _xattn(x2.reshape(B, S, D), row2(ln_xattn[l]),
                    xa_w_q[l].astype(BF16), kv, xa_w_o[l].astype(BF16), tm)

        x = _mlp(x3.reshape(N, D), row2(ln_mlp[l]), w_up[l].astype(BF16), w_down[l].astype(BF16),
                 row2(ln_final), tm, final=(l == depth - 1)).reshape(B, S, D)
    return x
```

```python
import functools
import math

import numpy as np
import jax
import jax.numpy as jnp
from jax import lax
from jax.experimental import pallas as pl
from jax.experimental.pallas import tpu as pltpu

F32 = jnp.float32
BF16 = jnp.bfloat16

HEAD_DIM = 64
SB_WIDTH = 512
DIFF_WIDTH = 512
DIFF_HEADS = 4
CONV_CH = 512
CONV_WIDTH = 31
XA_HEADS = 4
XA_HEAD_DIM = 128
XA_WIDTH = XA_HEADS * XA_HEAD_DIM
NUM_BUCKETS = 32
MAX_EXACT = NUM_BUCKETS // 2
MAX_DISTANCE = 128
EPS = 1e-6
NEG_INF = -1e30

LANES = 128
SUBLANES = 8
ATT_BLOCK = 256
CONV_HALO = 32
CONV_ROWS = 64
VMEM_LIMIT = 56 * 1024 * 1024

SB_QBLOCK = 512
DF_QBLOCK = 1024
SB_EXIT = 110.0
LOG2E = 1.4426950408889634
ONES_ROWS = 16

ROW_COLS = 2 * CONV_CH + 2 * 512
ROW_SBK, ROW_DFK = 8, 12
TR_ROWS = 3 * 512
TR_SBQ, TR_SBV, TR_DFQ = 0, 4, 8
VA_ROWS = LANES + ONES_ROWS


def _params(*sem):
    return pltpu.CompilerParams(dimension_semantics=sem, vmem_limit_bytes=VMEM_LIMIT)


def _const_spec(shape):
    nd = len(shape)
    return pl.BlockSpec(shape, lambda *_: (0,) * nd)


def _rmsnorm(x, g):
    return x * lax.rsqrt(jnp.mean(x * x, axis=-1, keepdims=True) + EPS) * g


def _dot(a, b):
    return jnp.dot(a, b, preferred_element_type=F32)


def _dot_nt(a, b):
    return lax.dot_general(a, b, (((1,), (1,)), ((), ())), preferred_element_type=F32)


def _inproj_kernel(x_ref, g_ref, wrow_ref, wtr_ref, row_ref, tr_ref, va_ref):
    xb = _rmsnorm(x_ref[0], g_ref[...]).astype(BF16)
    tm = xb.shape[0]
    for c in range(ROW_COLS // 512):
        cs = slice(c * 512, (c + 1) * 512)
        row_ref[0, :, cs] = _dot(xb, wrow_ref[:, cs]).astype(BF16)
    for c in range(TR_ROWS // 512):
        cs = slice(c * 512, (c + 1) * 512)
        tr_ref[0, cs, :] = _dot_nt(wtr_ref[cs, :], xb).astype(BF16)
    dfv = _dot_nt(wtr_ref[TR_ROWS:, :], xb).astype(BF16)
    for h in range(DIFF_HEADS):
        va_ref[0, h * VA_ROWS:h * VA_ROWS + LANES, :] = dfv[h * LANES:(h + 1) * LANES, :]
        va_ref[0, h * VA_ROWS + LANES:(h + 1) * VA_ROWS, :] = jnp.ones((ONES_ROWS, tm), BF16)


def _inproj(x, g, wrow, wtr, tm):
    B, S, D = x.shape
    return pl.pallas_call(
        _inproj_kernel,
        grid=(B, S // tm),
        in_specs=[
            pl.BlockSpec((1, tm, D), lambda b, i: (b, i, 0)),
            _const_spec((1, D)),
            _const_spec((D, ROW_COLS)),
            _const_spec((TR_ROWS + DIFF_WIDTH, D)),
        ],
        out_specs=[
            pl.BlockSpec((1, tm, ROW_COLS), lambda b, i: (b, i, 0)),
            pl.BlockSpec((1, TR_ROWS, tm), lambda b, i: (b, 0, i)),
            pl.BlockSpec((1, DIFF_HEADS * VA_ROWS, tm), lambda b, i: (b, 0, i)),
        ],
        out_shape=[
            jax.ShapeDtypeStruct((B, S, ROW_COLS), BF16),
            jax.ShapeDtypeStruct((B, TR_ROWS, S), BF16),
            jax.ShapeDtypeStruct((B, DIFF_HEADS * VA_ROWS, S), BF16),
        ],
        compiler_params=_params("parallel", "parallel"),
        name="inproj",
    )(x, g, wrow, wtr)


def _knorm_kernel(k_ref, sel_ref, o_ref):
    k = k_ref[0].astype(F32)
    n2 = _dot((k * k).astype(BF16), sel_ref[...])
    o_ref[0] = jnp.sqrt(jnp.max(n2, axis=0, keepdims=True))


def _key_norm_max(row):
    B, S, _ = row.shape
    sel = np.zeros((SB_WIDTH, LANES), np.float32)
    sel[np.arange(SB_WIDTH), np.arange(SB_WIDTH) // HEAD_DIM] = 1.0
    return pl.pallas_call(
        _knorm_kernel,
        grid=(B,),
        in_specs=[pl.BlockSpec((1, S, SB_WIDTH), lambda b: (b, 0, ROW_SBK * LANES // SB_WIDTH)),
                  _const_spec((SB_WIDTH, LANES))],
        out_specs=pl.BlockSpec((1, 1, LANES), lambda b: (b, 0, 0)),
        out_shape=jax.ShapeDtypeStruct((B, 1, LANES), F32),
        compiler_params=_params("parallel"),
        name="key_norm_max",
    )(row, jnp.asarray(sel, BF16))


def _sb_kernel(qt_ref, k_ref, vt_ref, tri_ref, kn_ref, o_ref):
    kb, qb = ATT_BLOCK, SB_QBLOCK
    nq = qb // kb
    p, i = pl.program_id(1), pl.program_id(2)
    qt = qt_ref[0]
    sub = lax.broadcasted_iota(jnp.int32, qt.shape, 0)
    zero = jnp.zeros_like(qt)
    qtm = (jnp.where(sub < HEAD_DIM, qt, zero), jnp.where(sub >= HEAD_DIM, qt, zero))
    tri = tri_ref[...]

    lane = lax.broadcasted_iota(jnp.int32, (1, LANES), 1)
    kn = kn_ref[0]
    zbound = []
    for h in range(2):
        knh = jnp.max(jnp.where(lane == 2 * p + h, kn, 0.0), axis=1, keepdims=True)
        qf = qtm[h].astype(F32)
        zbound.append(jnp.sqrt(jnp.sum(qf * qf, axis=0, keepdims=True)) * (knh * 1.01))

    def tiles(j, st, lo, diag):
        n = qb - lo
        ks = pl.ds(pl.multiple_of(j * kb, kb), kb)
        k = k_ref[0, ks, :]
        vt = vt_ref[0, :, ks]
        carry, acc = (st[0], st[2]), (st[1], st[3])
        z = [_dot(k, qtm[h][:, lo:]) for h in range(2)]
        sp = [jnp.maximum(z[h], 0.0) + jnp.log(1.0 + jnp.exp2(jnp.abs(z[h]) * -LOG2E)) for h in range(2)]
        if diag:
            strict = (lax.broadcasted_iota(jnp.int32, (kb, n), 0) < lax.broadcasted_iota(jnp.int32, (kb, n), 1))
            sp = [jnp.where(strict, s, 0.0) for s in sp]
        c = [_dot(tri, sp[h].astype(BF16)) + carry[h][:, lo:] for h in range(2)]
        a = [jnp.exp(z[h] - c[h]) for h in range(2)]
        if diag:
            a = [jnp.where(strict, x, 0.0) for x in a]
        pv = [_dot(vt[h * HEAD_DIM:(h + 1) * HEAD_DIM, :], a[h].astype(BF16)) for h in range(2)]
        out = []
        for h in range(2):
            cn, an = c[h][0:1, :], acc[h][:, lo:] + pv[h]
            if lo:
                cn = jnp.concatenate([carry[h][:, :lo], cn], axis=1)
                an = jnp.concatenate([acc[h][:, :lo], an], axis=1)
            out += [cn, an]
        return tuple(out)

    zc = jnp.zeros((1, qb), F32)
    za = jnp.zeros((HEAD_DIM, qb), F32)
    st = (zc, za, zc, za)
    for r in reversed(range(nq)):
        st = tiles(nq * i + r, st, r * kb, True)

    def body(s):
        new = tiles(nq * i - 1 - s[0], s[2:], 0, False)
        slack = jnp.minimum(new[0] - zbound[0], new[2] - zbound[1])
        return (s[0] + 1, (jnp.min(slack) <= SB_EXIT).astype(jnp.int32)) + new

    st = lax.while_loop(lambda s: jnp.logical_and(s[0] < nq * i, s[1] > 0), body,
                        (jnp.int32(0), jnp.int32(1)) + st)[2:]
    o_ref[0] = jnp.concatenate([st[1], st[3]], axis=0).T.astype(o_ref.dtype)


def _sb_attention(row, tr, tri, kn):
    B, S, _ = row.shape
    qb = SB_QBLOCK
    return pl.pallas_call(
        _sb_kernel,
        grid=(B, SB_WIDTH // LANES, S // qb),
        in_specs=[
            pl.BlockSpec((1, LANES, qb), lambda b, p, i: (b, TR_SBQ + p, i)),
            pl.BlockSpec((1, S, LANES), lambda b, p, i: (b, 0, ROW_SBK + p)),
            pl.BlockSpec((1, LANES, S), lambda b, p, i: (b, TR_SBV + p, 0)),
            _const_spec((ATT_BLOCK, ATT_BLOCK)),
            pl.BlockSpec((1, 1, LANES), lambda b, p, i: (b, 0, 0)),
        ],
        out_specs=pl.BlockSpec((1, qb, LANES), lambda b, p, i: (b, i, p)),
        out_shape=jax.ShapeDtypeStruct((B, S, SB_WIDTH), BF16),
        compiler_params=_params("parallel", "parallel", "arbitrary"),
        name="sb_attention",
    )(tr, row, tr, tri, kn)


def _bucket_tiles(blk):
    r = np.arange(blk)[None, :] - np.arange(blk)[:, None]
    rel = np.stack([r, r + blk, r + 2 * blk]).astype(np.int64)
    n = np.maximum(rel, 0)
    nf = np.maximum(n, 1).astype(np.float32)
    large = MAX_EXACT + (np.log(nf / np.float32(MAX_EXACT)) / np.float32(math.log(MAX_DISTANCE / MAX_EXACT))
                         * np.float32(NUM_BUCKETS - MAX_EXACT)).astype(np.int32)
    large = np.minimum(large, NUM_BUCKETS - 1)
    idx = np.where(n < MAX_EXACT, n, large).astype(np.int32)
    assert (idx[2] == NUM_BUCKETS - 1).all() and blk >= MAX_DISTANCE
    return idx


def _bias_kernel(table_ref, idx_ref, o_ref):
    h = pl.program_id(0)
    idx = idx_ref[0]
    acc = jnp.zeros(idx.shape, F32)
    for b in range(NUM_BUCKETS):
        acc = jnp.where(idx == b, table_ref[b, h], acc)
    o_ref[0, 0] = acc


def _bias_tiles(rel_bias, blk):
    idx = jnp.asarray(_bucket_tiles(blk))
    return pl.pallas_call(
        _bias_kernel,
        grid=(DIFF_HEADS, 3),
        in_specs=[
            pl.BlockSpec(memory_space=pltpu.SMEM),
            pl.BlockSpec((1, blk, blk), lambda h, d: (d, 0, 0)),
        ],
        out_specs=pl.BlockSpec((1, 1, blk, blk), lambda h, d: (h, d, 0, 0)),
        out_shape=jax.ShapeDtypeStruct((DIFF_HEADS, 3, blk, blk), F32),
        name="bias_tiles",
    )(rel_bias.astype(F32), idx)


def _diff_kernel(qt_ref, k_ref, va_ref, bias_ref, lam_ref, subln_ref, o_ref, *, lam_init):
    kb, qb = ATT_BLOCK, DF_QBLOCK
    nq = qb // kb
    i = pl.program_id(2)
    qt = qt_ref[0]
    sub = lax.broadcasted_iota(jnp.int32, qt.shape, 0)
    zero = jnp.zeros_like(qt)
    qtm = (jnp.where(sub < HEAD_DIM, qt, zero), jnp.where(sub >= HEAD_DIM, qt, zero))
    far = bias_ref[0, 2][0:1, 0:1]

    def tiles(j, st, lo, bias, mask):
        ks = pl.ds(pl.multiple_of(j * kb, kb), kb)
        k = k_ref[0, ks, :]
        vta = va_ref[0, :, ks]
        z = [_dot(k, qtm[m][:, lo:]) for m in range(2)]
        if bias is not None:
            z = [x + bias for x in z]
        if mask is not None:
            z = [jnp.where(mask, x, NEG_INF) for x in z]
        mold = [st[m][0][:, lo:] for m in range(2)]
        top = [jnp.max(x, axis=0, keepdims=True) for x in z]
        if bias is None:
            top = [t + far for t in top]
        mnew = [jnp.maximum(mold[m], top[m]) for m in range(2)]
        alpha = [jnp.exp(mold[m] - mnew[m]) for m in range(2)]
        shift = [mn - far for mn in mnew] if bias is None else mnew
        p = [jnp.exp(z[m] - shift[m]) for m in range(2)]
        pv = [_dot(vta, p[m].astype(BF16)) for m in range(2)]
        out = []
        for m in range(2):
            new = (mnew[m], alpha[m] * st[m][1][:, lo:] + pv[m][LANES:LANES + 1, :],
                   alpha[m] * st[m][2][:, lo:] + pv[m][:LANES, :])
            if lo:
                new = tuple(jnp.concatenate([old[:, :lo], x], axis=1) for old, x in zip(st[m], new))
            out.append(new)
        return tuple(out)

    init = (jnp.full((1, qb), NEG_INF, F32), jnp.zeros((1, qb), F32), jnp.zeros((LANES, qb), F32))
    st = (init, init)
    for r in reversed(range(nq)):
        n = qb - r * kb
        bias = jnp.concatenate([bias_ref[0, min(c - r, 2)] for c in range(r, nq)], axis=1)
        causal = lax.broadcasted_iota(jnp.int32, (kb, n), 0) <= lax.broadcasted_iota(jnp.int32, (kb, n), 1)
        st = tiles(nq * i + r, st, r * kb, bias, causal)
    bias = jnp.concatenate([bias_ref[0, min(c + 1, 2)] for c in range(nq)], axis=1)
    st = tiles(jnp.maximum(nq * i - 1, 0), st, 0, bias, i > 0)
    st = lax.fori_loop(0, jnp.maximum(nq * i - 1, 0), lambda jj, s: tiles(nq * i - 2 - jj, s, 0, None, None), st)

    lp = lam_ref[...]
    lam = (jnp.exp(jnp.sum(lp[0:1] * lp[1:2], axis=1, keepdims=True))
           - jnp.exp(jnp.sum(lp[2:3] * lp[3:4], axis=1, keepdims=True)) + lam_init)
    (_, l0, acc0), (_, l1, acc1) = st
    o = (acc0 / l0 - lam * (acc1 / l1)).T
    o_ref[0] = (_rmsnorm(o, subln_ref[...]) * (1.0 - lam_init)).astype(o_ref.dtype)


def _diff_attention(row, tr, va, bias, lam_p, subln, lam_init):
    B, S, _ = row.shape
    kb, qb = ATT_BLOCK, DF_QBLOCK
    return pl.pallas_call(
        functools.partial(_diff_kernel, lam_init=lam_init),
        grid=(B, DIFF_HEADS, S // qb),
        in_specs=[
            pl.BlockSpec((1, LANES, qb), lambda b, h, i: (b, TR_DFQ + h, i)),
            pl.BlockSpec((1, S, LANES), lambda b, h, i: (b, 0, ROW_DFK + h)),
            pl.BlockSpec((1, VA_ROWS, S), lambda b, h, i: (b, h, 0)),
            pl.BlockSpec((1, 3, kb, kb), lambda b, h, i: (h, 0, 0, 0)),
            _const_spec((4, HEAD_DIM)),
            _const_spec((1, 2 * HEAD_DIM)),
        ],
        out_specs=pl.BlockSpec((1, qb, LANES), lambda b, h, i: (b, i, h)),
        out_shape=jax.ShapeDtypeStruct((B, S, DIFF_WIDTH), BF16),
        compiler_params=_params("parallel", "parallel", "arbitrary"),
        name="diff_attention",
    )(tr, row, va, bias, lam_p, subln)


def _conv_kernel(u_ref, uprev_ref, cw_ref, cb_ref, lg_ref, lb_ref, o_ref, h_ref, hs_ref, *, tc):
    i = pl.program_id(1)

    def glu(u):
        u = u.astype(F32)
        return u[:, :CONV_CH] * jax.nn.sigmoid(u[:, CONV_CH:])

    h_ref[0:CONV_HALO, :] = jnp.where(i == 0, 0.0, glu(uprev_ref[0]))
    h_ref[CONV_HALO:CONV_HALO + tc, :] = glu(u_ref[0])
    for b in range(1, SUBLANES):
        hs_ref[b - 1] = h_ref[b:b + tc + CONV_HALO - SUBLANES, :]
    first = CONV_HALO - (CONV_WIDTH - 1)
    for c in range(tc // CONV_ROWS):
        acc = jnp.zeros((CONV_ROWS, CONV_CH), F32)
        for w in range(CONV_WIDTH):
            a, b = divmod(first + w, SUBLANES)
            r0 = c * CONV_ROWS + a * SUBLANES
            src = h_ref[r0:r0 + CONV_ROWS, :] if b == 0 else hs_ref[b - 1, r0:r0 + CONV_ROWS, :]
            acc = acc + src * cw_ref[w:w + 1, :]
        y = acc + cb_ref[...]
        mu = jnp.mean(y, axis=-1, keepdims=True)
        var = jnp.mean(jnp.square(y - mu), axis=-1, keepdims=True)
        y = (y - mu) * lax.rsqrt(var + EPS) * lg_ref[...] + lb_ref[...]
        o_ref[0, c * CONV_ROWS:(c + 1) * CONV_ROWS, :] = (y * jax.nn.sigmoid(y)).astype(o_ref.dtype)


def _conv(row, cw, cb, lg, lb, tc):
    B, S, _ = row.shape
    ucol = 0
    per = tc // CONV_HALO
    return pl.pallas_call(
        functools.partial(_conv_kernel, tc=tc),
        grid=(B, S // tc),
        in_specs=[
            pl.BlockSpec((1, tc, 2 * CONV_CH), lambda b, i: (b, i, ucol)),
            pl.BlockSpec((1, CONV_HALO, 2 * CONV_CH), lambda b, i: (b, jnp.maximum(i * per - 1, 0), ucol)),
            _const_spec((CONV_WIDTH, CONV_CH)),
            _const_spec((1, CONV_CH)),
            _const_spec((1, CONV_CH)),
            _const_spec((1, CONV_CH)),
        ],
        out_specs=pl.BlockSpec((1, tc, CONV_CH), lambda b, i: (b, i, 0)),
        out_shape=jax.ShapeDtypeStruct((B, S, CONV_CH), BF16),
        scratch_shapes=[pltpu.VMEM((CONV_HALO + tc, CONV_CH), F32),
                        pltpu.VMEM((SUBLANES - 1, CONV_HALO + tc - SUBLANES, CONV_CH), F32)],
        compiler_params=_params("parallel", "parallel"),
        name="conformer_conv",
    )(row, row, cw, cb, lg, lb)


def _merge_kernel(x_ref, g_ref, wg_ref, osb_ref, odf_ref, ocv_ref, wsb_ref, wdf_ref, wcv_ref, wout_ref, o_ref):
    x = x_ref[...]
    D = x.shape[1]
    xb = _rmsnorm(x, g_ref[...]).astype(BF16)
    y = None
    for n, (o_br, w_br) in enumerate(((osb_ref, wsb_ref), (odf_ref, wdf_ref), (ocv_ref, wcv_ref))):
        gate = jax.nn.sigmoid(_dot(xb, wg_ref[:, n * D:(n + 1) * D]))
        t = gate * _dot(o_br[...], w_br[...])
        y = t if y is None else y + t
    o_ref[...] = x + _dot(y.astype(BF16), wout_ref[...])


def _merge(x2, g, wg, osb, odf, ocv, wsb, wdf, wcv, wout, tm):
    N, D = x2.shape
    W = osb.shape[1]
    tok = lambda w: pl.BlockSpec((tm, w), lambda i: (i, 0))
    return pl.pallas_call(
        _merge_kernel,
        grid=(N // tm,),
        in_specs=[tok(D), _const_spec((1, D)), _const_spec((D, 3 * D)), tok(W), tok(W), tok(W),
                  _const_spec((W, D)), _const_spec((W, D)), _const_spec((W, D)), _const_spec((D, D))],
        out_specs=tok(D),
        out_shape=jax.ShapeDtypeStruct((N, D), F32),
        compiler_params=_params("parallel"),
        name="gated_merge",
    )(x2, g, wg, osb, odf, ocv, wsb, wdf, wcv, wout)


def _memkv_kernel(mem_ref, g_ref, wkv_ref, o_ref):
    mb = _rmsnorm(mem_ref[0], g_ref[...]).astype(BF16)
    o_ref[0] = _dot(mb, wkv_ref[...]).astype(o_ref.dtype)


def _memkv(mem, g, wkv):
    B, M, D = mem.shape
    return pl.pallas_call(
        _memkv_kernel,
        grid=(B,),
        in_specs=[pl.BlockSpec((1, M, D), lambda b: (b, 0, 0)), _const_spec((1, D)), _const_spec((D, 2 * XA_WIDTH))],
        out_specs=pl.BlockSpec((1, M, 2 * XA_WIDTH), lambda b: (b, 0, 0)),
        out_shape=jax.ShapeDtypeStruct((B, M, 2 * XA_WIDTH), BF16),
        compiler_params=_params("parallel"),
        name="mem_kv",
    )(mem, g, wkv)


def _xattn_kernel(x_ref, g_ref, wq_ref, kv_ref, wo_ref, o_ref):
    x = x_ref[0]
    xb = _rmsnorm(x, g_ref[...]).astype(BF16)
    q = _dot(xb, wq_ref[...]).astype(BF16)
    heads = []
    for h in range(XA_HEADS):
        hs = slice(h * XA_HEAD_DIM, (h + 1) * XA_HEAD_DIM)
        k = kv_ref[0, :, hs]
        v = kv_ref[0, :, XA_WIDTH + h * XA_HEAD_DIM:XA_WIDTH + (h + 1) * XA_HEAD_DIM]
        s = _dot_nt(q[:, hs], k) * (XA_HEAD_DIM ** -0.5)
        p = jnp.exp(s - jnp.max(s, axis=1, keepdims=True))
        p = p / jnp.sum(p, axis=1, keepdims=True)
        heads.append(_dot(p.astype(BF16), v).astype(BF16))
    o = jnp.concatenate(heads, axis=1)
    o_ref[0] = x + _dot(o, wo_ref[...])


def _xattn(x, g, wq, kv, wo, tm):
    B, S, D = x.shape
    M = kv.shape[1]
    return pl.pallas_call(
        _xattn_kernel,
        grid=(B, S // tm),
        in_specs=[
            pl.BlockSpec((1, tm, D), lambda b, i: (b, i, 0)),
            _const_spec((1, D)),
            _const_spec((D, XA_WIDTH)),
            pl.BlockSpec((1, M, 2 * XA_WIDTH), lambda b, i: (b, 0, 0)),
            _const_spec((XA_WIDTH, D)),
        ],
        out_specs=pl.BlockSpec((1, tm, D), lambda b, i: (b, i, 0)),
        out_shape=jax.ShapeDtypeStruct((B, S, D), F32),
        compiler_params=_params("parallel", "parallel"),
        name="cross_attention",
    )(x, g, wq, kv, wo)


def _mlp_kernel(x_ref, g_ref, wup_ref, wdown_ref, gf_ref, o_ref, *, fc, final):
    x = x_ref[...]
    xb = _rmsnorm(x, g_ref[...]).astype(BF16)
    acc = x
    for c in range(wup_ref.shape[1] // fc):
        cs = slice(c * fc, (c + 1) * fc)
        h = jnp.square(jnp.maximum(_dot(xb, wup_ref[:, cs]), 0.0)).astype(BF16)
        acc = acc + _dot(h, wdown_ref[cs, :])
    o_ref[...] = _rmsnorm(acc, gf_ref[...]) if final else acc


def _mlp(x2, g, wup, wdown, gf, tm, final):
    N, D = x2.shape
    F = wup.shape[1]
    return pl.pallas_call(
        functools.partial(_mlp_kernel, fc=1024, final=final),
        grid=(N // tm,),
        in_specs=[pl.BlockSpec((tm, D), lambda i: (i, 0)), _const_spec((1, D)),
                  _const_spec((D, F)), _const_spec((F, D)), _const_spec((1, D))],
        out_specs=pl.BlockSpec((tm, D), lambda i: (i, 0)),
        out_shape=jax.ShapeDtypeStruct((N, D), F32),
        compiler_params=_params("parallel"),
        name="sqrelu_mlp",
    )(x2, g, wup, wdown, gf)


def kernel(x, mem, rel_bias, ln_mix, w_in, diff_lambda, diff_subln, conv_w, conv_b, conv_ln_g, conv_ln_b,
           w_sb_proj, w_diff_proj, w_conv_proj, w_out, ln_xattn, ln_mem, xa_w_q, xa_w_kv, xa_w_o,
           ln_mlp, w_up, w_down, ln_final):
    B, S, D = x.shape
    depth = w_in.shape[0]
    blk = ATT_BLOCK
    tm = min(512, S)
    tc = min(256, S)
    assert S % SB_QBLOCK == 0 and S % DF_QBLOCK == 0 and S % tm == 0 and S % tc == 0 and D == 1024

    row2 = lambda a: a.reshape(1, -1).astype(F32)
    tri = jnp.asarray(np.triu(np.ones((blk, blk), np.float32)), BF16)
    bias = _bias_tiles(rel_bias, blk)
    scale = HEAD_DIM ** -0.5

    for l in range(depth):
        w = w_in[l]
        sbq, sbk, sbv = w[:, 0:512] * scale, w[:, 512:1024], w[:, 1024:1536]
        dfq, dfk, dfv = w[:, 1536:2048] * scale, w[:, 2048:2560], w[:, 2560:3072]
        wrow = jnp.concatenate([w[:, 3072:4096], sbk, dfk], axis=1).astype(BF16)
        wtr = jnp.concatenate([sbq, sbv, dfq, dfv], axis=1).T.astype(BF16)
        wg = w[:, 4096:].astype(BF16)

        row, tr, va = _inproj(x, row2(ln_mix[l]), wrow, wtr, tm)
        o_sb = _sb_attention(row, tr, tri, _key_norm_max(row))
        lam_init = 0.8 - 0.6 * math.exp(-0.3 * l)
        o_df = _diff_attention(row, tr, va, bias, diff_lambda[l].astype(F32), row2(diff_subln[l]), lam_init)
        o_cv = _conv(row, conv_w[l].reshape(CONV_WIDTH, CONV_CH).astype(F32), row2(conv_b[l]),
                     row2(conv_ln_g[l]), row2(conv_ln_b[l]), tc)

        N = B * S
        x2 = _merge(x.reshape(N, D), row2(ln_mix[l]), wg,
                    o_sb.reshape(N, -1), o_df.reshape(N, -1), o_cv.reshape(N, -1),
                    w_sb_proj[l].astype(BF16), w_diff_proj[l].astype(BF16), w_conv_proj[l].astype(BF16),
                    w_out[l].astype(BF16), tm)

        kv = _memkv(mem, row2(ln_mem[l]), xa_w_kv[l].astype(BF16))
        x3 = _xattn(x2.reshape(B, S, D), row2(ln_xattn[l]),
                    xa_w_q[l].astype(BF16), kv, xa_w_o[l].astype(BF16), tm)

        x = _mlp(x3.reshape(N, D), row2(ln_mlp[l]), w_up[l].astype(BF16), w_down[l].astype(BF16),
                 row2(ln_final), tm, final=(l == depth - 1)).reshape(B, S, D)
    return x
```

```python
import functools
import math

import numpy as np
import jax
import jax.numpy as jnp
from jax import lax
from jax.experimental import pallas as pl
from jax.experimental.pallas import tpu as pltpu

F32 = jnp.float32
BF16 = jnp.bfloat16

HEAD_DIM = 64
SB_WIDTH = 512
DIFF_WIDTH = 512
DIFF_HEADS = 4
CONV_CH = 512
CONV_WIDTH = 31
XA_HEADS = 4
XA_HEAD_DIM = 128
XA_WIDTH = XA_HEADS * XA_HEAD_DIM
NUM_BUCKETS = 32
MAX_EXACT = NUM_BUCKETS // 2
MAX_DISTANCE = 128
EPS = 1e-6
NEG_INF = -1e30

LANES = 128
SUBLANES = 8
ATT_BLOCK = 256
CONV_HALO = 32
CONV_ROWS = 64
VMEM_LIMIT = 56 * 1024 * 1024

SB_QBLOCK = 512
DF_QBLOCK = 1024
SB_EXIT = 110.0
DF_UNROLL = 4
DF_SPREAD = 80.0
LOG2E = 1.4426950408889634
ONES_ROWS = 16

ROW_COLS = 2 * CONV_CH + 2 * 512
ROW_SBK, ROW_DFK = 8, 12
TR_ROWS = 3 * 512
TR_SBQ, TR_SBV, TR_DFQ = 0, 4, 8
VA_ROWS = LANES + ONES_ROWS


def _params(*sem):
    return pltpu.CompilerParams(dimension_semantics=sem, vmem_limit_bytes=VMEM_LIMIT)


def _const_spec(shape):
    nd = len(shape)
    return pl.BlockSpec(shape, lambda *_: (0,) * nd)


def _rmsnorm(x, g):
    return x * lax.rsqrt(jnp.mean(x * x, axis=-1, keepdims=True) + EPS) * g


def _dot(a, b):
    return jnp.dot(a, b, preferred_element_type=F32)


def _dot_nt(a, b):
    return lax.dot_general(a, b, (((1,), (1,)), ((), ())), preferred_element_type=F32)


def _inproj_kernel(x_ref, g_ref, wrow_ref, wtr_ref, row_ref, tr_ref, va_ref):
    xb = _rmsnorm(x_ref[0], g_ref[...]).astype(BF16)
    tm = xb.shape[0]
    for c in range(ROW_COLS // 512):
        cs = slice(c * 512, (c + 1) * 512)
        row_ref[0, :, cs] = _dot(xb, wrow_ref[:, cs]).astype(BF16)
    for c in range(TR_ROWS // 512):
        cs = slice(c * 512, (c + 1) * 512)
        tr_ref[0, cs, :] = _dot_nt(wtr_ref[cs, :], xb).astype(BF16)
    dfv = _dot_nt(wtr_ref[TR_ROWS:, :], xb).astype(BF16)
    for h in range(DIFF_HEADS):
        va_ref[0, h * VA_ROWS:h * VA_ROWS + LANES, :] = dfv[h * LANES:(h + 1) * LANES, :]
        va_ref[0, h * VA_ROWS + LANES:(h + 1) * VA_ROWS, :] = jnp.ones((ONES_ROWS, tm), BF16)


def _inproj(x, g, wrow, wtr, tm):
    B, S, D = x.shape
    return pl.pallas_call(
        _inproj_kernel,
        grid=(B, S // tm),
        in_specs=[
            pl.BlockSpec((1, tm, D), lambda b, i: (b, i, 0)),
            _const_spec((1, D)),
            _const_spec((D, ROW_COLS)),
            _const_spec((TR_ROWS + DIFF_WIDTH, D)),
        ],
        out_specs=[
            pl.BlockSpec((1, tm, ROW_COLS), lambda b, i: (b, i, 0)),
            pl.BlockSpec((1, TR_ROWS, tm), lambda b, i: (b, 0, i)),
            pl.BlockSpec((1, DIFF_HEADS * VA_ROWS, tm), lambda b, i: (b, 0, i)),
        ],
        out_shape=[
            jax.ShapeDtypeStruct((B, S, ROW_COLS), BF16),
            jax.ShapeDtypeStruct((B, TR_ROWS, S), BF16),
            jax.ShapeDtypeStruct((B, DIFF_HEADS * VA_ROWS, S), BF16),
        ],
        compiler_params=_params("parallel", "parallel"),
        name="inproj",
    )(x, g, wrow, wtr)


def _knorm_kernel(k_ref, sel_ref, o_ref):
    k = k_ref[0].astype(F32)
    n2 = _dot((k * k).astype(BF16), sel_ref[...])
    o_ref[0] = jnp.sqrt(jnp.max(n2, axis=0, keepdims=True))


def _key_norm_max(row):
    B, S, _ = row.shape
    width = SB_WIDTH + DIFF_WIDTH
    sel = np.zeros((width, LANES), np.float32)
    sel[np.arange(width), np.arange(width) // HEAD_DIM] = 1.0
    return pl.pallas_call(
        _knorm_kernel,
        grid=(B,),
        in_specs=[pl.BlockSpec((1, S, width), lambda b: (b, 0, ROW_SBK * LANES // width)),
                  _const_spec((width, LANES))],
        out_specs=pl.BlockSpec((1, 1, LANES), lambda b: (b, 0, 0)),
        out_shape=jax.ShapeDtypeStruct((B, 1, LANES), F32),
        compiler_params=_params("parallel"),
        name="key_norm_max",
    )(row, jnp.asarray(sel, BF16))


def _sb_kernel(qt_ref, k_ref, vt_ref, tri_ref, kn_ref, o_ref):
    kb, qb = ATT_BLOCK, SB_QBLOCK
    nq = qb // kb
    p, i = pl.program_id(1), pl.program_id(2)
    qt = qt_ref[0]
    sub = lax.broadcasted_iota(jnp.int32, qt.shape, 0)
    zero = jnp.zeros_like(qt)
    qtm = (jnp.where(sub < HEAD_DIM, qt, zero), jnp.where(sub >= HEAD_DIM, qt, zero))
    tri = tri_ref[...]

    lane = lax.broadcasted_iota(jnp.int32, (1, LANES), 1)
    kn = kn_ref[0]
    zbound = []
    for h in range(2):
        knh = jnp.max(jnp.where(lane == 2 * p + h, kn, 0.0), axis=1, keepdims=True)
        qf = qtm[h].astype(F32)
        zbound.append(jnp.sqrt(jnp.sum(qf * qf, axis=0, keepdims=True)) * (knh * 1.01))

    def tiles(jobs, st):
        carry, acc = [st[0], st[2]], [st[1], st[3]]
        ks = [pl.ds(pl.multiple_of(j * kb, kb), kb) for j, _, _ in jobs]
        units = [(n, h) for n in range(len(jobs)) for h in range(2)]
        z = {(n, h): _dot(k_ref[0, ks[n], :], qtm[h][:, jobs[n][1]:]) for n, h in units}
        sp = {u: jnp.maximum(z[u], 0.0) + jnp.log(1.0 + jnp.exp2(jnp.abs(z[u]) * -LOG2E)) for u in units}
        strict = {}
        for n, (_, lo, diag) in enumerate(jobs):
            if diag:
                strict[n] = (lax.broadcasted_iota(jnp.int32, (kb, qb - lo), 0)
                             < lax.broadcasted_iota(jnp.int32, (kb, qb - lo), 1))
                for h in range(2):
                    sp[n, h] = jnp.where(strict[n], sp[n, h], 0.0)
        cs = {u: _dot(tri, sp[u].astype(BF16)) for u in units}
        for n, (_, lo, diag) in enumerate(jobs):
            vt = vt_ref[0, :, ks[n]]
            c = [cs[n, h] + carry[h][:, lo:] for h in range(2)]
            a = [jnp.exp(z[n, h] - c[h]) for h in range(2)]
            if diag:
                a = [jnp.where(strict[n], x, 0.0) for x in a]
            pv = [_dot(vt[h * HEAD_DIM:(h + 1) * HEAD_DIM, :], a[h].astype(BF16)) for h in range(2)]
            for h in range(2):
                cn, an = c[h][0:1, :], acc[h][:, lo:] + pv[h]
                if lo:
                    cn = jnp.concatenate([carry[h][:, :lo], cn], axis=1)
                    an = jnp.concatenate([acc[h][:, :lo], an], axis=1)
                carry[h], acc[h] = cn, an
        return carry[0], acc[0], carry[1], acc[1]

    zc = jnp.zeros((1, qb), F32)
    za = jnp.zeros((HEAD_DIM, qb), F32)
    diagonal = [(nq * i + r, r * kb, True) for r in reversed(range(nq))]

    def unfinished(st):
        slack = jnp.minimum(st[0] - zbound[0], st[2] - zbound[1])
        return (jnp.min(slack) <= SB_EXIT).astype(jnp.int32)

    def body(s):
        new = tiles([(nq * i - 1 - s[0], 0, False)], s[2:])
        return (s[0] + 1, unfinished(new)) + new

    def with_older():
        st = tiles(diagonal + [(nq * i - 1, 0, False)], (zc, za, zc, za))
        return lax.while_loop(lambda s: jnp.logical_and(s[0] < nq * i, s[1] > 0), body,
                              (jnp.int32(1), unfinished(st)) + st)[2:]

    st = lax.cond(i > 0, with_older, lambda: tiles(diagonal, (zc, za, zc, za)))
    o_ref[0] = jnp.concatenate([st[1], st[3]], axis=0).T.astype(o_ref.dtype)


def _sb_attention(row, tr, tri, kn):
    B, S, _ = row.shape
    qb = SB_QBLOCK
    return pl.pallas_call(
        _sb_kernel,
        grid=(B, SB_WIDTH // LANES, S // qb),
        in_specs=[
            pl.BlockSpec((1, LANES, qb), lambda b, p, i: (b, TR_SBQ + p, i)),
            pl.BlockSpec((1, S, LANES), lambda b, p, i: (b, 0, ROW_SBK + p)),
            pl.BlockSpec((1, LANES, S), lambda b, p, i: (b, TR_SBV + p, 0)),
            _const_spec((ATT_BLOCK, ATT_BLOCK)),
            pl.BlockSpec((1, 1, LANES), lambda b, p, i: (b, 0, 0)),
        ],
        out_specs=pl.BlockSpec((1, qb, LANES), lambda b, p, i: (b, i, p)),
        out_shape=jax.ShapeDtypeStruct((B, S, SB_WIDTH), BF16),
        compiler_params=_params("parallel", "parallel", "arbitrary"),
        name="sb_attention",
    )(tr, row, tr, tri, kn)


def _bucket_tiles(blk):
    r = np.arange(blk)[None, :] - np.arange(blk)[:, None]
    rel = np.stack([r, r + blk, r + 2 * blk]).astype(np.int64)
    n = np.maximum(rel, 0)
    nf = np.maximum(n, 1).astype(np.float32)
    large = MAX_EXACT + (np.log(nf / np.float32(MAX_EXACT)) / np.float32(math.log(MAX_DISTANCE / MAX_EXACT))
                         * np.float32(NUM_BUCKETS - MAX_EXACT)).astype(np.int32)
    large = np.minimum(large, NUM_BUCKETS - 1)
    idx = np.where(n < MAX_EXACT, n, large).astype(np.int32)
    assert (idx[2] == NUM_BUCKETS - 1).all() and blk >= MAX_DISTANCE
    return idx


def _bias_kernel(table_ref, idx_ref, o_ref):
    h = pl.program_id(0)
    idx = idx_ref[0]
    acc = jnp.zeros(idx.shape, F32)
    for b in range(NUM_BUCKETS):
        acc = jnp.where(idx == b, table_ref[b, h], acc)
    o_ref[0, 0] = acc


def _bias_tiles(rel_bias, blk):
    idx = jnp.asarray(_bucket_tiles(blk))
    return pl.pallas_call(
        _bias_kernel,
        grid=(DIFF_HEADS, 3),
        in_specs=[
            pl.BlockSpec(memory_space=pltpu.SMEM),
            pl.BlockSpec((1, blk, blk), lambda h, d: (d, 0, 0)),
        ],
        out_specs=pl.BlockSpec((1, 1, blk, blk), lambda h, d: (h, d, 0, 0)),
        out_shape=jax.ShapeDtypeStruct((DIFF_HEADS, 3, blk, blk), F32),
        name="bias_tiles",
    )(rel_bias.astype(F32), idx)


def _diff_kernel(qt_ref, k_ref, va_ref, bias_ref, kn_ref, lam_ref, subln_ref, o_ref, *, lam_init):
    kb, qb = ATT_BLOCK, DF_QBLOCK
    nq = qb // kb
    h, i = pl.program_id(1), pl.program_id(2)
    qt = qt_ref[0]
    sub = lax.broadcasted_iota(jnp.int32, qt.shape, 0)
    zero = jnp.zeros_like(qt)
    qtm = (jnp.where(sub < HEAD_DIM, qt, zero), jnp.where(sub >= HEAD_DIM, qt, zero))
    far = bias_ref[0, 2][0:1, 0:1]

    def scores(j, lo):
        k = k_ref[0, pl.ds(pl.multiple_of(j * kb, kb), kb), :]
        return [_dot(k, qtm[m][:, lo:]) for m in range(2)]

    def update(z, j, st, lo, bias, mask, online):
        vta = va_ref[0, :, pl.ds(pl.multiple_of(j * kb, kb), kb)]
        if bias is not None:
            z = [x + bias for x in z]
        if mask is not None:
            z = [jnp.where(mask, x, NEG_INF) for x in z]
        mnew = [st[m][0][:, lo:] for m in range(2)]
        scale = None
        if online:
            top = [jnp.max(x, axis=0, keepdims=True) for x in z]
            if bias is None:
                top = [t + far for t in top]
            mold, mnew = mnew, [jnp.maximum(mnew[m], top[m]) for m in range(2)]
            scale = [jnp.exp(mold[m] - mnew[m]) for m in range(2)]
        shift = [mn - far for mn in mnew] if bias is None else mnew
        p = [jnp.exp(z[m] - shift[m]) for m in range(2)]
        pv = [_dot(vta, p[m].astype(BF16)) for m in range(2)]
        out = []
        for m in range(2):
            l, acc = st[m][1][:, lo:], st[m][2][:, lo:]
            if online:
                l, acc = scale[m] * l, scale[m] * acc
            new = (mnew[m], l + pv[m][LANES:LANES + 1, :], acc + pv[m][:LANES, :])
            if lo:
                new = tuple(jnp.concatenate([old[:, :lo], x], axis=1) for old, x in zip(st[m], new))
            out.append(new)
        return tuple(out)

    def run(stab, online):
        zl, za = jnp.zeros((1, qb), F32), jnp.zeros((LANES, qb), F32)
        st = ((stab[0], zl, za), (stab[1], zl, za))
        for r in reversed(range(nq)):
            n = qb - r * kb
            bias = jnp.concatenate([bias_ref[0, min(c - r, 2)] for c in range(r, nq)], axis=1)
            causal = lax.broadcasted_iota(jnp.int32, (kb, n), 0) <= lax.broadcasted_iota(jnp.int32, (kb, n), 1)
            st = update(scores(nq * i + r, r * kb), nq * i + r, st, r * kb, bias, causal, online)
        def older(st):
            bias = jnp.concatenate([bias_ref[0, min(c + 1, 2)] for c in range(nq)], axis=1)
            st = update(scores(nq * i - 1, 0), nq * i - 1, st, 0, bias, None, online)
            for d in range(2, DF_UNROLL + 1):
                st = update(scores(nq * i - d, 0), nq * i - d, st, 0, None, None, online)

            def body(jj, s):
                for d in range(DF_UNROLL):
                    j = nq * i - DF_UNROLL * (jj + 1) - 1 - d
                    s = update(scores(j, 0), j, s, 0, None, None, online)
                return s

            return lax.fori_loop(0, (nq * i) // DF_UNROLL - 1, body, st)

        return lax.cond(i > 0, older, lambda s: s, st)

    lane = lax.broadcasted_iota(jnp.int32, (1, LANES), 1)
    kn = kn_ref[0]
    allb = bias_ref[0]
    bmax = jnp.max(jnp.max(allb, axis=0), axis=(0, 1), keepdims=True)
    bmin = jnp.min(jnp.min(allb, axis=0), axis=(0, 1), keepdims=True)
    qk = []
    for m in range(2):
        knm = jnp.max(jnp.where(lane == SB_WIDTH // HEAD_DIM + 2 * h + m, kn, 0.0), axis=1, keepdims=True)
        qf = qtm[m].astype(F32)
        qk.append(jnp.sqrt(jnp.sum(qf * qf, axis=0, keepdims=True)) * (knm * 1.01))
    spread = 2.0 * jnp.maximum(qk[0], qk[1]) + (bmax - bmin)
    bound = [x + bmax for x in qk]
    neg = jnp.full((1, qb), NEG_INF, F32)
    st = lax.cond(jnp.max(spread) <= DF_SPREAD,
                  lambda: run(bound, False), lambda: run([neg, neg], True))

    lp = lam_ref[...]
    lam = (jnp.exp(jnp.sum(lp[0:1] * lp[1:2], axis=1, keepdims=True))
           - jnp.exp(jnp.sum(lp[2:3] * lp[3:4], axis=1, keepdims=True)) + lam_init)
    (_, l0, acc0), (_, l1, acc1) = st
    o = (acc0 / l0 - lam * (acc1 / l1)).T
    o_ref[0] = (_rmsnorm(o, subln_ref[...]) * (1.0 - lam_init)).astype(o_ref.dtype)


def _diff_attention(row, tr, va, bias, kn, lam_p, subln, lam_init):
    B, S, _ = row.shape
    kb, qb = ATT_BLOCK, DF_QBLOCK
    return pl.pallas_call(
        functools.partial(_diff_kernel, lam_init=lam_init),
        grid=(B, DIFF_HEADS, S // qb),
        in_specs=[
            pl.BlockSpec((1, LANES, qb), lambda b, h, i: (b, TR_DFQ + h, i)),
            pl.BlockSpec((1, S, LANES), lambda b, h, i: (b, 0, ROW_DFK + h)),
            pl.BlockSpec((1, VA_ROWS, S), lambda b, h, i: (b, h, 0)),
            pl.BlockSpec((1, 3, kb, kb), lambda b, h, i: (h, 0, 0, 0)),
            pl.BlockSpec((1, 1, LANES), lambda b, h, i: (b, 0, 0)),
            _const_spec((4, HEAD_DIM)),
            _const_spec((1, 2 * HEAD_DIM)),
        ],
        out_specs=pl.BlockSpec((1, qb, LANES), lambda b, h, i: (b, i, h)),
        out_shape=jax.ShapeDtypeStruct((B, S, DIFF_WIDTH), BF16),
        compiler_params=_params("parallel", "parallel", "arbitrary"),
        name="diff_attention",
    )(tr, row, va, bias, kn, lam_p, subln)


def _conv_kernel(u_ref, uprev_ref, cw_ref, cb_ref, lg_ref, lb_ref, o_ref, h_ref, hs_ref, *, tc):
    i = pl.program_id(1)

    def glu(u):
        u = u.astype(F32)
        return u[:, :CONV_CH] * jax.nn.sigmoid(u[:, CONV_CH:])

    h_ref[0:CONV_HALO, :] = jnp.where(i == 0, 0.0, glu(uprev_ref[0]))
    h_ref[CONV_HALO:CONV_HALO + tc, :] = glu(u_ref[0])
    for b in range(1, SUBLANES):
        hs_ref[b - 1] = h_ref[b:b + tc + CONV_HALO - SUBLANES, :]
    first = CONV_HALO - (CONV_WIDTH - 1)
    for c in range(tc // CONV_ROWS):
        acc = jnp.zeros((CONV_ROWS, CONV_CH), F32)
        for w in range(CONV_WIDTH):
            a, b = divmod(first + w, SUBLANES)
            r0 = c * CONV_ROWS + a * SUBLANES
            src = h_ref[r0:r0 + CONV_ROWS, :] if b == 0 else hs_ref[b - 1, r0:r0 + CONV_ROWS, :]
            acc = acc + src * cw_ref[w:w + 1, :]
        y = acc + cb_ref[...]
        mu = jnp.mean(y, axis=-1, keepdims=True)
        var = jnp.mean(jnp.square(y - mu), axis=-1, keepdims=True)
        y = (y - mu) * lax.rsqrt(var + EPS) * lg_ref[...] + lb_ref[...]
        o_ref[0, c * CONV_ROWS:(c + 1) * CONV_ROWS, :] = (y * jax.nn.sigmoid(y)).astype(o_ref.dtype)


def _conv(row, cw, cb, lg, lb, tc):
    B, S, _ = row.shape
    ucol = 0
    per = tc // CONV_HALO
    return pl.pallas_call(
        functools.partial(_conv_kernel, tc=tc),
        grid=(B, S // tc),
        in_specs=[
            pl.BlockSpec((1, tc, 2 * CONV_CH), lambda b, i: (b, i, ucol)),
            pl.BlockSpec((1, CONV_HALO, 2 * CONV_CH), lambda b, i: (b, jnp.maximum(i * per - 1, 0), ucol)),
            _const_spec((CONV_WIDTH, CONV_CH)),
            _const_spec((1, CONV_CH)),
            _const_spec((1, CONV_CH)),
            _const_spec((1, CONV_CH)),
        ],
        out_specs=pl.BlockSpec((1, tc, CONV_CH), lambda b, i: (b, i, 0)),
        out_shape=jax.ShapeDtypeStruct((B, S, CONV_CH), BF16),
        scratch_shapes=[pltpu.VMEM((CONV_HALO + tc, CONV_CH), F32),
                        pltpu.VMEM((SUBLANES - 1, CONV_HALO + tc - SUBLANES, CONV_CH), F32)],
        compiler_params=_params("parallel", "parallel"),
        name="conformer_conv",
    )(row, row, cw, cb, lg, lb)


def _merge_kernel(x_ref, g_ref, wg_ref, osb_ref, odf_ref, ocv_ref, wsb_ref, wdf_ref, wcv_ref, wout_ref, o_ref):
    x = x_ref[...]
    D = x.shape[1]
    xb = _rmsnorm(x, g_ref[...]).astype(BF16)
    y = None
    for n, (o_br, w_br) in enumerate(((osb_ref, wsb_ref), (odf_ref, wdf_ref), (ocv_ref, wcv_ref))):
        gate = jax.nn.sigmoid(_dot(xb, wg_ref[:, n * D:(n + 1) * D]))
        t = gate * _dot(o_br[...], w_br[...])
        y = t if y is None else y + t
    o_ref[...] = x + _dot(y.astype(BF16), wout_ref[...])


def _merge(x2, g, wg, osb, odf, ocv, wsb, wdf, wcv, wout, tm):
    N, D = x2.shape
    W = osb.shape[1]
    tok = lambda w: pl.BlockSpec((tm, w), lambda i: (i, 0))
    return pl.pallas_call(
        _merge_kernel,
        grid=(N // tm,),
        in_specs=[tok(D), _const_spec((1, D)), _const_spec((D, 3 * D)), tok(W), tok(W), tok(W),
                  _const_spec((W, D)), _const_spec((W, D)), _const_spec((W, D)), _const_spec((D, D))],
        out_specs=tok(D),
        out_shape=jax.ShapeDtypeStruct((N, D), F32),
        compiler_params=_params("parallel"),
        name="gated_merge",
    )(x2, g, wg, osb, odf, ocv, wsb, wdf, wcv, wout)


def _memkv_kernel(mem_ref, g_ref, wkv_ref, o_ref):
    mb = _rmsnorm(mem_ref[0], g_ref[...]).astype(BF16)
    o_ref[0] = _dot(mb, wkv_ref[...]).astype(o_ref.dtype)


def _memkv(mem, g, wkv):
    B, M, D = mem.shape
    return pl.pallas_call(
        _memkv_kernel,
        grid=(B,),
        in_specs=[pl.BlockSpec((1, M, D), lambda b: (b, 0, 0)), _const_spec((1, D)), _const_spec((D, 2 * XA_WIDTH))],
        out_specs=pl.BlockSpec((1, M, 2 * XA_WIDTH), lambda b: (b, 0, 0)),
        out_shape=jax.ShapeDtypeStruct((B, M, 2 * XA_WIDTH), BF16),
        compiler_params=_params("parallel"),
        name="mem_kv",
    )(mem, g, wkv)


def _xattn_kernel(x_ref, g_ref, wq_ref, kv_ref, wo_ref, o_ref):
    x = x_ref[0]
    xb = _rmsnorm(x, g_ref[...]).astype(BF16)
    q = _dot(xb, wq_ref[...]).astype(BF16)
    heads = []
    for h in range(XA_HEADS):
        hs = slice(h * XA_HEAD_DIM, (h + 1) * XA_HEAD_DIM)
        k = kv_ref[0, :, hs]
        v = kv_ref[0, :, XA_WIDTH + h * XA_HEAD_DIM:XA_WIDTH + (h + 1) * XA_HEAD_DIM]
        s = _dot_nt(q[:, hs], k) * (XA_HEAD_DIM ** -0.5)
        p = jnp.exp(s - jnp.max(s, axis=1, keepdims=True))
        p = p / jnp.sum(p, axis=1, keepdims=True)
        heads.append(_dot(p.astype(BF16), v).astype(BF16))
    o = jnp.concatenate(heads, axis=1)
    o_ref[0] = x + _dot(o, wo_ref[...])


def _xattn(x, g, wq, kv, wo, tm):
    B, S, D = x.shape
    M = kv.shape[1]
    return pl.pallas_call(
        _xattn_kernel,
        grid=(B, S // tm),
        in_specs=[
            pl.BlockSpec((1, tm, D), lambda b, i: (b, i, 0)),
            _const_spec((1, D)),
            _const_spec((D, XA_WIDTH)),
            pl.BlockSpec((1, M, 2 * XA_WIDTH), lambda b, i: (b, 0, 0)),
            _const_spec((XA_WIDTH, D)),
        ],
        out_specs=pl.BlockSpec((1, tm, D), lambda b, i: (b, i, 0)),
        out_shape=jax.ShapeDtypeStruct((B, S, D), F32),
        compiler_params=_params("parallel", "parallel"),
        name="cross_attention",
    )(x, g, wq, kv, wo)


def _mlp_kernel(x_ref, g_ref, wup_ref, wdown_ref, gf_ref, o_ref, *, fc, final):
    x = x_ref[...]
    xb = _rmsnorm(x, g_ref[...]).astype(BF16)
    acc = x
    for c in range(wup_ref.shape[1] // fc):
        cs = slice(c * fc, (c + 1) * fc)
        h = jnp.square(jnp.maximum(_dot(xb, wup_ref[:, cs]), 0.0)).astype(BF16)
        acc = acc + _dot(h, wdown_ref[cs, :])
    o_ref[...] = _rmsnorm(acc, gf_ref[...]) if final else acc


def _mlp(x2, g, wup, wdown, gf, tm, final):
    N, D = x2.shape
    F = wup.shape[1]
    return pl.pallas_call(
        functools.partial(_mlp_kernel, fc=1024, final=final),
        grid=(N // tm,),
        in_specs=[pl.BlockSpec((tm, D), lambda i: (i, 0)), _const_spec((1, D)),
                  _const_spec((D, F)), _const_spec((F, D)), _const_spec((1, D))],
        out_specs=pl.BlockSpec((tm, D), lambda i: (i, 0)),
        out_shape=jax.ShapeDtypeStruct((N, D), F32),
        compiler_params=_params("parallel"),
        name="sqrelu_mlp",
    )(x2, g, wup, wdown, gf)


def kernel(x, mem, rel_bias, ln_mix, w_in, diff_lambda, diff_subln, conv_w, conv_b, conv_ln_g, conv_ln_b,
           w_sb_proj, w_diff_proj, w_conv_proj, w_out, ln_xattn, ln_mem, xa_w_q, xa_w_kv, xa_w_o,
           ln_mlp, w_up, w_down, ln_final):
    B, S, D = x.shape
    depth = w_in.shape[0]
    blk = ATT_BLOCK
    tm = min(512, S)
    tc = min(256, S)
    assert S % SB_QBLOCK == 0 and S % DF_QBLOCK == 0 and S % tm == 0 and S % tc == 0 and D == 1024

    row2 = lambda a: a.reshape(1, -1).astype(F32)
    tri = jnp.asarray(np.triu(np.ones((blk, blk), np.float32)), BF16)
    bias = _bias_tiles(rel_bias, blk)
    scale = HEAD_DIM ** -0.5

    for l in range(depth):
        w = w_in[l]
        sbq, sbk, sbv = w[:, 0:512] * scale, w[:, 512:1024], w[:, 1024:1536]
        dfq, dfk, dfv = w[:, 1536:2048] * scale, w[:, 2048:2560], w[:, 2560:3072]
        wrow = jnp.concatenate([w[:, 3072:4096], sbk, dfk], axis=1).astype(BF16)
        wtr = jnp.concatenate([sbq, sbv, dfq, dfv], axis=1).T.astype(BF16)
        wg = w[:, 4096:].astype(BF16)

        row, tr, va = _inproj(x, row2(ln_mix[l]), wrow, wtr, tm)
        kn = _key_norm_max(row)
        o_sb = _sb_attention(row, tr, tri, kn)
        lam_init = 0.8 - 0.6 * math.exp(-0.3 * l)
        o_df = _diff_attention(row, tr, va, bias, kn, diff_lambda[l].astype(F32), row2(diff_subln[l]), lam_init)
        o_cv = _conv(row, conv_w[l].reshape(CONV_WIDTH, CONV_CH).astype(F32), row2(conv_b[l]),
                     row2(conv_ln_g[l]), row2(conv_ln_b[l]), tc)

        N = B * S
        x2 = _merge(x.reshape(N, D), row2(ln_mix[l]), wg,
                    o_sb.reshape(N, -1), o_df.reshape(N, -1), o_cv.reshape(N, -1),
                    w_sb_proj[l].astype(BF16), w_diff_proj[l].astype(BF16), w_conv_proj[l].astype(BF16),
                    w_out[l].astype(BF16), tm)

        kv = _memkv(mem, row2(ln_mem[l]), xa_w_kv[l].astype(BF16))
        x3 = _xattn(x2.reshape(B, S, D), row2(ln_xattn[l]),
                    xa_w_q[l].astype(BF16), kv, xa_w_o[l].astype(BF16), tm)

        x = _mlp(x3.reshape(N, D), row2(ln_mlp[l]), w_up[l].astype(BF16), w_down[l].astype(BF16),
                 row2(ln_final), tm, final=(l == depth - 1)).reshape(B, S, D)
    return x
```

```python
import functools
import math

import numpy as np
import jax
import jax.numpy as jnp
from jax import lax
from jax.experimental import pallas as pl
from jax.experimental.pallas import tpu as pltpu

F32 = jnp.float32
BF16 = jnp.bfloat16

HEAD_DIM = 64
SB_WIDTH = 512
DIFF_WIDTH = 512
DIFF_HEADS = 4
CONV_CH = 512
CONV_WIDTH = 31
XA_HEADS = 4
XA_HEAD_DIM = 128
XA_WIDTH = XA_HEADS * XA_HEAD_DIM
NUM_BUCKETS = 32
MAX_EXACT = NUM_BUCKETS // 2
MAX_DISTANCE = 128
EPS = 1e-6
NEG_INF = -1e30

LANES = 128
SUBLANES = 8
ATT_BLOCK = 256
CONV_HALO = 32
CONV_ROWS = 64
VMEM_LIMIT = 56 * 1024 * 1024

SB_QBLOCK = 1024
DF_QBLOCK = 1024
SB_EXIT = 110.0
DF_UNROLL = 4
DF_SPREAD = 80.0
LOG2E = 1.4426950408889634
ONES_ROWS = 16

ROW_COLS = 2 * CONV_CH + 2 * 512
ROW_SBK, ROW_DFK = 8, 12
TR_ROWS = 3 * 512
TR_SBQ, TR_SBV, TR_DFQ = 0, 4, 8
VA_ROWS = LANES + ONES_ROWS


def _params(*sem):
    return pltpu.CompilerParams(dimension_semantics=sem, vmem_limit_bytes=VMEM_LIMIT)


def _const_spec(shape):
    nd = len(shape)
    return pl.BlockSpec(shape, lambda *_: (0,) * nd)


def _rmsnorm(x, g):
    return x * lax.rsqrt(jnp.mean(x * x, axis=-1, keepdims=True) + EPS) * g


def _dot(a, b):
    return jnp.dot(a, b, preferred_element_type=F32)


def _dot_nt(a, b):
    return lax.dot_general(a, b, (((1,), (1,)), ((), ())), preferred_element_type=F32)


def _inproj_kernel(x_ref, g_ref, wrow_ref, wtr_ref, row_ref, tr_ref, va_ref):
    xb = _rmsnorm(x_ref[0], g_ref[...]).astype(BF16)
    tm = xb.shape[0]
    for c in range(ROW_COLS // 512):
        cs = slice(c * 512, (c + 1) * 512)
        row_ref[0, :, cs] = _dot(xb, wrow_ref[:, cs]).astype(BF16)
    for c in range(TR_ROWS // 512):
        cs = slice(c * 512, (c + 1) * 512)
        tr_ref[0, cs, :] = _dot_nt(wtr_ref[cs, :], xb).astype(BF16)
    dfv = _dot_nt(wtr_ref[TR_ROWS:, :], xb).astype(BF16)
    for h in range(DIFF_HEADS):
        va_ref[0, h * VA_ROWS:h * VA_ROWS + LANES, :] = dfv[h * LANES:(h + 1) * LANES, :]
        va_ref[0, h * VA_ROWS + LANES:(h + 1) * VA_ROWS, :] = jnp.ones((ONES_ROWS, tm), BF16)


def _inproj(x, g, wrow, wtr, tm):
    B, S, D = x.shape
    return pl.pallas_call(
        _inproj_kernel,
        grid=(B, S // tm),
        in_specs=[
            pl.BlockSpec((1, tm, D), lambda b, i: (b, i, 0)),
            _const_spec((1, D)),
            _const_spec((D, ROW_COLS)),
            _const_spec((TR_ROWS + DIFF_WIDTH, D)),
        ],
        out_specs=[
            pl.BlockSpec((1, tm, ROW_COLS), lambda b, i: (b, i, 0)),
            pl.BlockSpec((1, TR_ROWS, tm), lambda b, i: (b, 0, i)),
            pl.BlockSpec((1, DIFF_HEADS * VA_ROWS, tm), lambda b, i: (b, 0, i)),
        ],
        out_shape=[
            jax.ShapeDtypeStruct((B, S, ROW_COLS), BF16),
            jax.ShapeDtypeStruct((B, TR_ROWS, S), BF16),
            jax.ShapeDtypeStruct((B, DIFF_HEADS * VA_ROWS, S), BF16),
        ],
        compiler_params=_params("parallel", "parallel"),
        name="inproj",
    )(x, g, wrow, wtr)


def _knorm_kernel(k_ref, sel_ref, o_ref):
    k = k_ref[0].astype(F32)
    n2 = _dot((k * k).astype(BF16), sel_ref[...])
    o_ref[0] = jnp.sqrt(jnp.max(n2, axis=0, keepdims=True))


def _key_norm_max(row):
    B, S, _ = row.shape
    width = SB_WIDTH + DIFF_WIDTH
    sel = np.zeros((width, LANES), np.float32)
    sel[np.arange(width), np.arange(width) // HEAD_DIM] = 1.0
    return pl.pallas_call(
        _knorm_kernel,
        grid=(B,),
        in_specs=[pl.BlockSpec((1, S, width), lambda b: (b, 0, ROW_SBK * LANES // width)),
                  _const_spec((width, LANES))],
        out_specs=pl.BlockSpec((1, 1, LANES), lambda b: (b, 0, 0)),
        out_shape=jax.ShapeDtypeStruct((B, 1, LANES), F32),
        compiler_params=_params("parallel"),
        name="key_norm_max",
    )(row, jnp.asarray(sel, BF16))


def _sb_kernel(qt_ref, k_ref, vt_ref, tri_ref, kn_ref, o_ref):
    kb, qb = ATT_BLOCK, SB_QBLOCK
    nq = qb // kb
    p, i = pl.program_id(1), pl.program_id(2)
    qt = qt_ref[0]
    sub = lax.broadcasted_iota(jnp.int32, qt.shape, 0)
    zero = jnp.zeros_like(qt)
    qtm = (jnp.where(sub < HEAD_DIM, qt, zero), jnp.where(sub >= HEAD_DIM, qt, zero))
    tri = tri_ref[...]

    lane = lax.broadcasted_iota(jnp.int32, (1, LANES), 1)
    kn = kn_ref[0]
    zbound = []
    for h in range(2):
        knh = jnp.max(jnp.where(lane == 2 * p + h, kn, 0.0), axis=1, keepdims=True)
        qf = qtm[h].astype(F32)
        zbound.append(jnp.sqrt(jnp.sum(qf * qf, axis=0, keepdims=True)) * (knh * 1.01))

    def splice(old, new, lo, hi):
        parts = ([old[:, :lo]] if lo else []) + [new] + ([old[:, hi:]] if hi < qb else [])
        return parts[0] if len(parts) == 1 else jnp.concatenate(parts, axis=1)

    def tiles(jobs, st):
        carry, acc = [st[0], st[2]], [st[1], st[3]]
        ks = [pl.ds(pl.multiple_of(j * kb, kb), kb) for j, _, _, _ in jobs]
        units = [(n, h) for n in range(len(jobs)) for h in range(2)]
        z = {(n, h): _dot(k_ref[0, ks[n], :], qtm[h][:, jobs[n][1]:jobs[n][2]]) for n, h in units}
        sp = {u: jnp.maximum(z[u], 0.0) + jnp.log(1.0 + jnp.exp2(jnp.abs(z[u]) * -LOG2E)) for u in units}
        keep = []
        for n, (_, lo, hi, valid) in enumerate(jobs):
            m = (lax.broadcasted_iota(jnp.int32, (kb, hi - lo), 0)
                 < lax.broadcasted_iota(jnp.int32, (kb, hi - lo), 1)) if valid is None else valid
            keep.append(m)
            for h in range(2):
                sp[n, h] = jnp.where(m, sp[n, h], 0.0)
        cs = {u: _dot(tri, sp[u].astype(BF16)) for u in units}
        for n, (_, lo, hi, _) in enumerate(jobs):
            vt = vt_ref[0, :, ks[n]]
            c = [cs[n, h] + carry[h][:, lo:hi] for h in range(2)]
            a = [jnp.where(keep[n], jnp.exp(z[n, h] - c[h]), 0.0) for h in range(2)]
            pv = [_dot(vt[h * HEAD_DIM:(h + 1) * HEAD_DIM, :], a[h].astype(BF16)) for h in range(2)]
            for h in range(2):
                carry[h] = splice(carry[h], c[h][0:1, :], lo, hi)
                acc[h] = splice(acc[h], acc[h][:, lo:hi] + pv[h], lo, hi)
        return carry[0], acc[0], carry[1], acc[1]

    zc = jnp.zeros((1, qb), F32)
    za = jnp.zeros((HEAD_DIM, qb), F32)
    newest = [(nq * i + c, c * kb, min(c + 2, nq) * kb, None) for c in reversed(range(nq))]

    def unfinished(st):
        slack = jnp.minimum(st[0] - zbound[0], st[2] - zbound[1])
        return (jnp.min(slack) <= SB_EXIT).astype(jnp.int32)

    def body(s):
        d = s[0]
        new = tiles([(jnp.maximum(nq * i + c - d, 0), c * kb, (c + 1) * kb, nq * i + c >= d)
                     for c in reversed(range(nq))], s[2:])
        return (d + 1, unfinished(new)) + new

    st = lax.cond(i > 0, lambda: tiles(newest + [(nq * i - 1, 0, kb, True)], (zc, za, zc, za)),
                  lambda: tiles(newest, (zc, za, zc, za)))
    st = lax.while_loop(lambda s: jnp.logical_and(s[0] < nq * (i + 1), s[1] > 0), body,
                        (jnp.int32(2), unfinished(st)) + st)[2:]
    o_ref[0] = jnp.concatenate([st[1], st[3]], axis=0).T.astype(o_ref.dtype)


def _sb_attention(row, tr, tri, kn):
    B, S, _ = row.shape
    qb = SB_QBLOCK
    return pl.pallas_call(
        _sb_kernel,
        grid=(B, SB_WIDTH // LANES, S // qb),
        in_specs=[
            pl.BlockSpec((1, LANES, qb), lambda b, p, i: (b, TR_SBQ + p, i)),
            pl.BlockSpec((1, S, LANES), lambda b, p, i: (b, 0, ROW_SBK + p)),
            pl.BlockSpec((1, LANES, S), lambda b, p, i: (b, TR_SBV + p, 0)),
            _const_spec((ATT_BLOCK, ATT_BLOCK)),
            pl.BlockSpec((1, 1, LANES), lambda b, p, i: (b, 0, 0)),
        ],
        out_specs=pl.BlockSpec((1, qb, LANES), lambda b, p, i: (b, i, p)),
        out_shape=jax.ShapeDtypeStruct((B, S, SB_WIDTH), BF16),
        compiler_params=_params("parallel", "parallel", "arbitrary"),
        name="sb_attention",
    )(tr, row, tr, tri, kn)


def _bucket_tiles(blk):
    r = np.arange(blk)[None, :] - np.arange(blk)[:, None]
    rel = np.stack([r, r + blk, r + 2 * blk]).astype(np.int64)
    n = np.maximum(rel, 0)
    nf = np.maximum(n, 1).astype(np.float32)
    large = MAX_EXACT + (np.log(nf / np.float32(MAX_EXACT)) / np.float32(math.log(MAX_DISTANCE / MAX_EXACT))
                         * np.float32(NUM_BUCKETS - MAX_EXACT)).astype(np.int32)
    large = np.minimum(large, NUM_BUCKETS - 1)
    idx = np.where(n < MAX_EXACT, n, large).astype(np.int32)
    assert (idx[2] == NUM_BUCKETS - 1).all() and blk >= MAX_DISTANCE
    return idx


def _bias_kernel(table_ref, idx_ref, o_ref):
    h = pl.program_id(0)
    idx = idx_ref[0]
    acc = jnp.zeros(idx.shape, F32)
    for b in range(NUM_BUCKETS):
        acc = jnp.where(idx == b, table_ref[b, h], acc)
    o_ref[0, 0] = acc


def _bias_tiles(rel_bias, blk):
    idx = jnp.asarray(_bucket_tiles(blk))
    return pl.pallas_call(
        _bias_kernel,
        grid=(DIFF_HEADS, 3),
        in_specs=[
            pl.BlockSpec(memory_space=pltpu.SMEM),
            pl.BlockSpec((1, blk, blk), lambda h, d: (d, 0, 0)),
        ],
        out_specs=pl.BlockSpec((1, 1, blk, blk), lambda h, d: (h, d, 0, 0)),
        out_shape=jax.ShapeDtypeStruct((DIFF_HEADS, 3, blk, blk), F32),
        name="bias_tiles",
    )(rel_bias.astype(F32), idx)


def _diff_kernel(qt_ref, k_ref, va_ref, bias_ref, kn_ref, lam_ref, subln_ref, o_ref, *, lam_init):
    kb, qb = ATT_BLOCK, DF_QBLOCK
    nq = qb // kb
    h, i = pl.program_id(1), pl.program_id(2)
    qt = qt_ref[0]
    sub = lax.broadcasted_iota(jnp.int32, qt.shape, 0)
    zero = jnp.zeros_like(qt)
    qtm = (jnp.where(sub < HEAD_DIM, qt, zero), jnp.where(sub >= HEAD_DIM, qt, zero))
    far = bias_ref[0, 2][0:1, 0:1]

    def scores(j, lo):
        k = k_ref[0, pl.ds(pl.multiple_of(j * kb, kb), kb), :]
        return [_dot(k, qtm[m][:, lo:]) for m in range(2)]

    def update(z, j, st, lo, bias, mask, online):
        vta = va_ref[0, :, pl.ds(pl.multiple_of(j * kb, kb), kb)]
        if bias is not None:
            z = [x + bias for x in z]
        if mask is not None:
            z = [jnp.where(mask, x, NEG_INF) for x in z]
        mnew = [st[m][0][:, lo:] for m in range(2)]
        scale = None
        if online:
            top = [jnp.max(x, axis=0, keepdims=True) for x in z]
            if bias is None:
                top = [t + far for t in top]
            mold, mnew = mnew, [jnp.maximum(mnew[m], top[m]) for m in range(2)]
            scale = [jnp.exp(mold[m] - mnew[m]) for m in range(2)]
        shift = [mn - far for mn in mnew] if bias is None else mnew
        p = [jnp.exp(z[m] - shift[m]) for m in range(2)]
        pv = [_dot(vta, p[m].astype(BF16)) for m in range(2)]
        out = []
        for m in range(2):
            l, acc = st[m][1][:, lo:], st[m][2][:, lo:]
            if online:
                l, acc = scale[m] * l, scale[m] * acc
            new = (mnew[m], l + pv[m][LANES:LANES + 1, :], acc + pv[m][:LANES, :])
            if lo:
                new = tuple(jnp.concatenate([old[:, :lo], x], axis=1) for old, x in zip(st[m], new))
            out.append(new)
        return tuple(out)

    def run(stab, online):
        zl, za = jnp.zeros((1, qb), F32), jnp.zeros((LANES, qb), F32)
        st = ((stab[0], zl, za), (stab[1], zl, za))
        for r in reversed(range(nq)):
            n = qb - r * kb
            bias = jnp.concatenate([bias_ref[0, min(c - r, 2)] for c in range(r, nq)], axis=1)
            causal = lax.broadcasted_iota(jnp.int32, (kb, n), 0) <= lax.broadcasted_iota(jnp.int32, (kb, n), 1)
            st = update(scores(nq * i + r, r * kb), nq * i + r, st, r * kb, bias, causal, online)
        def older(st):
            bias = jnp.concatenate([bias_ref[0, min(c + 1, 2)] for c in range(nq)], axis=1)
            st = update(scores(nq * i - 1, 0), nq * i - 1, st, 0, bias, None, online)
            for d in range(2, DF_UNROLL + 1):
                st = update(scores(nq * i - d, 0), nq * i - d, st, 0, None, None, online)

            def body(jj, s):
                for d in range(DF_UNROLL):
                    j = nq * i - DF_UNROLL * (jj + 1) - 1 - d
                    s = update(scores(j, 0), j, s, 0, None, None, online)
                return s

            return lax.fori_loop(0, (nq * i) // DF_UNROLL - 1, body, st)

        return lax.cond(i > 0, older, lambda s: s, st)

    lane = lax.broadcasted_iota(jnp.int32, (1, LANES), 1)
    kn = kn_ref[0]
    allb = bias_ref[0]
    bmax = jnp.max(jnp.max(allb, axis=0), axis=(0, 1), keepdims=True)
    bmin = jnp.min(jnp.min(allb, axis=0), axis=(0, 1), keepdims=True)
    qk = []
    for m in range(2):
        knm = jnp.max(jnp.where(lane == SB_WIDTH // HEAD_DIM + 2 * h + m, kn, 0.0), axis=1, keepdims=True)
        qf = qtm[m].astype(F32)
        qk.append(jnp.sqrt(jnp.sum(qf * qf, axis=0, keepdims=True)) * (knm * 1.01))
    spread = 2.0 * jnp.maximum(qk[0], qk[1]) + (bmax - bmin)
    bound = [x + bmax for x in qk]
    neg = jnp.full((1, qb), NEG_INF, F32)
    st = lax.cond(jnp.max(spread) <= DF_SPREAD,
                  lambda: run(bound, False), lambda: run([neg, neg], True))

    lp = lam_ref[...]
    lam = (jnp.exp(jnp.sum(lp[0:1] * lp[1:2], axis=1, keepdims=True))
           - jnp.exp(jnp.sum(lp[2:3] * lp[3:4], axis=1, keepdims=True)) + lam_init)
    (_, l0, acc0), (_, l1, acc1) = st
    o = (acc0 / l0 - lam * (acc1 / l1)).T
    o_ref[0] = (_rmsnorm(o, subln_ref[...]) * (1.0 - lam_init)).astype(o_ref.dtype)


def _diff_attention(row, tr, va, bias, kn, lam_p, subln, lam_init):
    B, S, _ = row.shape
    kb, qb = ATT_BLOCK, DF_QBLOCK
    return pl.pallas_call(
        functools.partial(_diff_kernel, lam_init=lam_init),
        grid=(B, DIFF_HEADS, S // qb),
        in_specs=[
            pl.BlockSpec((1, LANES, qb), lambda b, h, i: (b, TR_DFQ + h, i)),
            pl.BlockSpec((1, S, LANES), lambda b, h, i: (b, 0, ROW_DFK + h)),
            pl.BlockSpec((1, VA_ROWS, S), lambda b, h, i: (b, h, 0)),
            pl.BlockSpec((1, 3, kb, kb), lambda b, h, i: (h, 0, 0, 0)),
            pl.BlockSpec((1, 1, LANES), lambda b, h, i: (b, 0, 0)),
            _const_spec((4, HEAD_DIM)),
            _const_spec((1, 2 * HEAD_DIM)),
        ],
        out_specs=pl.BlockSpec((1, qb, LANES), lambda b, h, i: (b, i, h)),
        out_shape=jax.ShapeDtypeStruct((B, S, DIFF_WIDTH), BF16),
        compiler_params=_params("parallel", "parallel", "arbitrary"),
        name="diff_attention",
    )(tr, row, va, bias, kn, lam_p, subln)


def _conv_kernel(u_ref, uprev_ref, cw_ref, cb_ref, lg_ref, lb_ref, o_ref, h_ref, hs_ref, *, tc):
    i = pl.program_id(1)

    def glu(u):
        u = u.astype(F32)
        return u[:, :CONV_CH] * jax.nn.sigmoid(u[:, CONV_CH:])

    h_ref[0:CONV_HALO, :] = jnp.where(i == 0, 0.0, glu(uprev_ref[0]))
    h_ref[CONV_HALO:CONV_HALO + tc, :] = glu(u_ref[0])
    for b in range(1, SUBLANES):
        hs_ref[b - 1] = h_ref[b:b + tc + CONV_HALO - SUBLANES, :]
    first = CONV_HALO - (CONV_WIDTH - 1)
    for c in range(tc // CONV_ROWS):
        acc = jnp.zeros((CONV_ROWS, CONV_CH), F32)
        for w in range(CONV_WIDTH):
            a, b = divmod(first + w, SUBLANES)
            r0 = c * CONV_ROWS + a * SUBLANES
            src = h_ref[r0:r0 + CONV_ROWS, :] if b == 0 else hs_ref[b - 1, r0:r0 + CONV_ROWS, :]
            acc = acc + src * cw_ref[w:w + 1, :]
        y = acc + cb_ref[...]
        mu = jnp.mean(y, axis=-1, keepdims=True)
        var = jnp.mean(jnp.square(y - mu), axis=-1, keepdims=True)
        y = (y - mu) * lax.rsqrt(var + EPS) * lg_ref[...] + lb_ref[...]
        o_ref[0, c * CONV_ROWS:(c + 1) * CONV_ROWS, :] = (y * jax.nn.sigmoid(y)).astype(o_ref.dtype)


def _conv(row, cw, cb, lg, lb, tc):
    B, S, _ = row.shape
    ucol = 0
    per = tc // CONV_HALO
    return pl.pallas_call(
        functools.partial(_conv_kernel, tc=tc),
        grid=(B, S // tc),
        in_specs=[
            pl.BlockSpec((1, tc, 2 * CONV_CH), lambda b, i: (b, i, ucol)),
            pl.BlockSpec((1, CONV_HALO, 2 * CONV_CH), lambda b, i: (b, jnp.maximum(i * per - 1, 0), ucol)),
            _const_spec((CONV_WIDTH, CONV_CH)),
            _const_spec((1, CONV_CH)),
            _const_spec((1, CONV_CH)),
            _const_spec((1, CONV_CH)),
        ],
        out_specs=pl.BlockSpec((1, tc, CONV_CH), lambda b, i: (b, i, 0)),
        out_shape=jax.ShapeDtypeStruct((B, S, CONV_CH), BF16),
        scratch_shapes=[pltpu.VMEM((CONV_HALO + tc, CONV_CH), F32),
                        pltpu.VMEM((SUBLANES - 1, CONV_HALO + tc - SUBLANES, CONV_CH), F32)],
        compiler_params=_params("parallel", "parallel"),
        name="conformer_conv",
    )(row, row, cw, cb, lg, lb)


def _merge_kernel(x_ref, g_ref, wg_ref, osb_ref, odf_ref, ocv_ref, wsb_ref, wdf_ref, wcv_ref, wout_ref, o_ref):
    x = x_ref[...]
    D = x.shape[1]
    xb = _rmsnorm(x, g_ref[...]).astype(BF16)
    y = None
    for n, (o_br, w_br) in enumerate(((osb_ref, wsb_ref), (odf_ref, wdf_ref), (ocv_ref, wcv_ref))):
        gate = jax.nn.sigmoid(_dot(xb, wg_ref[:, n * D:(n + 1) * D]))
        t = gate * _dot(o_br[...], w_br[...])
        y = t if y is None else y + t
    o_ref[...] = x + _dot(y.astype(BF16), wout_ref[...])


def _merge(x2, g, wg, osb, odf, ocv, wsb, wdf, wcv, wout, tm):
    N, D = x2.shape
    W = osb.shape[1]
    tok = lambda w: pl.BlockSpec((tm, w), lambda i: (i, 0))
    return pl.pallas_call(
        _merge_kernel,
        grid=(N // tm,),
        in_specs=[tok(D), _const_spec((1, D)), _const_spec((D, 3 * D)), tok(W), tok(W), tok(W),
                  _const_spec((W, D)), _const_spec((W, D)), _const_spec((W, D)), _const_spec((D, D))],
        out_specs=tok(D),
        out_shape=jax.ShapeDtypeStruct((N, D), F32),
        compiler_params=_params("parallel"),
        name="gated_merge",
    )(x2, g, wg, osb, odf, ocv, wsb, wdf, wcv, wout)


def _memkv_kernel(mem_ref, g_ref, wkv_ref, o_ref):
    mb = _rmsnorm(mem_ref[0], g_ref[...]).astype(BF16)
    o_ref[0] = _dot(mb, wkv_ref[...]).astype(o_ref.dtype)


def _memkv(mem, g, wkv):
    B, M, D = mem.shape
    return pl.pallas_call(
        _memkv_kernel,
        grid=(B,),
        in_specs=[pl.BlockSpec((1, M, D), lambda b: (b, 0, 0)), _const_spec((1, D)), _const_spec((D, 2 * XA_WIDTH))],
        out_specs=pl.BlockSpec((1, M, 2 * XA_WIDTH), lambda b: (b, 0, 0)),
        out_shape=jax.ShapeDtypeStruct((B, M, 2 * XA_WIDTH), BF16),
        compiler_params=_params("parallel"),
        name="mem_kv",
    )(mem, g, wkv)


def _xattn_kernel(x_ref, g_ref, wq_ref, kv_ref, wo_ref, o_ref):
    x = x_ref[0]
    xb = _rmsnorm(x, g_ref[...]).astype(BF16)
    q = _dot(xb, wq_ref[...]).astype(BF16)
    heads = []
    for h in range(XA_HEADS):
        hs = slice(h * XA_HEAD_DIM, (h + 1) * XA_HEAD_DIM)
        k = kv_ref[0, :, hs]
        v = kv_ref[0, :, XA_WIDTH + h * XA_HEAD_DIM:XA_WIDTH + (h + 1) * XA_HEAD_DIM]
        s = _dot_nt(q[:, hs], k) * (XA_HEAD_DIM ** -0.5)
        p = jnp.exp(s - jnp.max(s, axis=1, keepdims=True))
        p = p / jnp.sum(p, axis=1, keepdims=True)
        heads.append(_dot(p.astype(BF16), v).astype(BF16))
    o = jnp.concatenate(heads, axis=1)
    o_ref[0] = x + _dot(o, wo_ref[...])


def _xattn(x, g, wq, kv, wo, tm):
    B, S, D = x.shape
    M = kv.shape[1]
    return pl.pallas_call(
        _xattn_kernel,
        grid=(B, S // tm),
        in_specs=[
            pl.BlockSpec((1, tm, D), lambda b, i: (b, i, 0)),
            _const_spec((1, D)),
            _const_spec((D, XA_WIDTH)),
            pl.BlockSpec((1, M, 2 * XA_WIDTH), lambda b, i: (b, 0, 0)),
            _const_spec((XA_WIDTH, D)),
        ],
        out_specs=pl.BlockSpec((1, tm, D), lambda b, i: (b, i, 0)),
        out_shape=jax.ShapeDtypeStruct((B, S, D), F32),
        compiler_params=_params("parallel", "parallel"),
        name="cross_attention",
    )(x, g, wq, kv, wo)


def _mlp_kernel(x_ref, g_ref, wup_ref, wdown_ref, gf_ref, o_ref, *, fc, final):
    x = x_ref[...]
    xb = _rmsnorm(x, g_ref[...]).astype(BF16)
    acc = x
    for c in range(wup_ref.shape[1] // fc):
        cs = slice(c * fc, (c + 1) * fc)
        h = jnp.square(jnp.maximum(_dot(xb, wup_ref[:, cs]), 0.0)).astype(BF16)
        acc = acc + _dot(h, wdown_ref[cs, :])
    o_ref[...] = _rmsnorm(acc, gf_ref[...]) if final else acc


def _mlp(x2, g, wup, wdown, gf, tm, final):
    N, D = x2.shape
    F = wup.shape[1]
    return pl.pallas_call(
        functools.partial(_mlp_kernel, fc=1024, final=final),
        grid=(N // tm,),
        in_specs=[pl.BlockSpec((tm, D), lambda i: (i, 0)), _const_spec((1, D)),
                  _const_spec((D, F)), _const_spec((F, D)), _const_spec((1, D))],
        out_specs=pl.BlockSpec((tm, D), lambda i: (i, 0)),
        out_shape=jax.ShapeDtypeStruct((N, D), F32),
        compiler_params=_params("parallel"),
        name="sqrelu_mlp",
    )(x2, g, wup, wdown, gf)


def kernel(x, mem, rel_bias, ln_mix, w_in, diff_lambda, diff_subln, conv_w, conv_b, conv_ln_g, conv_ln_b,
           w_sb_proj, w_diff_proj, w_conv_proj, w_out, ln_xattn, ln_mem, xa_w_q, xa_w_kv, xa_w_o,
           ln_mlp, w_up, w_down, ln_final):
    B, S, D = x.shape
    depth = w_in.shape[0]
    blk = ATT_BLOCK
    tm = min(512, S)
    tc = min(256, S)
    assert S % SB_QBLOCK == 0 and S % DF_QBLOCK == 0 and S % tm == 0 and S % tc == 0 and D == 1024

    row2 = lambda a: a.reshape(1, -1).astype(F32)
    tri = jnp.asarray(np.triu(np.ones((blk, blk), np.float32)), BF16)
    bias = _bias_tiles(rel_bias, blk)
    scale = HEAD_DIM ** -0.5

    for l in range(depth):
        w = w_in[l]
        sbq, sbk, sbv = w[:, 0:512] * scale, w[:, 512:1024], w[:, 1024:1536]
        dfq, dfk, dfv = w[:, 1536:2048] * scale, w[:, 2048:2560], w[:, 2560:3072]
        wrow = jnp.concatenate([w[:, 3072:4096], sbk, dfk], axis=1).astype(BF16)
        wtr = jnp.concatenate([sbq, sbv, dfq, dfv], axis=1).T.astype(BF16)
        wg = w[:, 4096:].astype(BF16)

        row, tr, va = _inproj(x, row2(ln_mix[l]), wrow, wtr, tm)
        kn = _key_norm_max(row)
        o_sb = _sb_attention(row, tr, tri, kn)
        lam_init = 0.8 - 0.6 * math.exp(-0.3 * l)
        o_df = _diff_attention(row, tr, va, bias, kn, diff_lambda[l].astype(F32), row2(diff_subln[l]), lam_init)
        o_cv = _conv(row, conv_w[l].reshape(CONV_WIDTH, CONV_CH).astype(F32), row2(conv_b[l]),
                     row2(conv_ln_g[l]), row2(conv_ln_b[l]), tc)

        N = B * S
        x2 = _merge(x.reshape(N, D), row2(ln_mix[l]), wg,
                    o_sb.reshape(N, -1), o_df.reshape(N, -1), o_cv.reshape(N, -1),
                    w_sb_proj[l].astype(BF16), w_diff_proj[l].astype(BF16), w_conv_proj[l].astype(BF16),
                    w_out[l].astype(BF16), tm)

        kv = _memkv(mem, row2(ln_mem[l]), xa_w_kv[l].astype(BF16))
        x3 = _xattn(x2.reshape(B, S, D), row2(ln_xattn[l]),
                    xa_w_q[l].astype(BF16), kv, xa_w_o[l].astype(BF16), tm)

        x = _mlp(x3.reshape(N, D), row2(ln_mlp[l]), w_up[l].astype(BF16), w_down[l].astype(BF16),
                 row2(ln_final), tm, final=(l == depth - 1)).reshape(B, S, D)
    return x
```

```python
import functools
import math

import numpy as np
import jax
import jax.numpy as jnp
from jax import lax
from jax.experimental import pallas as pl
from jax.experimental.pallas import tpu as pltpu

F32 = jnp.float32
BF16 = jnp.bfloat16

HEAD_DIM = 64
SB_WIDTH = 512
DIFF_WIDTH = 512
DIFF_HEADS = 4
CONV_CH = 512
CONV_WIDTH = 31
XA_HEADS = 4
XA_HEAD_DIM = 128
XA_WIDTH = XA_HEADS * XA_HEAD_DIM
NUM_BUCKETS = 32
MAX_EXACT = NUM_BUCKETS // 2
MAX_DISTANCE = 128
EPS = 1e-6
NEG_INF = -1e30

LANES = 128
SUBLANES = 8
ATT_BLOCK = 256
CONV_HALO = 32
CONV_ROWS = 64
VMEM_LIMIT = 56 * 1024 * 1024

SB_QBLOCK = 1024
DF_QBLOCK = 1024
SB_EXIT = 110.0
DF_UNROLL = 4
DF_SPREAD = 80.0
LOG2E = 1.4426950408889634
ONES_ROWS = 16

ROW_COLS = 2 * CONV_CH + 2 * 512
ROW_SBK, ROW_DFK = 8, 12
TR_ROWS = 3 * 512
TR_SBQ, TR_SBV, TR_DFQ = 0, 4, 8
VA_ROWS = LANES + ONES_ROWS


def _params(*sem):
    return pltpu.CompilerParams(dimension_semantics=sem, vmem_limit_bytes=VMEM_LIMIT)


def _const_spec(shape):
    nd = len(shape)
    return pl.BlockSpec(shape, lambda *_: (0,) * nd)


def _rmsnorm(x, g):
    return x * lax.rsqrt(jnp.mean(x * x, axis=-1, keepdims=True) + EPS) * g


def _dot(a, b):
    return jnp.dot(a, b, preferred_element_type=F32)


def _dot_nt(a, b):
    return lax.dot_general(a, b, (((1,), (1,)), ((), ())), preferred_element_type=F32)


def _inproj_kernel(x_ref, g_ref, wrow_ref, wtr_ref, row_ref, tr_ref, va_ref):
    xb = _rmsnorm(x_ref[0], g_ref[...]).astype(BF16)
    tm = xb.shape[0]
    for c in range(ROW_COLS // 512):
        cs = slice(c * 512, (c + 1) * 512)
        row_ref[0, :, cs] = _dot(xb, wrow_ref[:, cs]).astype(BF16)
    for c in range(TR_ROWS // 512):
        cs = slice(c * 512, (c + 1) * 512)
        tr_ref[0, cs, :] = _dot_nt(wtr_ref[cs, :], xb).astype(BF16)
    dfv = _dot_nt(wtr_ref[TR_ROWS:, :], xb).astype(BF16)
    for h in range(DIFF_HEADS):
        va_ref[0, h * VA_ROWS:h * VA_ROWS + LANES, :] = dfv[h * LANES:(h + 1) * LANES, :]
        va_ref[0, h * VA_ROWS + LANES:(h + 1) * VA_ROWS, :] = jnp.ones((ONES_ROWS, tm), BF16)


def _inproj(x, g, wrow, wtr, tm):
    B, S, D = x.shape
    return pl.pallas_call(
        _inproj_kernel,
        grid=(B, S // tm),
        in_specs=[
            pl.BlockSpec((1, tm, D), lambda b, i: (b, i, 0)),
            _const_spec((1, D)),
            _const_spec((D, ROW_COLS)),
            _const_spec((TR_ROWS + DIFF_WIDTH, D)),
        ],
        out_specs=[
            pl.BlockSpec((1, tm, ROW_COLS), lambda b, i: (b, i, 0)),
            pl.BlockSpec((1, TR_ROWS, tm), lambda b, i: (b, 0, i)),
            pl.BlockSpec((1, DIFF_HEADS * VA_ROWS, tm), lambda b, i: (b, 0, i)),
        ],
        out_shape=[
            jax.ShapeDtypeStruct((B, S, ROW_COLS), BF16),
            jax.ShapeDtypeStruct((B, TR_ROWS, S), BF16),
            jax.ShapeDtypeStruct((B, DIFF_HEADS * VA_ROWS, S), BF16),
        ],
        compiler_params=_params("parallel", "parallel"),
        name="inproj",
    )(x, g, wrow, wtr)


def _knorm_kernel(k_ref, sel_ref, o_ref):
    k = k_ref[0].astype(F32)
    n2 = _dot((k * k).astype(BF16), sel_ref[...])
    o_ref[0] = jnp.sqrt(jnp.max(n2, axis=0, keepdims=True))


def _key_norm_max(row):
    B, S, _ = row.shape
    width = SB_WIDTH + DIFF_WIDTH
    sel = np.zeros((width, LANES), np.float32)
    sel[np.arange(width), np.arange(width) // HEAD_DIM] = 1.0
    return pl.pallas_call(
        _knorm_kernel,
        grid=(B,),
        in_specs=[pl.BlockSpec((1, S, width), lambda b: (b, 0, ROW_SBK * LANES // width)),
                  _const_spec((width, LANES))],
        out_specs=pl.BlockSpec((1, 1, LANES), lambda b: (b, 0, 0)),
        out_shape=jax.ShapeDtypeStruct((B, 1, LANES), F32),
        compiler_params=_params("parallel"),
        name="key_norm_max",
    )(row, jnp.asarray(sel, BF16))


def _sb_kernel(qt_ref, k_ref, vt_ref, tri_ref, kn_ref, o_ref):
    kb, qb = ATT_BLOCK, SB_QBLOCK
    nq = qb // kb
    p, i = pl.program_id(1), pl.program_id(2)
    qt = qt_ref[0]
    sub = lax.broadcasted_iota(jnp.int32, qt.shape, 0)
    zero = jnp.zeros_like(qt)
    qtm = (jnp.where(sub < HEAD_DIM, qt, zero), jnp.where(sub >= HEAD_DIM, qt, zero))
    tri = tri_ref[...]

    lane = lax.broadcasted_iota(jnp.int32, (1, LANES), 1)
    kn = kn_ref[0]
    zbound = []
    for h in range(2):
        knh = jnp.max(jnp.where(lane == 2 * p + h, kn, 0.0), axis=1, keepdims=True)
        qf = qtm[h].astype(F32)
        zbound.append(jnp.sqrt(jnp.sum(qf * qf, axis=0, keepdims=True)) * (knh * 1.01))

    def splice(old, new, lo, hi):
        parts = ([old[:, :lo]] if lo else []) + [new] + ([old[:, hi:]] if hi < qb else [])
        return parts[0] if len(parts) == 1 else jnp.concatenate(parts, axis=1)

    def tiles(jobs, st):
        carry, acc = [st[0], st[2]], [st[1], st[3]]
        ks = [pl.ds(pl.multiple_of(j * kb, kb), kb) for j, _, _, _ in jobs]
        units = [(n, h) for n in range(len(jobs)) for h in range(2)]
        z = {(n, h): _dot(k_ref[0, ks[n], :], qtm[h][:, jobs[n][1]:jobs[n][2]]) for n, h in units}
        sp = {u: jnp.maximum(z[u], 0.0) + jnp.log(1.0 + jnp.exp2(jnp.abs(z[u]) * -LOG2E)) for u in units}
        keep = []
        for n, (_, lo, hi, valid) in enumerate(jobs):
            m = (lax.broadcasted_iota(jnp.int32, (kb, hi - lo), 0)
                 < lax.broadcasted_iota(jnp.int32, (kb, hi - lo), 1)) if valid is None else valid
            keep.append(m)
            for h in range(2):
                sp[n, h] = jnp.where(m, sp[n, h], 0.0)
        cs = {u: _dot(tri, sp[u].astype(BF16)) for u in units}
        for n, (_, lo, hi, _) in enumerate(jobs):
            vt = vt_ref[0, :, ks[n]]
            c = [cs[n, h] + carry[h][:, lo:hi] for h in range(2)]
            a = [jnp.where(keep[n], jnp.exp(z[n, h] - c[h]), 0.0) for h in range(2)]
            pv = [_dot(vt[h * HEAD_DIM:(h + 1) * HEAD_DIM, :], a[h].astype(BF16)) for h in range(2)]
            for h in range(2):
                carry[h] = splice(carry[h], c[h][0:1, :], lo, hi)
                acc[h] = splice(acc[h], acc[h][:, lo:hi] + pv[h], lo, hi)
        return carry[0], acc[0], carry[1], acc[1]

    zc = jnp.zeros((1, qb), F32)
    za = jnp.zeros((HEAD_DIM, qb), F32)
    newest = [(nq * i + c, c * kb, min(c + 2, nq) * kb, None) for c in reversed(range(nq))]

    st = lax.cond(i > 0, lambda: tiles(newest + [(nq * i - 1, 0, kb, True)], (zc, za, zc, za)),
                  lambda: tiles(newest, (zc, za, zc, za)))

    def strip_step(j, cols, s):
        ks = pl.ds(pl.multiple_of(j * kb, kb), kb)
        k, vt = k_ref[0, ks, :], vt_ref[0, :, ks]
        z = [_dot(k, qtm[h][:, cols]) for h in range(2)]
        sp = [jnp.maximum(z[h], 0.0) + jnp.log(1.0 + jnp.exp2(jnp.abs(z[h]) * -LOG2E)) for h in range(2)]
        c = [_dot(tri, sp[h].astype(BF16)) + s[2 * h] for h in range(2)]
        a = [jnp.exp(z[h] - c[h]).astype(BF16) for h in range(2)]
        pv = [_dot(vt[h * HEAD_DIM:(h + 1) * HEAD_DIM, :], a[h]) for h in range(2)]
        return c[0][0:1, :], s[1] + pv[0], c[1][0:1, :], s[3] + pv[1]

    out = []
    for n in range(nq):
        cols = slice(n * kb, (n + 1) * kb)
        zb = (zbound[0][:, cols], zbound[1][:, cols])

        def unfinished(s, zb=zb):
            return (jnp.min(jnp.minimum(s[0] - zb[0], s[2] - zb[1])) <= SB_EXIT).astype(jnp.int32)

        def body(s, n=n, cols=cols, unfinished=unfinished):
            new = strip_step(nq * i + n - s[0], cols, s[2:])
            return (s[0] + 1, unfinished(new)) + new

        s0 = tuple(x[:, cols] for x in st)
        out.append(lax.while_loop(lambda s, n=n: jnp.logical_and(s[0] <= nq * i + n, s[1] > 0), body,
                                  (jnp.int32(2), unfinished(s0)) + s0)[2:])
    acc = [jnp.concatenate([o[1 + 2 * h] for o in out], axis=1) for h in range(2)]
    o_ref[0] = jnp.concatenate(acc, axis=0).T.astype(o_ref.dtype)


def _sb_attention(row, tr, tri, kn):
    B, S, _ = row.shape
    qb = SB_QBLOCK
    return pl.pallas_call(
        _sb_kernel,
        grid=(B, SB_WIDTH // LANES, S // qb),
        in_specs=[
            pl.BlockSpec((1, LANES, qb), lambda b, p, i: (b, TR_SBQ + p, i)),
            pl.BlockSpec((1, S, LANES), lambda b, p, i: (b, 0, ROW_SBK + p)),
            pl.BlockSpec((1, LANES, S), lambda b, p, i: (b, TR_SBV + p, 0)),
            _const_spec((ATT_BLOCK, ATT_BLOCK)),
            pl.BlockSpec((1, 1, LANES), lambda b, p, i: (b, 0, 0)),
        ],
        out_specs=pl.BlockSpec((1, qb, LANES), lambda b, p, i: (b, i, p)),
        out_shape=jax.ShapeDtypeStruct((B, S, SB_WIDTH), BF16),
        compiler_params=_params("parallel", "parallel", "arbitrary"),
        name="sb_attention",
    )(tr, row, tr, tri, kn)


def _bucket_tiles(blk):
    r = np.arange(blk)[None, :] - np.arange(blk)[:, None]
    rel = np.stack([r, r + blk, r + 2 * blk]).astype(np.int64)
    n = np.maximum(rel, 0)
    nf = np.maximum(n, 1).astype(np.float32)
    large = MAX_EXACT + (np.log(nf / np.float32(MAX_EXACT)) / np.float32(math.log(MAX_DISTANCE / MAX_EXACT))
                         * np.float32(NUM_BUCKETS - MAX_EXACT)).astype(np.int32)
    large = np.minimum(large, NUM_BUCKETS - 1)
    idx = np.where(n < MAX_EXACT, n, large).astype(np.int32)
    assert (idx[2] == NUM_BUCKETS - 1).all() and blk >= MAX_DISTANCE
    return idx


def _bias_kernel(table_ref, idx_ref, o_ref):
    h = pl.program_id(0)
    idx = idx_ref[0]
    acc = jnp.zeros(idx.shape, F32)
    for b in range(NUM_BUCKETS):
        acc = jnp.where(idx == b, table_ref[b, h], acc)
    o_ref[0, 0] = acc


def _bias_tiles(rel_bias, blk):
    idx = jnp.asarray(_bucket_tiles(blk))
    return pl.pallas_call(
        _bias_kernel,
        grid=(DIFF_HEADS, 3),
        in_specs=[
            pl.BlockSpec(memory_space=pltpu.SMEM),
            pl.BlockSpec((1, blk, blk), lambda h, d: (d, 0, 0)),
        ],
        out_specs=pl.BlockSpec((1, 1, blk, blk), lambda h, d: (h, d, 0, 0)),
        out_shape=jax.ShapeDtypeStruct((DIFF_HEADS, 3, blk, blk), F32),
        name="bias_tiles",
    )(rel_bias.astype(F32), idx)


def _diff_kernel(qt_ref, k_ref, va_ref, bias_ref, kn_ref, lam_ref, subln_ref, o_ref, *, lam_init):
    kb, qb = ATT_BLOCK, DF_QBLOCK
    nq = qb // kb
    h, i = pl.program_id(1), pl.program_id(2)
    qt = qt_ref[0]
    sub = lax.broadcasted_iota(jnp.int32, qt.shape, 0)
    zero = jnp.zeros_like(qt)
    qtm = (jnp.where(sub < HEAD_DIM, qt, zero), jnp.where(sub >= HEAD_DIM, qt, zero))
    far = bias_ref[0, 2][0:1, 0:1]

    def scores(j, lo):
        k = k_ref[0, pl.ds(pl.multiple_of(j * kb, kb), kb), :]
        return [_dot(k, qtm[m][:, lo:]) for m in range(2)]

    def update(z, j, st, lo, bias, mask, online):
        vta = va_ref[0, :, pl.ds(pl.multiple_of(j * kb, kb), kb)]
        if bias is not None:
            z = [x + bias for x in z]
        if mask is not None:
            z = [jnp.where(mask, x, NEG_INF) for x in z]
        mnew = [st[m][0][:, lo:] for m in range(2)]
        scale = None
        if online:
            top = [jnp.max(x, axis=0, keepdims=True) for x in z]
            if bias is None:
                top = [t + far for t in top]
            mold, mnew = mnew, [jnp.maximum(mnew[m], top[m]) for m in range(2)]
            scale = [jnp.exp(mold[m] - mnew[m]) for m in range(2)]
        shift = [mn - far for mn in mnew] if bias is None else mnew
        p = [jnp.exp(z[m] - shift[m]) for m in range(2)]
        pv = [_dot(vta, p[m].astype(BF16)) for m in range(2)]
        out = []
        for m in range(2):
            l, acc = st[m][1][:, lo:], st[m][2][:, lo:]
            if online:
                l, acc = scale[m] * l, scale[m] * acc
            new = (mnew[m], l + pv[m][LANES:LANES + 1, :], acc + pv[m][:LANES, :])
            if lo:
                new = tuple(jnp.concatenate([old[:, :lo], x], axis=1) for old, x in zip(st[m], new))
            out.append(new)
        return tuple(out)

    def run(stab, online):
        zl, za = jnp.zeros((1, qb), F32), jnp.zeros((LANES, qb), F32)
        st = ((stab[0], zl, za), (stab[1], zl, za))
        for r in reversed(range(nq)):
            n = qb - r * kb
            bias = jnp.concatenate([bias_ref[0, min(c - r, 2)] for c in range(r, nq)], axis=1)
            causal = lax.broadcasted_iota(jnp.int32, (kb, n), 0) <= lax.broadcasted_iota(jnp.int32, (kb, n), 1)
            st = update(scores(nq * i + r, r * kb), nq * i + r, st, r * kb, bias, causal, online)
        def older(st):
            bias = jnp.concatenate([bias_ref[0, min(c + 1, 2)] for c in range(nq)], axis=1)
            st = update(scores(nq * i - 1, 0), nq * i - 1, st, 0, bias, None, online)
            for d in range(2, DF_UNROLL + 1):
                st = update(scores(nq * i - d, 0), nq * i - d, st, 0, None, None, online)

            def body(jj, s):
                for d in range(DF_UNROLL):
                    j = nq * i - DF_UNROLL * (jj + 1) - 1 - d
                    s = update(scores(j, 0), j, s, 0, None, None, online)
                return s

            return lax.fori_loop(0, (nq * i) // DF_UNROLL - 1, body, st)

        return lax.cond(i > 0, older, lambda s: s, st)

    lane = lax.broadcasted_iota(jnp.int32, (1, LANES), 1)
    kn = kn_ref[0]
    allb = bias_ref[0]
    bmax = jnp.max(jnp.max(allb, axis=0), axis=(0, 1), keepdims=True)
    bmin = jnp.min(jnp.min(allb, axis=0), axis=(0, 1), keepdims=True)
    qk = []
    for m in range(2):
        knm = jnp.max(jnp.where(lane == SB_WIDTH // HEAD_DIM + 2 * h + m, kn, 0.0), axis=1, keepdims=True)
        qf = qtm[m].astype(F32)
        qk.append(jnp.sqrt(jnp.sum(qf * qf, axis=0, keepdims=True)) * (knm * 1.01))
    spread = 2.0 * jnp.maximum(qk[0], qk[1]) + (bmax - bmin)
    bound = [x + bmax for x in qk]
    neg = jnp.full((1, qb), NEG_INF, F32)
    st = lax.cond(jnp.max(spread) <= DF_SPREAD,
                  lambda: run(bound, False), lambda: run([neg, neg], True))

    lp = lam_ref[...]
    lam = (jnp.exp(jnp.sum(lp[0:1] * lp[1:2], axis=1, keepdims=True))
           - jnp.exp(jnp.sum(lp[2:3] * lp[3:4], axis=1, keepdims=True)) + lam_init)
    (_, l0, acc0), (_, l1, acc1) = st
    o = (acc0 / l0 - lam * (acc1 / l1)).T
    o_ref[0] = (_rmsnorm(o, subln_ref[...]) * (1.0 - lam_init)).astype(o_ref.dtype)


def _diff_attention(row, tr, va, bias, kn, lam_p, subln, lam_init):
    B, S, _ = row.shape
    kb, qb = ATT_BLOCK, DF_QBLOCK
    return pl.pallas_call(
        functools.partial(_diff_kernel, lam_init=lam_init),
        grid=(B, DIFF_HEADS, S // qb),
        in_specs=[
            pl.BlockSpec((1, LANES, qb), lambda b, h, i: (b, TR_DFQ + h, i)),
            pl.BlockSpec((1, S, LANES), lambda b, h, i: (b, 0, ROW_DFK + h)),
            pl.BlockSpec((1, VA_ROWS, S), lambda b, h, i: (b, h, 0)),
            pl.BlockSpec((1, 3, kb, kb), lambda b, h, i: (h, 0, 0, 0)),
            pl.BlockSpec((1, 1, LANES), lambda b, h, i: (b, 0, 0)),
            _const_spec((4, HEAD_DIM)),
            _const_spec((1, 2 * HEAD_DIM)),
        ],
        out_specs=pl.BlockSpec((1, qb, LANES), lambda b, h, i: (b, i, h)),
        out_shape=jax.ShapeDtypeStruct((B, S, DIFF_WIDTH), BF16),
        compiler_params=_params("parallel", "parallel", "arbitrary"),
        name="diff_attention",
    )(tr, row, va, bias, kn, lam_p, subln)


def _conv_kernel(u_ref, uprev_ref, cw_ref, cb_ref, lg_ref, lb_ref, o_ref, h_ref, hs_ref, *, tc):
    i = pl.program_id(1)

    def glu(u):
        u = u.astype(F32)
        return u[:, :CONV_CH] * jax.nn.sigmoid(u[:, CONV_CH:])

    h_ref[0:CONV_HALO, :] = jnp.where(i == 0, 0.0, glu(uprev_ref[0]))
    h_ref[CONV_HALO:CONV_HALO + tc, :] = glu(u_ref[0])
    for b in range(1, SUBLANES):
        hs_ref[b - 1] = h_ref[b:b + tc + CONV_HALO - SUBLANES, :]
    first = CONV_HALO - (CONV_WIDTH - 1)
    for c in range(tc // CONV_ROWS):
        acc = jnp.zeros((CONV_ROWS, CONV_CH), F32)
        for w in range(CONV_WIDTH):
            a, b = divmod(first + w, SUBLANES)
            r0 = c * CONV_ROWS + a * SUBLANES
            src = h_ref[r0:r0 + CONV_ROWS, :] if b == 0 else hs_ref[b - 1, r0:r0 + CONV_ROWS, :]
            acc = acc + src * cw_ref[w:w + 1, :]
        y = acc + cb_ref[...]
        mu = jnp.mean(y, axis=-1, keepdims=True)
        var = jnp.mean(jnp.square(y - mu), axis=-1, keepdims=True)
        y = (y - mu) * lax.rsqrt(var + EPS) * lg_ref[...] + lb_ref[...]
        o_ref[0, c * CONV_ROWS:(c + 1) * CONV_ROWS, :] = (y * jax.nn.sigmoid(y)).astype(o_ref.dtype)


def _conv(row, cw, cb, lg, lb, tc):
    B, S, _ = row.shape
    ucol = 0
    per = tc // CONV_HALO
    return pl.pallas_call(
        functools.partial(_conv_kernel, tc=tc),
        grid=(B, S // tc),
        in_specs=[
            pl.BlockSpec((1, tc, 2 * CONV_CH), lambda b, i: (b, i, ucol)),
            pl.BlockSpec((1, CONV_HALO, 2 * CONV_CH), lambda b, i: (b, jnp.maximum(i * per - 1, 0), ucol)),
            _const_spec((CONV_WIDTH, CONV_CH)),
            _const_spec((1, CONV_CH)),
            _const_spec((1, CONV_CH)),
            _const_spec((1, CONV_CH)),
        ],
        out_specs=pl.BlockSpec((1, tc, CONV_CH), lambda b, i: (b, i, 0)),
        out_shape=jax.ShapeDtypeStruct((B, S, CONV_CH), BF16),
        scratch_shapes=[pltpu.VMEM((CONV_HALO + tc, CONV_CH), F32),
                        pltpu.VMEM((SUBLANES - 1, CONV_HALO + tc - SUBLANES, CONV_CH), F32)],
        compiler_params=_params("parallel", "parallel"),
        name="conformer_conv",
    )(row, row, cw, cb, lg, lb)


def _merge_kernel(x_ref, g_ref, wg_ref, osb_ref, odf_ref, ocv_ref, wsb_ref, wdf_ref, wcv_ref, wout_ref, o_ref):
    x = x_ref[...]
    D = x.shape[1]
    xb = _rmsnorm(x, g_ref[...]).astype(BF16)
    y = None
    for n, (o_br, w_br) in enumerate(((osb_ref, wsb_ref), (odf_ref, wdf_ref), (ocv_ref, wcv_ref))):
        gate = jax.nn.sigmoid(_dot(xb, wg_ref[:, n * D:(n + 1) * D]))
        t = gate * _dot(o_br[...], w_br[...])
        y = t if y is None else y + t
    o_ref[...] = x + _dot(y.astype(BF16), wout_ref[...])


def _merge(x2, g, wg, osb, odf, ocv, wsb, wdf, wcv, wout, tm):
    N, D = x2.shape
    W = osb.shape[1]
    tok = lambda w: pl.BlockSpec((tm, w), lambda i: (i, 0))
    return pl.pallas_call(
        _merge_kernel,
        grid=(N // tm,),
        in_specs=[tok(D), _const_spec((1, D)), _const_spec((D, 3 * D)), tok(W), tok(W), tok(W),
                  _const_spec((W, D)), _const_spec((W, D)), _const_spec((W, D)), _const_spec((D, D))],
        out_specs=tok(D),
        out_shape=jax.ShapeDtypeStruct((N, D), F32),
        compiler_params=_params("parallel"),
        name="gated_merge",
    )(x2, g, wg, osb, odf, ocv, wsb, wdf, wcv, wout)


def _memkv_kernel(mem_ref, g_ref, wkv_ref, o_ref):
    mb = _rmsnorm(mem_ref[0], g_ref[...]).astype(BF16)
    o_ref[0] = _dot(mb, wkv_ref[...]).astype(o_ref.dtype)


def _memkv(mem, g, wkv):
    B, M, D = mem.shape
    return pl.pallas_call(
        _memkv_kernel,
        grid=(B,),
        in_specs=[pl.BlockSpec((1, M, D), lambda b: (b, 0, 0)), _const_spec((1, D)), _const_spec((D, 2 * XA_WIDTH))],
        out_specs=pl.BlockSpec((1, M, 2 * XA_WIDTH), lambda b: (b, 0, 0)),
        out_shape=jax.ShapeDtypeStruct((B, M, 2 * XA_WIDTH), BF16),
        compiler_params=_params("parallel"),
        name="mem_kv",
    )(mem, g, wkv)


def _xattn_kernel(x_ref, g_ref, wq_ref, kv_ref, wo_ref, o_ref):
    x = x_ref[0]
    xb = _rmsnorm(x, g_ref[...]).astype(BF16)
    q = _dot(xb, wq_ref[...]).astype(BF16)
    heads = []
    for h in range(XA_HEADS):
        hs = slice(h * XA_HEAD_DIM, (h + 1) * XA_HEAD_DIM)
        k = kv_ref[0, :, hs]
        v = kv_ref[0, :, XA_WIDTH + h * XA_HEAD_DIM:XA_WIDTH + (h + 1) * XA_HEAD_DIM]
        s = _dot_nt(q[:, hs], k) * (XA_HEAD_DIM ** -0.5)
        p = jnp.exp(s - jnp.max(s, axis=1, keepdims=True))
        p = p / jnp.sum(p, axis=1, keepdims=True)
        heads.append(_dot(p.astype(BF16), v).astype(BF16))
    o = jnp.concatenate(heads, axis=1)
    o_ref[0] = x + _dot(o, wo_ref[...])


def _xattn(x, g, wq, kv, wo, tm):
    B, S, D = x.shape
    M = kv.shape[1]
    return pl.pallas_call(
        _xattn_kernel,
        grid=(B, S // tm),
        in_specs=[
            pl.BlockSpec((1, tm, D), lambda b, i: (b, i, 0)),
            _const_spec((1, D)),
            _const_spec((D, XA_WIDTH)),
            pl.BlockSpec((1, M, 2 * XA_WIDTH), lambda b, i: (b, 0, 0)),
            _const_spec((XA_WIDTH, D)),
        ],
        out_specs=pl.BlockSpec((1, tm, D), lambda b, i: (b, i, 0)),
        out_shape=jax.ShapeDtypeStruct((B, S, D), F32),
        compiler_params=_params("parallel", "parallel"),
        name="cross_attention",
    )(x, g, wq, kv, wo)


def _mlp_kernel(x_ref, g_ref, wup_ref, wdown_ref, gf_ref, o_ref, *, fc, final):
    x = x_ref[...]
    xb = _rmsnorm(x, g_ref[...]).astype(BF16)
    acc = x
    for c in range(wup_ref.shape[1] // fc):
        cs = slice(c * fc, (c + 1) * fc)
        h = jnp.square(jnp.maximum(_dot(xb, wup_ref[:, cs]), 0.0)).astype(BF16)
        acc = acc + _dot(h, wdown_ref[cs, :])
    o_ref[...] = _rmsnorm(acc, gf_ref[...]) if final else acc


def _mlp(x2, g, wup, wdown, gf, tm, final):
    N, D = x2.shape
    F = wup.shape[1]
    return pl.pallas_call(
        functools.partial(_mlp_kernel, fc=1024, final=final),
        grid=(N // tm,),
        in_specs=[pl.BlockSpec((tm, D), lambda i: (i, 0)), _const_spec((1, D)),
                  _const_spec((D, F)), _const_spec((F, D)), _const_spec((1, D))],
        out_specs=pl.BlockSpec((tm, D), lambda i: (i, 0)),
        out_shape=jax.ShapeDtypeStruct((N, D), F32),
        compiler_params=_params("parallel"),
        name="sqrelu_mlp",
    )(x2, g, wup, wdown, gf)


def kernel(x, mem, rel_bias, ln_mix, w_in, diff_lambda, diff_subln, conv_w, conv_b, conv_ln_g, conv_ln_b,
           w_sb_proj, w_diff_proj, w_conv_proj, w_out, ln_xattn, ln_mem, xa_w_q, xa_w_kv, xa_w_o,
           ln_mlp, w_up, w_down, ln_final):
    B, S, D = x.shape
    depth = w_in.shape[0]
    blk = ATT_BLOCK
    tm = min(512, S)
    tc = min(256, S)
    assert S % SB_QBLOCK == 0 and S % DF_QBLOCK == 0 and S % tm == 0 and S % tc == 0 and D == 1024

    row2 = lambda a: a.reshape(1, -1).astype(F32)
    tri = jnp.asarray(np.triu(np.ones((blk, blk), np.float32)), BF16)
    bias = _bias_tiles(rel_bias, blk)
    scale = HEAD_DIM ** -0.5

    for l in range(depth):
        w = w_in[l]
        sbq, sbk, sbv = w[:, 0:512] * scale, w[:, 512:1024], w[:, 1024:1536]
        dfq, dfk, dfv = w[:, 1536:2048] * scale, w[:, 2048:2560], w[:, 2560:3072]
        wrow = jnp.concatenate([w[:, 3072:4096], sbk, dfk], axis=1).astype(BF16)
        wtr = jnp.concatenate([sbq, sbv, dfq, dfv], axis=1).T.astype(BF16)
        wg = w[:, 4096:].astype(BF16)

        row, tr, va = _inproj(x, row2(ln_mix[l]), wrow, wtr, tm)
        kn = _key_norm_max(row)
        o_sb = _sb_attention(row, tr, tri, kn)
        lam_init = 0.8 - 0.6 * math.exp(-0.3 * l)
        o_df = _diff_attention(row, tr, va, bias, kn, diff_lambda[l].astype(F32), row2(diff_subln[l]), lam_init)
        o_cv = _conv(row, conv_w[l].reshape(CONV_WIDTH, CONV_CH).astype(F32), row2(conv_b[l]),
                     row2(conv_ln_g[l]), row2(conv_ln_b[l]), tc)

        N = B * S
        x2 = _merge(x.reshape(N, D), row2(ln_mix[l]), wg,
                    o_sb.reshape(N, -1), o_df.reshape(N, -1), o_cv.reshape(N, -1),
                    w_sb_proj[l].astype(BF16), w_diff_proj[l].astype(BF16), w_conv_proj[l].astype(BF16),
                    w_out[l].astype(BF16), tm)

        kv = _memkv(mem, row2(ln_mem[l]), xa_w_kv[l].astype(BF16))
        x3 = _xattn(x2.reshape(B, S, D), row2(ln_xattn[l]),
                    xa_w_q[l].astype(BF16), kv, xa_w_o[l].astype(BF16), tm)

        x = _mlp(x3.reshape(N, D), row2(ln_mlp[l]), w_up[l].astype(BF16), w_down[l].astype(BF16),
                 row2(ln_final), tm, final=(l == depth - 1)).reshape(B, S, D)
    return x
```

```python
import functools
import math

import numpy as np
import jax
import jax.numpy as jnp
from jax import lax
from jax.experimental import pallas as pl
from jax.experimental.pallas import tpu as pltpu

F32 = jnp.float32
BF16 = jnp.bfloat16

HEAD_DIM = 64
SB_WIDTH = 512
DIFF_WIDTH = 512
DIFF_HEADS = 4
CONV_CH = 512
CONV_WIDTH = 31
XA_HEADS = 4
XA_HEAD_DIM = 128
XA_WIDTH = XA_HEADS * XA_HEAD_DIM
NUM_BUCKETS = 32
MAX_EXACT = NUM_BUCKETS // 2
MAX_DISTANCE = 128
EPS = 1e-6
NEG_INF = -1e30

LANES = 128
SUBLANES = 8
ATT_BLOCK = 256
CONV_HALO = 32
CONV_ROWS = 64
VMEM_LIMIT = 56 * 1024 * 1024

SB_QBLOCK = 1024
DF_QBLOCK = 1024
SB_EXIT = 110.0
DF_UNROLL = 4
DF_SPREAD = 80.0
LOG2E = 1.4426950408889634
ONES_ROWS = 16

ROW_COLS = 2 * CONV_CH + 2 * 512
ROW_SBK, ROW_DFK = 8, 12
TR_ROWS = 3 * 512
TR_SBQ, TR_SBV, TR_DFQ = 0, 4, 8
VA_ROWS = LANES + ONES_ROWS


def _params(*sem):
    return pltpu.CompilerParams(dimension_semantics=sem, vmem_limit_bytes=VMEM_LIMIT)


def _const_spec(shape):
    nd = len(shape)
    return pl.BlockSpec(shape, lambda *_: (0,) * nd)


def _rmsnorm(x, g):
    return x * lax.rsqrt(jnp.mean(x * x, axis=-1, keepdims=True) + EPS) * g


def _dot(a, b):
    return jnp.dot(a, b, preferred_element_type=F32)


def _dot_nt(a, b):
    return lax.dot_general(a, b, (((1,), (1,)), ((), ())), preferred_element_type=F32)


def _inproj_kernel(x_ref, g_ref, wrow_ref, wtr_ref, row_ref, tr_ref, va_ref):
    xb = _rmsnorm(x_ref[0], g_ref[...]).astype(BF16)
    tm = xb.shape[0]
    for c in range(ROW_COLS // 512):
        cs = slice(c * 512, (c + 1) * 512)
        row_ref[0, :, cs] = _dot(xb, wrow_ref[:, cs]).astype(BF16)
    for c in range(TR_ROWS // 512):
        cs = slice(c * 512, (c + 1) * 512)
        tr_ref[0, cs, :] = _dot_nt(wtr_ref[cs, :], xb).astype(BF16)
    dfv = _dot_nt(wtr_ref[TR_ROWS:, :], xb).astype(BF16)
    for h in range(DIFF_HEADS):
        va_ref[0, h * VA_ROWS:h * VA_ROWS + LANES, :] = dfv[h * LANES:(h + 1) * LANES, :]
        va_ref[0, h * VA_ROWS + LANES:(h + 1) * VA_ROWS, :] = jnp.ones((ONES_ROWS, tm), BF16)


def _inproj(x, g, wrow, wtr, tm):
    B, S, D = x.shape
    return pl.pallas_call(
        _inproj_kernel,
        grid=(B, S // tm),
        in_specs=[
            pl.BlockSpec((1, tm, D), lambda b, i: (b, i, 0)),
            _const_spec((1, D)),
            _const_spec((D, ROW_COLS)),
            _const_spec((TR_ROWS + DIFF_WIDTH, D)),
        ],
        out_specs=[
            pl.BlockSpec((1, tm, ROW_COLS), lambda b, i: (b, i, 0)),
            pl.BlockSpec((1, TR_ROWS, tm), lambda b, i: (b, 0, i)),
            pl.BlockSpec((1, DIFF_HEADS * VA_ROWS, tm), lambda b, i: (b, 0, i)),
        ],
        out_shape=[
            jax.ShapeDtypeStruct((B, S, ROW_COLS), BF16),
            jax.ShapeDtypeStruct((B, TR_ROWS, S), BF16),
            jax.ShapeDtypeStruct((B, DIFF_HEADS * VA_ROWS, S), BF16),
        ],
        compiler_params=_params("parallel", "parallel"),
        name="inproj",
    )(x, g, wrow, wtr)


def _knorm_kernel(k_ref, sel_ref, o_ref):
    k = k_ref[0].astype(F32)
    n2 = _dot((k * k).astype(BF16), sel_ref[...])
    o_ref[0] = jnp.sqrt(jnp.max(n2, axis=0, keepdims=True))


def _key_norm_max(row):
    B, S, _ = row.shape
    width = SB_WIDTH + DIFF_WIDTH
    sel = np.zeros((width, LANES), np.float32)
    sel[np.arange(width), np.arange(width) // HEAD_DIM] = 1.0
    return pl.pallas_call(
        _knorm_kernel,
        grid=(B,),
        in_specs=[pl.BlockSpec((1, S, width), lambda b: (b, 0, ROW_SBK * LANES // width)),
                  _const_spec((width, LANES))],
        out_specs=pl.BlockSpec((1, 1, LANES), lambda b: (b, 0, 0)),
        out_shape=jax.ShapeDtypeStruct((B, 1, LANES), F32),
        compiler_params=_params("parallel"),
        name="key_norm_max",
    )(row, jnp.asarray(sel, BF16))


def _sb_kernel(qt_ref, k_ref, vt_ref, tri_ref, kn_ref, o_ref):
    kb, qb = ATT_BLOCK, SB_QBLOCK
    nq = qb // kb
    p, i = pl.program_id(1), pl.program_id(2)
    qt = qt_ref[0]
    sub = lax.broadcasted_iota(jnp.int32, qt.shape, 0)
    zero = jnp.zeros_like(qt)
    qtm = (jnp.where(sub < HEAD_DIM, qt, zero), jnp.where(sub >= HEAD_DIM, qt, zero))
    tri = tri_ref[...]

    lane = lax.broadcasted_iota(jnp.int32, (1, LANES), 1)
    kn = kn_ref[0]
    zbound = []
    for h in range(2):
        knh = jnp.max(jnp.where(lane == 2 * p + h, kn, 0.0), axis=1, keepdims=True)
        qf = qtm[h].astype(F32)
        zbound.append(jnp.sqrt(jnp.sum(qf * qf, axis=0, keepdims=True)) * (knh * 1.01))

    def splice(old, new, lo, hi):
        parts = ([old[:, :lo]] if lo else []) + [new] + ([old[:, hi:]] if hi < qb else [])
        return parts[0] if len(parts) == 1 else jnp.concatenate(parts, axis=1)

    def tiles(jobs, st):
        carry, acc = [st[0], st[2]], [st[1], st[3]]
        ks = [pl.ds(pl.multiple_of(j * kb, kb), kb) for j, _, _, _ in jobs]
        units = [(n, h) for n in range(len(jobs)) for h in range(2)]
        z = {(n, h): _dot(k_ref[0, ks[n], :], qtm[h][:, jobs[n][1]:jobs[n][2]]) for n, h in units}
        sp = {u: jnp.maximum(z[u], 0.0) + jnp.log(1.0 + jnp.exp2(jnp.abs(z[u]) * -LOG2E)) for u in units}
        own = {u: z[u] - sp[u] for u in units}
        keep = []
        for n, (_, lo, hi, valid) in enumerate(jobs):
            m = (lax.broadcasted_iota(jnp.int32, (kb, hi - lo), 0)
                 < lax.broadcasted_iota(jnp.int32, (kb, hi - lo), 1)) if valid is None else valid
            keep.append(m)
            for h in range(2):
                sp[n, h] = jnp.where(m, sp[n, h], 0.0)
        cs = {u: _dot(tri, sp[u].astype(BF16)) for u in units}
        oldest_row = {u: sp[u][0:1, :] for u in units}
        for n, (_, lo, hi, _) in enumerate(jobs):
            vt = vt_ref[0, :, ks[n]]
            c = [cs[n, h] + carry[h][:, lo:hi] for h in range(2)]
            a = [jnp.where(keep[n], jnp.exp(own[n, h] - c[h]), 0.0) for h in range(2)]
            pv = [_dot(vt[h * HEAD_DIM:(h + 1) * HEAD_DIM, :], a[h].astype(BF16)) for h in range(2)]
            for h in range(2):
                carry[h] = splice(carry[h], c[h][0:1, :] + oldest_row[n, h], lo, hi)
                acc[h] = splice(acc[h], acc[h][:, lo:hi] + pv[h], lo, hi)
        return carry[0], acc[0], carry[1], acc[1]

    zc = jnp.zeros((1, qb), F32)
    za = jnp.zeros((HEAD_DIM, qb), F32)
    newest = [(nq * i + c, c * kb, min(c + 2, nq) * kb, None) for c in reversed(range(nq))]

    st = lax.cond(i > 0, lambda: tiles(newest + [(nq * i - 1, 0, kb, True)], (zc, za, zc, za)),
                  lambda: tiles(newest, (zc, za, zc, za)))

    def strip_step(j, cols, s):
        ks = pl.ds(pl.multiple_of(j * kb, kb), kb)
        k, vt = k_ref[0, ks, :], vt_ref[0, :, ks]
        z = [_dot(k, qtm[h][:, cols]) for h in range(2)]
        sp = [jnp.maximum(z[h], 0.0) + jnp.log(1.0 + jnp.exp2(jnp.abs(z[h]) * -LOG2E)) for h in range(2)]
        c = [_dot(tri, sp[h].astype(BF16)) + s[2 * h] for h in range(2)]
        a = [jnp.exp((z[h] - sp[h]) - c[h]).astype(BF16) for h in range(2)]
        pv = [_dot(vt[h * HEAD_DIM:(h + 1) * HEAD_DIM, :], a[h]) for h in range(2)]
        return (c[0][0:1, :] + sp[0][0:1, :], s[1] + pv[0], c[1][0:1, :] + sp[1][0:1, :], s[3] + pv[1])

    out = []
    for n in range(nq):
        cols = slice(n * kb, (n + 1) * kb)
        zb = (zbound[0][:, cols], zbound[1][:, cols])

        def unfinished(s, zb=zb):
            return (jnp.min(jnp.minimum(s[0] - zb[0], s[2] - zb[1])) <= SB_EXIT).astype(jnp.int32)

        def body(s, n=n, cols=cols, unfinished=unfinished):
            new = strip_step(nq * i + n - s[0], cols, s[2:])
            return (s[0] + 1, unfinished(new)) + new

        s0 = tuple(x[:, cols] for x in st)
        out.append(lax.while_loop(lambda s, n=n: jnp.logical_and(s[0] <= nq * i + n, s[1] > 0), body,
                                  (jnp.int32(2), unfinished(s0)) + s0)[2:])
    acc = [jnp.concatenate([o[1 + 2 * h] for o in out], axis=1) for h in range(2)]
    o_ref[0] = jnp.concatenate(acc, axis=0).T.astype(o_ref.dtype)


def _sb_attention(row, tr, tri, kn):
    B, S, _ = row.shape
    qb = SB_QBLOCK
    return pl.pallas_call(
        _sb_kernel,
        grid=(B, SB_WIDTH // LANES, S // qb),
        in_specs=[
            pl.BlockSpec((1, LANES, qb), lambda b, p, i: (b, TR_SBQ + p, i)),
            pl.BlockSpec((1, S, LANES), lambda b, p, i: (b, 0, ROW_SBK + p)),
            pl.BlockSpec((1, LANES, S), lambda b, p, i: (b, TR_SBV + p, 0)),
            _const_spec((ATT_BLOCK, ATT_BLOCK)),
            pl.BlockSpec((1, 1, LANES), lambda b, p, i: (b, 0, 0)),
        ],
        out_specs=pl.BlockSpec((1, qb, LANES), lambda b, p, i: (b, i, p)),
        out_shape=jax.ShapeDtypeStruct((B, S, SB_WIDTH), BF16),
        compiler_params=_params("parallel", "parallel", "arbitrary"),
        name="sb_attention",
    )(tr, row, tr, tri, kn)


def _bucket_tiles(blk):
    r = np.arange(blk)[None, :] - np.arange(blk)[:, None]
    rel = np.stack([r, r + blk, r + 2 * blk]).astype(np.int64)
    n = np.maximum(rel, 0)
    nf = np.maximum(n, 1).astype(np.float32)
    large = MAX_EXACT + (np.log(nf / np.float32(MAX_EXACT)) / np.float32(math.log(MAX_DISTANCE / MAX_EXACT))
                         * np.float32(NUM_BUCKETS - MAX_EXACT)).astype(np.int32)
    large = np.minimum(large, NUM_BUCKETS - 1)
    idx = np.where(n < MAX_EXACT, n, large).astype(np.int32)
    assert (idx[2] == NUM_BUCKETS - 1).all() and blk >= MAX_DISTANCE
    return idx


def _bias_kernel(table_ref, idx_ref, o_ref):
    h = pl.program_id(0)
    idx = idx_ref[0]
    acc = jnp.zeros(idx.shape, F32)
    for b in range(NUM_BUCKETS):
        acc = jnp.where(idx == b, table_ref[b, h], acc)
    o_ref[0, 0] = acc


def _bias_tiles(rel_bias, blk):
    idx = jnp.asarray(_bucket_tiles(blk))
    return pl.pallas_call(
        _bias_kernel,
        grid=(DIFF_HEADS, 3),
        in_specs=[
            pl.BlockSpec(memory_space=pltpu.SMEM),
            pl.BlockSpec((1, blk, blk), lambda h, d: (d, 0, 0)),
        ],
        out_specs=pl.BlockSpec((1, 1, blk, blk), lambda h, d: (h, d, 0, 0)),
        out_shape=jax.ShapeDtypeStruct((DIFF_HEADS, 3, blk, blk), F32),
        name="bias_tiles",
    )(rel_bias.astype(F32), idx)


def _diff_kernel(qt_ref, k_ref, va_ref, bias_ref, kn_ref, lam_ref, subln_ref, o_ref, *, lam_init):
    kb, qb = ATT_BLOCK, DF_QBLOCK
    nq = qb // kb
    h, i = pl.program_id(1), pl.program_id(2)
    qt = qt_ref[0]
    sub = lax.broadcasted_iota(jnp.int32, qt.shape, 0)
    zero = jnp.zeros_like(qt)
    qtm = (jnp.where(sub < HEAD_DIM, qt, zero), jnp.where(sub >= HEAD_DIM, qt, zero))
    far = bias_ref[0, 2][0:1, 0:1]

    def scores(j, lo):
        k = k_ref[0, pl.ds(pl.multiple_of(j * kb, kb), kb), :]
        return [_dot(k, qtm[m][:, lo:]) for m in range(2)]

    def update(z, j, st, lo, bias, mask, online):
        vta = va_ref[0, :, pl.ds(pl.multiple_of(j * kb, kb), kb)]
        if bias is not None:
            z = [x + bias for x in z]
        if mask is not None:
            z = [jnp.where(mask, x, NEG_INF) for x in z]
        mnew = [st[m][0][:, lo:] for m in range(2)]
        scale = None
        if online:
            top = [jnp.max(x, axis=0, keepdims=True) for x in z]
            if bias is None:
                top = [t + far for t in top]
            mold, mnew = mnew, [jnp.maximum(mnew[m], top[m]) for m in range(2)]
            scale = [jnp.exp(mold[m] - mnew[m]) for m in range(2)]
        shift = [mn - far for mn in mnew] if bias is None else mnew
        p = [jnp.exp(z[m] - shift[m]) for m in range(2)]
        pv = [_dot(vta, p[m].astype(BF16)) for m in range(2)]
        out = []
        for m in range(2):
            l, acc = st[m][1][:, lo:], st[m][2][:, lo:]
            if online:
                l, acc = scale[m] * l, scale[m] * acc
            new = (mnew[m], l + pv[m][LANES:LANES + 1, :], acc + pv[m][:LANES, :])
            if lo:
                new = tuple(jnp.concatenate([old[:, :lo], x], axis=1) for old, x in zip(st[m], new))
            out.append(new)
        return tuple(out)

    def run(stab, online):
        zl, za = jnp.zeros((1, qb), F32), jnp.zeros((LANES, qb), F32)
        st = ((stab[0], zl, za), (stab[1], zl, za))
        for r in reversed(range(nq)):
            n = qb - r * kb
            bias = jnp.concatenate([bias_ref[0, min(c - r, 2)] for c in range(r, nq)], axis=1)
            causal = lax.broadcasted_iota(jnp.int32, (kb, n), 0) <= lax.broadcasted_iota(jnp.int32, (kb, n), 1)
            st = update(scores(nq * i + r, r * kb), nq * i + r, st, r * kb, bias, causal, online)
        def older(st):
            bias = jnp.concatenate([bias_ref[0, min(c + 1, 2)] for c in range(nq)], axis=1)
            st = update(scores(nq * i - 1, 0), nq * i - 1, st, 0, bias, None, online)
            for d in range(2, DF_UNROLL + 1):
                st = update(scores(nq * i - d, 0), nq * i - d, st, 0, None, None, online)

            def body(jj, s):
                for d in range(DF_UNROLL):
                    j = nq * i - DF_UNROLL * (jj + 1) - 1 - d
                    s = update(scores(j, 0), j, s, 0, None, None, online)
                return s

            return lax.fori_loop(0, (nq * i) // DF_UNROLL - 1, body, st)

        return lax.cond(i > 0, older, lambda s: s, st)

    lane = lax.broadcasted_iota(jnp.int32, (1, LANES), 1)
    kn = kn_ref[0]
    allb = bias_ref[0]
    bmax = jnp.max(jnp.max(allb, axis=0), axis=(0, 1), keepdims=True)
    bmin = jnp.min(jnp.min(allb, axis=0), axis=(0, 1), keepdims=True)
    qk = []
    for m in range(2):
        knm = jnp.max(jnp.where(lane == SB_WIDTH // HEAD_DIM + 2 * h + m, kn, 0.0), axis=1, keepdims=True)
        qf = qtm[m].astype(F32)
        qk.append(jnp.sqrt(jnp.sum(qf * qf, axis=0, keepdims=True)) * (knm * 1.01))
    spread = 2.0 * jnp.maximum(qk[0], qk[1]) + (bmax - bmin)
    bound = [x + bmax for x in qk]
    neg = jnp.full((1, qb), NEG_INF, F32)
    st = lax.cond(jnp.max(spread) <= DF_SPREAD,
                  lambda: run(bound, False), lambda: run([neg, neg], True))

    lp = lam_ref[...]
    lam = (jnp.exp(jnp.sum(lp[0:1] * lp[1:2], axis=1, keepdims=True))
           - jnp.exp(jnp.sum(lp[2:3] * lp[3:4], axis=1, keepdims=True)) + lam_init)
    (_, l0, acc0), (_, l1, acc1) = st
    o = (acc0 / l0 - lam * (acc1 / l1)).T
    o_ref[0] = (_rmsnorm(o, subln_ref[...]) * (1.0 - lam_init)).astype(o_ref.dtype)


def _diff_attention(row, tr, va, bias, kn, lam_p, subln, lam_init):
    B, S, _ = row.shape
    kb, qb = ATT_BLOCK, DF_QBLOCK
    return pl.pallas_call(
        functools.partial(_diff_kernel, lam_init=lam_init),
        grid=(B, DIFF_HEADS, S // qb),
        in_specs=[
            pl.BlockSpec((1, LANES, qb), lambda b, h, i: (b, TR_DFQ + h, i)),
            pl.BlockSpec((1, S, LANES), lambda b, h, i: (b, 0, ROW_DFK + h)),
            pl.BlockSpec((1, VA_ROWS, S), lambda b, h, i: (b, h, 0)),
            pl.BlockSpec((1, 3, kb, kb), lambda b, h, i: (h, 0, 0, 0)),
            pl.BlockSpec((1, 1, LANES), lambda b, h, i: (b, 0, 0)),
            _const_spec((4, HEAD_DIM)),
            _const_spec((1, 2 * HEAD_DIM)),
        ],
        out_specs=pl.BlockSpec((1, qb, LANES), lambda b, h, i: (b, i, h)),
        out_shape=jax.ShapeDtypeStruct((B, S, DIFF_WIDTH), BF16),
        compiler_params=_params("parallel", "parallel", "arbitrary"),
        name="diff_attention",
    )(tr, row, va, bias, kn, lam_p, subln)


def _conv_kernel(u_ref, uprev_ref, cw_ref, cb_ref, lg_ref, lb_ref, o_ref, h_ref, hs_ref, *, tc):
    i = pl.program_id(1)

    def glu(u):
        u = u.astype(F32)
        return u[:, :CONV_CH] * jax.nn.sigmoid(u[:, CONV_CH:])

    h_ref[0:CONV_HALO, :] = jnp.where(i == 0, 0.0, glu(uprev_ref[0]))
    h_ref[CONV_HALO:CONV_HALO + tc, :] = glu(u_ref[0])
    for b in range(1, SUBLANES):
        hs_ref[b - 1] = h_ref[b:b + tc + CONV_HALO - SUBLANES, :]
    first = CONV_HALO - (CONV_WIDTH - 1)
    for c in range(tc // CONV_ROWS):
        acc = jnp.zeros((CONV_ROWS, CONV_CH), F32)
        for w in range(CONV_WIDTH):
            a, b = divmod(first + w, SUBLANES)
            r0 = c * CONV_ROWS + a * SUBLANES
            src = h_ref[r0:r0 + CONV_ROWS, :] if b == 0 else hs_ref[b - 1, r0:r0 + CONV_ROWS, :]
            acc = acc + src * cw_ref[w:w + 1, :]
        y = acc + cb_ref[...]
        mu = jnp.mean(y, axis=-1, keepdims=True)
        var = jnp.mean(jnp.square(y - mu), axis=-1, keepdims=True)
        y = (y - mu) * lax.rsqrt(var + EPS) * lg_ref[...] + lb_ref[...]
        o_ref[0, c * CONV_ROWS:(c + 1) * CONV_ROWS, :] = (y * jax.nn.sigmoid(y)).astype(o_ref.dtype)


def _conv(row, cw, cb, lg, lb, tc):
    B, S, _ = row.shape
    ucol = 0
    per = tc // CONV_HALO
    return pl.pallas_call(
        functools.partial(_conv_kernel, tc=tc),
        grid=(B, S // tc),
        in_specs=[
            pl.BlockSpec((1, tc, 2 * CONV_CH), lambda b, i: (b, i, ucol)),
            pl.BlockSpec((1, CONV_HALO, 2 * CONV_CH), lambda b, i: (b, jnp.maximum(i * per - 1, 0), ucol)),
            _const_spec((CONV_WIDTH, CONV_CH)),
            _const_spec((1, CONV_CH)),
            _const_spec((1, CONV_CH)),
            _const_spec((1, CONV_CH)),
        ],
        out_specs=pl.BlockSpec((1, tc, CONV_CH), lambda b, i: (b, i, 0)),
        out_shape=jax.ShapeDtypeStruct((B, S, CONV_CH), BF16),
        scratch_shapes=[pltpu.VMEM((CONV_HALO + tc, CONV_CH), F32),
                        pltpu.VMEM((SUBLANES - 1, CONV_HALO + tc - SUBLANES, CONV_CH), F32)],
        compiler_params=_params("parallel", "parallel"),
        name="conformer_conv",
    )(row, row, cw, cb, lg, lb)


def _merge_kernel(x_ref, g_ref, wg_ref, osb_ref, odf_ref, ocv_ref, wsb_ref, wdf_ref, wcv_ref, wout_ref, o_ref):
    x = x_ref[...]
    D = x.shape[1]
    xb = _rmsnorm(x, g_ref[...]).astype(BF16)
    y = None
    for n, (o_br, w_br) in enumerate(((osb_ref, wsb_ref), (odf_ref, wdf_ref), (ocv_ref, wcv_ref))):
        gate = jax.nn.sigmoid(_dot(xb, wg_ref[:, n * D:(n + 1) * D]))
        t = gate * _dot(o_br[...], w_br[...])
        y = t if y is None else y + t
    o_ref[...] = x + _dot(y.astype(BF16), wout_ref[...])


def _merge(x2, g, wg, osb, odf, ocv, wsb, wdf, wcv, wout, tm):
    N, D = x2.shape
    W = osb.shape[1]
    tok = lambda w: pl.BlockSpec((tm, w), lambda i: (i, 0))
    return pl.pallas_call(
        _merge_kernel,
        grid=(N // tm,),
        in_specs=[tok(D), _const_spec((1, D)), _const_spec((D, 3 * D)), tok(W), tok(W), tok(W),
                  _const_spec((W, D)), _const_spec((W, D)), _const_spec((W, D)), _const_spec((D, D))],
        out_specs=tok(D),
        out_shape=jax.ShapeDtypeStruct((N, D), F32),
        compiler_params=_params("parallel"),
        name="gated_merge",
    )(x2, g, wg, osb, odf, ocv, wsb, wdf, wcv, wout)


def _memkv_kernel(mem_ref, g_ref, wkv_ref, o_ref):
    mb = _rmsnorm(mem_ref[0], g_ref[...]).astype(BF16)
    o_ref[0] = _dot(mb, wkv_ref[...]).astype(o_ref.dtype)


def _memkv(mem, g, wkv):
    B, M, D = mem.shape
    return pl.pallas_call(
        _memkv_kernel,
        grid=(B,),
        in_specs=[pl.BlockSpec((1, M, D), lambda b: (b, 0, 0)), _const_spec((1, D)), _const_spec((D, 2 * XA_WIDTH))],
        out_specs=pl.BlockSpec((1, M, 2 * XA_WIDTH), lambda b: (b, 0, 0)),
        out_shape=jax.ShapeDtypeStruct((B, M, 2 * XA_WIDTH), BF16),
        compiler_params=_params("parallel"),
        name="mem_kv",
    )(mem, g, wkv)


def _xattn_kernel(x_ref, g_ref, wq_ref, kv_ref, wo_ref, o_ref):
    x = x_ref[0]
    xb = _rmsnorm(x, g_ref[...]).astype(BF16)
    q = _dot(xb, wq_ref[...]).astype(BF16)
    heads = []
    for h in range(XA_HEADS):
        hs = slice(h * XA_HEAD_DIM, (h + 1) * XA_HEAD_DIM)
        k = kv_ref[0, :, hs]
        v = kv_ref[0, :, XA_WIDTH + h * XA_HEAD_DIM:XA_WIDTH + (h + 1) * XA_HEAD_DIM]
        s = _dot_nt(q[:, hs], k) * (XA_HEAD_DIM ** -0.5)
        p = jnp.exp(s - jnp.max(s, axis=1, keepdims=True))
        p = p / jnp.sum(p, axis=1, keepdims=True)
        heads.append(_dot(p.astype(BF16), v).astype(BF16))
    o = jnp.concatenate(heads, axis=1)
    o_ref[0] = x + _dot(o, wo_ref[...])


def _xattn(x, g, wq, kv, wo, tm):
    B, S, D = x.shape
    M = kv.shape[1]
    return pl.pallas_call(
        _xattn_kernel,
        grid=(B, S // tm),
        in_specs=[
            pl.BlockSpec((1, tm, D), lambda b, i: (b, i, 0)),
            _const_spec((1, D)),
            _const_spec((D, XA_WIDTH)),
            pl.BlockSpec((1, M, 2 * XA_WIDTH), lambda b, i: (b, 0, 0)),
            _const_spec((XA_WIDTH, D)),
        ],
        out_specs=pl.BlockSpec((1, tm, D), lambda b, i: (b, i, 0)),
        out_shape=jax.ShapeDtypeStruct((B, S, D), F32),
        compiler_params=_params("parallel", "parallel"),
        name="cross_attention",
    )(x, g, wq, kv, wo)


def _mlp_kernel(x_ref, g_ref, wup_ref, wdown_ref, gf_ref, o_ref, *, fc, final):
    x = x_ref[...]
    xb = _rmsnorm(x, g_ref[...]).astype(BF16)
    acc = x
    for c in range(wup_ref.shape[1] // fc):
        cs = slice(c * fc, (c + 1) * fc)
        h = jnp.square(jnp.maximum(_dot(xb, wup_ref[:, cs]), 0.0)).astype(BF16)
        acc = acc + _dot(h, wdown_ref[cs, :])
    o_ref[...] = _rmsnorm(acc, gf_ref[...]) if final else acc


def _mlp(x2, g, wup, wdown, gf, tm, final):
    N, D = x2.shape
    F = wup.shape[1]
    return pl.pallas_call(
        functools.partial(_mlp_kernel, fc=1024, final=final),
        grid=(N // tm,),
        in_specs=[pl.BlockSpec((tm, D), lambda i: (i, 0)), _const_spec((1, D)),
                  _const_spec((D, F)), _const_spec((F, D)), _const_spec((1, D))],
        out_specs=pl.BlockSpec((tm, D), lambda i: (i, 0)),
        out_shape=jax.ShapeDtypeStruct((N, D), F32),
        compiler_params=_params("parallel"),
        name="sqrelu_mlp",
    )(x2, g, wup, wdown, gf)


def kernel(x, mem, rel_bias, ln_mix, w_in, diff_lambda, diff_subln, conv_w, conv_b, conv_ln_g, conv_ln_b,
           w_sb_proj, w_diff_proj, w_conv_proj, w_out, ln_xattn, ln_mem, xa_w_q, xa_w_kv, xa_w_o,
           ln_mlp, w_up, w_down, ln_final):
    B, S, D = x.shape
    depth = w_in.shape[0]
    blk = ATT_BLOCK
    tm = min(512, S)
    tc = min(256, S)
    assert S % SB_QBLOCK == 0 and S % DF_QBLOCK == 0 and S % tm == 0 and S % tc == 0 and D == 1024

    row2 = lambda a: a.reshape(1, -1).astype(F32)
    tri = jnp.asarray(np.triu(np.ones((blk, blk), np.float32), k=1), BF16)
    bias = _bias_tiles(rel_bias, blk)
    scale = HEAD_DIM ** -0.5

    for l in range(depth):
        w = w_in[l]
        sbq, sbk, sbv = w[:, 0:512] * scale, w[:, 512:1024], w[:, 1024:1536]
        dfq, dfk, dfv = w[:, 1536:2048] * scale, w[:, 2048:2560], w[:, 2560:3072]
        wrow = jnp.concatenate([w[:, 3072:4096], sbk, dfk], axis=1).astype(BF16)
        wtr = jnp.concatenate([sbq, sbv, dfq, dfv], axis=1).T.astype(BF16)
        wg = w[:, 4096:].astype(BF16)

        row, tr, va = _inproj(x, row2(ln_mix[l]), wrow, wtr, tm)
        kn = _key_norm_max(row)
        o_sb = _sb_attention(row, tr, tri, kn)
        lam_init = 0.8 - 0.6 * math.exp(-0.3 * l)
        o_df = _diff_attention(row, tr, va, bias, kn, diff_lambda[l].astype(F32), row2(diff_subln[l]), lam_init)
        o_cv = _conv(row, conv_w[l].reshape(CONV_WIDTH, CONV_CH).astype(F32), row2(conv_b[l]),
                     row2(conv_ln_g[l]), row2(conv_ln_b[l]), tc)

        N = B * S
        x2 = _merge(x.reshape(N, D), row2(ln_mix[l]), wg,
                    o_sb.reshape(N, -1), o_df.reshape(N, -1), o_cv.reshape(N, -1),
                    w_sb_proj[l].astype(BF16), w_diff_proj[l].astype(BF16), w_conv_proj[l].astype(BF16),
                    w_out[l].astype(BF16), tm)

        kv = _memkv(mem, row2(ln_mem[l]), xa_w_kv[l].astype(BF16))
        x3 = _xattn(x2.reshape(B, S, D), row2(ln_xattn[l]),
                    xa_w_q[l].astype(BF16), kv, xa_w_o[l].astype(BF16), tm)

        x = _mlp(x3.reshape(N, D), row2(ln_mlp[l]), w_up[l].astype(BF16), w_down[l].astype(BF16),
                 row2(ln_final), tm, final=(l == depth - 1)).reshape(B, S, D)
    return x
```

```python
import functools
import math

import numpy as np
import jax
import jax.numpy as jnp
from jax import lax
from jax.experimental import pallas as pl
from jax.experimental.pallas import tpu as pltpu

F32 = jnp.float32
BF16 = jnp.bfloat16

HEAD_DIM = 64
SB_WIDTH = 512
DIFF_WIDTH = 512
DIFF_HEADS = 4
CONV_CH = 512
CONV_WIDTH = 31
XA_HEADS = 4
XA_HEAD_DIM = 128
XA_WIDTH = XA_HEADS * XA_HEAD_DIM
NUM_BUCKETS = 32
MAX_EXACT = NUM_BUCKETS // 2
MAX_DISTANCE = 128
EPS = 1e-6
NEG_INF = -1e30

LANES = 128
SUBLANES = 8
ATT_BLOCK = 256
CONV_HALO = 32
CONV_ROWS = 64
VMEM_LIMIT = 56 * 1024 * 1024

SB_QBLOCK = 1024
DF_QBLOCK = 2048
SB_EXIT = 110.0
DF_UNROLL = 4
DF_SPREAD = 80.0
LOG2E = 1.4426950408889634
ONES_ROWS = 16

ROW_COLS = 2 * CONV_CH + 2 * 512
ROW_SBK, ROW_DFK = 8, 12
TR_ROWS = 3 * 512
TR_SBQ, TR_SBV, TR_DFQ = 0, 4, 8
VA_ROWS = LANES + ONES_ROWS


def _params(*sem):
    return pltpu.CompilerParams(dimension_semantics=sem, vmem_limit_bytes=VMEM_LIMIT)


def _const_spec(shape):
    nd = len(shape)
    return pl.BlockSpec(shape, lambda *_: (0,) * nd)


def _rmsnorm(x, g):
    return x * lax.rsqrt(jnp.mean(x * x, axis=-1, keepdims=True) + EPS) * g


def _dot(a, b):
    return jnp.dot(a, b, preferred_element_type=F32)


def _dot_nt(a, b):
    return lax.dot_general(a, b, (((1,), (1,)), ((), ())), preferred_element_type=F32)


def _inproj_kernel(x_ref, g_ref, wrow_ref, wtr_ref, row_ref, tr_ref, va_ref):
    xb = _rmsnorm(x_ref[0], g_ref[...]).astype(BF16)
    tm = xb.shape[0]
    for c in range(ROW_COLS // 512):
        cs = slice(c * 512, (c + 1) * 512)
        row_ref[0, :, cs] = _dot(xb, wrow_ref[:, cs]).astype(BF16)
    for c in range(TR_ROWS // 512):
        cs = slice(c * 512, (c + 1) * 512)
        tr_ref[0, cs, :] = _dot_nt(wtr_ref[cs, :], xb).astype(BF16)
    dfv = _dot_nt(wtr_ref[TR_ROWS:, :], xb).astype(BF16)
    for h in range(DIFF_HEADS):
        va_ref[0, h * VA_ROWS:h * VA_ROWS + LANES, :] = dfv[h * LANES:(h + 1) * LANES, :]
        va_ref[0, h * VA_ROWS + LANES:(h + 1) * VA_ROWS, :] = jnp.ones((ONES_ROWS, tm), BF16)


def _inproj(x, g, wrow, wtr, tm):
    B, S, D = x.shape
    return pl.pallas_call(
        _inproj_kernel,
        grid=(B, S // tm),
        in_specs=[
            pl.BlockSpec((1, tm, D), lambda b, i: (b, i, 0)),
            _const_spec((1, D)),
            _const_spec((D, ROW_COLS)),
            _const_spec((TR_ROWS + DIFF_WIDTH, D)),
        ],
        out_specs=[
            pl.BlockSpec((1, tm, ROW_COLS), lambda b, i: (b, i, 0)),
            pl.BlockSpec((1, TR_ROWS, tm), lambda b, i: (b, 0, i)),
            pl.BlockSpec((1, DIFF_HEADS * VA_ROWS, tm), lambda b, i: (b, 0, i)),
        ],
        out_shape=[
            jax.ShapeDtypeStruct((B, S, ROW_COLS), BF16),
            jax.ShapeDtypeStruct((B, TR_ROWS, S), BF16),
            jax.ShapeDtypeStruct((B, DIFF_HEADS * VA_ROWS, S), BF16),
        ],
        compiler_params=_params("parallel", "parallel"),
        name="inproj",
    )(x, g, wrow, wtr)


def _knorm_kernel(k_ref, sel_ref, o_ref):
    k = k_ref[0].astype(F32)
    n2 = _dot((k * k).astype(BF16), sel_ref[...])
    o_ref[0] = jnp.sqrt(jnp.max(n2, axis=0, keepdims=True))


def _key_norm_max(row):
    B, S, _ = row.shape
    width = SB_WIDTH + DIFF_WIDTH
    sel = np.zeros((width, LANES), np.float32)
    sel[np.arange(width), np.arange(width) // HEAD_DIM] = 1.0
    return pl.pallas_call(
        _knorm_kernel,
        grid=(B,),
        in_specs=[pl.BlockSpec((1, S, width), lambda b: (b, 0, ROW_SBK * LANES // width)),
                  _const_spec((width, LANES))],
        out_specs=pl.BlockSpec((1, 1, LANES), lambda b: (b, 0, 0)),
        out_shape=jax.ShapeDtypeStruct((B, 1, LANES), F32),
        compiler_params=_params("parallel"),
        name="key_norm_max",
    )(row, jnp.asarray(sel, BF16))


def _sb_kernel(qt_ref, k_ref, vt_ref, tri_ref, kn_ref, o_ref):
    kb, qb = ATT_BLOCK, SB_QBLOCK
    nq = qb // kb
    p, i = pl.program_id(1), pl.program_id(2)
    qt = qt_ref[0]
    sub = lax.broadcasted_iota(jnp.int32, qt.shape, 0)
    zero = jnp.zeros_like(qt)
    qtm = (jnp.where(sub < HEAD_DIM, qt, zero), jnp.where(sub >= HEAD_DIM, qt, zero))
    tri = tri_ref[...]

    lane = lax.broadcasted_iota(jnp.int32, (1, LANES), 1)
    kn = kn_ref[0]
    zbound = []
    for h in range(2):
        knh = jnp.max(jnp.where(lane == 2 * p + h, kn, 0.0), axis=1, keepdims=True)
        qf = qtm[h].astype(F32)
        zbound.append(jnp.sqrt(jnp.sum(qf * qf, axis=0, keepdims=True)) * (knh * 1.01))

    def splice(old, new, lo, hi):
        parts = ([old[:, :lo]] if lo else []) + [new] + ([old[:, hi:]] if hi < qb else [])
        return parts[0] if len(parts) == 1 else jnp.concatenate(parts, axis=1)

    def tiles(jobs, st):
        carry, acc = [st[0], st[2]], [st[1], st[3]]
        ks = [pl.ds(pl.multiple_of(j * kb, kb), kb) for j, _, _, _ in jobs]
        units = [(n, h) for n in range(len(jobs)) for h in range(2)]
        z = {(n, h): _dot(k_ref[0, ks[n], :], qtm[h][:, jobs[n][1]:jobs[n][2]]) for n, h in units}
        sp = {u: jnp.maximum(z[u], 0.0) + jnp.log(1.0 + jnp.exp2(jnp.abs(z[u]) * -LOG2E)) for u in units}
        own = {u: z[u] - sp[u] for u in units}
        keep = []
        for n, (_, lo, hi, valid) in enumerate(jobs):
            m = (lax.broadcasted_iota(jnp.int32, (kb, hi - lo), 0)
                 < lax.broadcasted_iota(jnp.int32, (kb, hi - lo), 1)) if valid is None else valid
            keep.append(m)
            for h in range(2):
                sp[n, h] = jnp.where(m, sp[n, h], 0.0)
        cs = {u: _dot(tri, sp[u].astype(BF16)) for u in units}
        oldest_row = {u: sp[u][0:1, :] for u in units}
        for n, (_, lo, hi, _) in enumerate(jobs):
            vt = vt_ref[0, :, ks[n]]
            c = [cs[n, h] + carry[h][:, lo:hi] for h in range(2)]
            a = [jnp.where(keep[n], jnp.exp(own[n, h] - c[h]), 0.0) for h in range(2)]
            pv = [_dot(vt[h * HEAD_DIM:(h + 1) * HEAD_DIM, :], a[h].astype(BF16)) for h in range(2)]
            for h in range(2):
                carry[h] = splice(carry[h], c[h][0:1, :] + oldest_row[n, h], lo, hi)
                acc[h] = splice(acc[h], acc[h][:, lo:hi] + pv[h], lo, hi)
        return carry[0], acc[0], carry[1], acc[1]

    zc = jnp.zeros((1, qb), F32)
    za = jnp.zeros((HEAD_DIM, qb), F32)
    newest = [(nq * i + c, c * kb, min(c + 2, nq) * kb, None) for c in reversed(range(nq))]

    st = lax.cond(i > 0, lambda: tiles(newest + [(nq * i - 1, 0, kb, True)], (zc, za, zc, za)),
                  lambda: tiles(newest, (zc, za, zc, za)))

    def strip_step(j, cols, s):
        ks = pl.ds(pl.multiple_of(j * kb, kb), kb)
        k, vt = k_ref[0, ks, :], vt_ref[0, :, ks]
        z = [_dot(k, qtm[h][:, cols]) for h in range(2)]
        sp = [jnp.maximum(z[h], 0.0) + jnp.log(1.0 + jnp.exp2(jnp.abs(z[h]) * -LOG2E)) for h in range(2)]
        c = [_dot(tri, sp[h].astype(BF16)) + s[2 * h] for h in range(2)]
        a = [jnp.exp((z[h] - sp[h]) - c[h]).astype(BF16) for h in range(2)]
        pv = [_dot(vt[h * HEAD_DIM:(h + 1) * HEAD_DIM, :], a[h]) for h in range(2)]
        return (c[0][0:1, :] + sp[0][0:1, :], s[1] + pv[0], c[1][0:1, :] + sp[1][0:1, :], s[3] + pv[1])

    out = []
    for n in range(nq):
        cols = slice(n * kb, (n + 1) * kb)
        zb = (zbound[0][:, cols], zbound[1][:, cols])

        def unfinished(s, zb=zb):
            return (jnp.min(jnp.minimum(s[0] - zb[0], s[2] - zb[1])) <= SB_EXIT).astype(jnp.int32)

        def body(s, n=n, cols=cols, unfinished=unfinished):
            new = strip_step(nq * i + n - s[0], cols, s[2:])
            return (s[0] + 1, unfinished(new)) + new

        s0 = tuple(x[:, cols] for x in st)
        out.append(lax.while_loop(lambda s, n=n: jnp.logical_and(s[0] <= nq * i + n, s[1] > 0), body,
                                  (jnp.int32(2), unfinished(s0)) + s0)[2:])
    acc = [jnp.concatenate([o[1 + 2 * h] for o in out], axis=1) for h in range(2)]
    o_ref[0] = jnp.concatenate(acc, axis=0).T.astype(o_ref.dtype)


def _sb_attention(row, tr, tri, kn):
    B, S, _ = row.shape
    qb = SB_QBLOCK
    return pl.pallas_call(
        _sb_kernel,
        grid=(B, SB_WIDTH // LANES, S // qb),
        in_specs=[
            pl.BlockSpec((1, LANES, qb), lambda b, p, i: (b, TR_SBQ + p, i)),
            pl.BlockSpec((1, S, LANES), lambda b, p, i: (b, 0, ROW_SBK + p)),
            pl.BlockSpec((1, LANES, S), lambda b, p, i: (b, TR_SBV + p, 0)),
            _const_spec((ATT_BLOCK, ATT_BLOCK)),
            pl.BlockSpec((1, 1, LANES), lambda b, p, i: (b, 0, 0)),
        ],
        out_specs=pl.BlockSpec((1, qb, LANES), lambda b, p, i: (b, i, p)),
        out_shape=jax.ShapeDtypeStruct((B, S, SB_WIDTH), BF16),
        compiler_params=_params("parallel", "parallel", "arbitrary"),
        name="sb_attention",
    )(tr, row, tr, tri, kn)


def _bucket_tiles(blk):
    r = np.arange(blk)[None, :] - np.arange(blk)[:, None]
    rel = np.stack([r, r + blk, r + 2 * blk]).astype(np.int64)
    n = np.maximum(rel, 0)
    nf = np.maximum(n, 1).astype(np.float32)
    large = MAX_EXACT + (np.log(nf / np.float32(MAX_EXACT)) / np.float32(math.log(MAX_DISTANCE / MAX_EXACT))
                         * np.float32(NUM_BUCKETS - MAX_EXACT)).astype(np.int32)
    large = np.minimum(large, NUM_BUCKETS - 1)
    idx = np.where(n < MAX_EXACT, n, large).astype(np.int32)
    assert (idx[2] == NUM_BUCKETS - 1).all() and blk >= MAX_DISTANCE
    return idx


def _bias_kernel(table_ref, idx_ref, o_ref):
    h = pl.program_id(0)
    idx = idx_ref[0]
    acc = jnp.zeros(idx.shape, F32)
    for b in range(NUM_BUCKETS):
        acc = jnp.where(idx == b, table_ref[b, h], acc)
    o_ref[0, 0] = acc


def _bias_tiles(rel_bias, blk):
    idx = jnp.asarray(_bucket_tiles(blk))
    return pl.pallas_call(
        _bias_kernel,
        grid=(DIFF_HEADS, 3),
        in_specs=[
            pl.BlockSpec(memory_space=pltpu.SMEM),
            pl.BlockSpec((1, blk, blk), lambda h, d: (d, 0, 0)),
        ],
        out_specs=pl.BlockSpec((1, 1, blk, blk), lambda h, d: (h, d, 0, 0)),
        out_shape=jax.ShapeDtypeStruct((DIFF_HEADS, 3, blk, blk), F32),
        name="bias_tiles",
    )(rel_bias.astype(F32), idx)


def _diff_kernel(qt_ref, k_ref, va_ref, bias_ref, kn_ref, lam_ref, subln_ref, o_ref, *, lam_init):
    kb, qb = ATT_BLOCK, DF_QBLOCK
    nq = qb // kb
    h, i = pl.program_id(1), pl.program_id(2)
    qt = qt_ref[0]
    sub = lax.broadcasted_iota(jnp.int32, qt.shape, 0)
    zero = jnp.zeros_like(qt)
    qtm = (jnp.where(sub < HEAD_DIM, qt, zero), jnp.where(sub >= HEAD_DIM, qt, zero))
    far = bias_ref[0, 2][0:1, 0:1]

    def scores(j, lo):
        k = k_ref[0, pl.ds(pl.multiple_of(j * kb, kb), kb), :]
        return [_dot(k, qtm[m][:, lo:]) for m in range(2)]

    def update(z, j, st, lo, bias, mask, online):
        vta = va_ref[0, :, pl.ds(pl.multiple_of(j * kb, kb), kb)]
        if bias is not None:
            z = [x + bias for x in z]
        if mask is not None:
            z = [jnp.where(mask, x, NEG_INF) for x in z]
        mnew = [st[m][0][:, lo:] for m in range(2)]
        scale = None
        if online:
            top = [jnp.max(x, axis=0, keepdims=True) for x in z]
            if bias is None:
                top = [t + far for t in top]
            mold, mnew = mnew, [jnp.maximum(mnew[m], top[m]) for m in range(2)]
            scale = [jnp.exp(mold[m] - mnew[m]) for m in range(2)]
        shift = [mn - far for mn in mnew] if bias is None else mnew
        p = [jnp.exp(z[m] - shift[m]) for m in range(2)]
        pv = [_dot(vta, p[m].astype(BF16)) for m in range(2)]
        out = []
        for m in range(2):
            l, acc = st[m][1][:, lo:], st[m][2][:, lo:]
            if online:
                l, acc = scale[m] * l, scale[m] * acc
            new = (mnew[m], l + pv[m][LANES:LANES + 1, :], acc + pv[m][:LANES, :])
            if lo:
                new = tuple(jnp.concatenate([old[:, :lo], x], axis=1) for old, x in zip(st[m], new))
            out.append(new)
        return tuple(out)

    def run(stab, online):
        zl, za = jnp.zeros((1, qb), F32), jnp.zeros((LANES, qb), F32)
        st = ((stab[0], zl, za), (stab[1], zl, za))
        for r in reversed(range(nq)):
            n = qb - r * kb
            bias = jnp.concatenate([bias_ref[0, min(c - r, 2)] for c in range(r, nq)], axis=1)
            causal = lax.broadcasted_iota(jnp.int32, (kb, n), 0) <= lax.broadcasted_iota(jnp.int32, (kb, n), 1)
            st = update(scores(nq * i + r, r * kb), nq * i + r, st, r * kb, bias, causal, online)
        def older(st):
            bias = jnp.concatenate([bias_ref[0, min(c + 1, 2)] for c in range(nq)], axis=1)
            st = update(scores(nq * i - 1, 0), nq * i - 1, st, 0, bias, None, online)
            for d in range(2, DF_UNROLL + 1):
                st = update(scores(nq * i - d, 0), nq * i - d, st, 0, None, None, online)

            def body(jj, s):
                for d in range(DF_UNROLL):
                    j = nq * i - DF_UNROLL * (jj + 1) - 1 - d
                    s = update(scores(j, 0), j, s, 0, None, None, online)
                return s

            return lax.fori_loop(0, (nq * i) // DF_UNROLL - 1, body, st)

        return lax.cond(i > 0, older, lambda s: s, st)

    lane = lax.broadcasted_iota(jnp.int32, (1, LANES), 1)
    kn = kn_ref[0]
    allb = bias_ref[0]
    bmax = jnp.max(jnp.max(allb, axis=0), axis=(0, 1), keepdims=True)
    bmin = jnp.min(jnp.min(allb, axis=0), axis=(0, 1), keepdims=True)
    qk = []
    for m in range(2):
        knm = jnp.max(jnp.where(lane == SB_WIDTH // HEAD_DIM + 2 * h + m, kn, 0.0), axis=1, keepdims=True)
        qf = qtm[m].astype(F32)
        qk.append(jnp.sqrt(jnp.sum(qf * qf, axis=0, keepdims=True)) * (knm * 1.01))
    spread = 2.0 * jnp.maximum(qk[0], qk[1]) + (bmax - bmin)
    bound = [x + bmax for x in qk]
    neg = jnp.full((1, qb), NEG_INF, F32)
    st = lax.cond(jnp.max(spread) <= DF_SPREAD,
                  lambda: run(bound, False), lambda: run([neg, neg], True))

    lp = lam_ref[...]
    lam = (jnp.exp(jnp.sum(lp[0:1] * lp[1:2], axis=1, keepdims=True))
           - jnp.exp(jnp.sum(lp[2:3] * lp[3:4], axis=1, keepdims=True)) + lam_init)
    (_, l0, acc0), (_, l1, acc1) = st
    o = (acc0 / l0 - lam * (acc1 / l1)).T
    o_ref[0] = (_rmsnorm(o, subln_ref[...]) * (1.0 - lam_init)).astype(o_ref.dtype)


def _diff_attention(row, tr, va, bias, kn, lam_p, subln, lam_init):
    B, S, _ = row.shape
    kb, qb = ATT_BLOCK, DF_QBLOCK
    return pl.pallas_call(
        functools.partial(_diff_kernel, lam_init=lam_init),
        grid=(B, DIFF_HEADS, S // qb),
        in_specs=[
            pl.BlockSpec((1, LANES, qb), lambda b, h, i: (b, TR_DFQ + h, i)),
            pl.BlockSpec((1, S, LANES), lambda b, h, i: (b, 0, ROW_DFK + h)),
            pl.BlockSpec((1, VA_ROWS, S), lambda b, h, i: (b, h, 0)),
            pl.BlockSpec((1, 3, kb, kb), lambda b, h, i: (h, 0, 0, 0)),
            pl.BlockSpec((1, 1, LANES), lambda b, h, i: (b, 0, 0)),
            _const_spec((4, HEAD_DIM)),
            _const_spec((1, 2 * HEAD_DIM)),
        ],
        out_specs=pl.BlockSpec((1, qb, LANES), lambda b, h, i: (b, i, h)),
        out_shape=jax.ShapeDtypeStruct((B, S, DIFF_WIDTH), BF16),
        compiler_params=_params("parallel", "parallel", "arbitrary"),
        name="diff_attention",
    )(tr, row, va, bias, kn, lam_p, subln)


def _conv_kernel(u_ref, uprev_ref, cw_ref, cb_ref, lg_ref, lb_ref, o_ref, h_ref, hs_ref, *, tc):
    i = pl.program_id(1)

    def glu(u):
        u = u.astype(F32)
        return u[:, :CONV_CH] * jax.nn.sigmoid(u[:, CONV_CH:])

    h_ref[0:CONV_HALO, :] = jnp.where(i == 0, 0.0, glu(uprev_ref[0]))
    h_ref[CONV_HALO:CONV_HALO + tc, :] = glu(u_ref[0])
    for b in range(1, SUBLANES):
        hs_ref[b - 1] = h_ref[b:b + tc + CONV_HALO - SUBLANES, :]
    first = CONV_HALO - (CONV_WIDTH - 1)
    for c in range(tc // CONV_ROWS):
        acc = jnp.zeros((CONV_ROWS, CONV_CH), F32)
        for w in range(CONV_WIDTH):
            a, b = divmod(first + w, SUBLANES)
            r0 = c * CONV_ROWS + a * SUBLANES
            src = h_ref[r0:r0 + CONV_ROWS, :] if b == 0 else hs_ref[b - 1, r0:r0 + CONV_ROWS, :]
            acc = acc + src * cw_ref[w:w + 1, :]
        y = acc + cb_ref[...]
        mu = jnp.mean(y, axis=-1, keepdims=True)
        var = jnp.mean(jnp.square(y - mu), axis=-1, keepdims=True)
        y = (y - mu) * lax.rsqrt(var + EPS) * lg_ref[...] + lb_ref[...]
        o_ref[0, c * CONV_ROWS:(c + 1) * CONV_ROWS, :] = (y * jax.nn.sigmoid(y)).astype(o_ref.dtype)


def _conv(row, cw, cb, lg, lb, tc):
    B, S, _ = row.shape
    ucol = 0
    per = tc // CONV_HALO
    return pl.pallas_call(
        functools.partial(_conv_kernel, tc=tc),
        grid=(B, S // tc),
        in_specs=[
            pl.BlockSpec((1, tc, 2 * CONV_CH), lambda b, i: (b, i, ucol)),
            pl.BlockSpec((1, CONV_HALO, 2 * CONV_CH), lambda b, i: (b, jnp.maximum(i * per - 1, 0), ucol)),
            _const_spec((CONV_WIDTH, CONV_CH)),
            _const_spec((1, CONV_CH)),
            _const_spec((1, CONV_CH)),
            _const_spec((1, CONV_CH)),
        ],
        out_specs=pl.BlockSpec((1, tc, CONV_CH), lambda b, i: (b, i, 0)),
        out_shape=jax.ShapeDtypeStruct((B, S, CONV_CH), BF16),
        scratch_shapes=[pltpu.VMEM((CONV_HALO + tc, CONV_CH), F32),
                        pltpu.VMEM((SUBLANES - 1, CONV_HALO + tc - SUBLANES, CONV_CH), F32)],
        compiler_params=_params("parallel", "parallel"),
        name="conformer_conv",
    )(row, row, cw, cb, lg, lb)


def _merge_kernel(x_ref, g_ref, wg_ref, osb_ref, odf_ref, ocv_ref, wsb_ref, wdf_ref, wcv_ref, wout_ref, o_ref):
    x = x_ref[...]
    D = x.shape[1]
    xb = _rmsnorm(x, g_ref[...]).astype(BF16)
    y = None
    for n, (o_br, w_br) in enumerate(((osb_ref, wsb_ref), (odf_ref, wdf_ref), (ocv_ref, wcv_ref))):
        gate = jax.nn.sigmoid(_dot(xb, wg_ref[:, n * D:(n + 1) * D]))
        t = gate * _dot(o_br[...], w_br[...])
        y = t if y is None else y + t
    o_ref[...] = x + _dot(y.astype(BF16), wout_ref[...])


def _merge(x2, g, wg, osb, odf, ocv, wsb, wdf, wcv, wout, tm):
    N, D = x2.shape
    W = osb.shape[1]
    tok = lambda w: pl.BlockSpec((tm, w), lambda i: (i, 0))
    return pl.pallas_call(
        _merge_kernel,
        grid=(N // tm,),
        in_specs=[tok(D), _const_spec((1, D)), _const_spec((D, 3 * D)), tok(W), tok(W), tok(W),
                  _const_spec((W, D)), _const_spec((W, D)), _const_spec((W, D)), _const_spec((D, D))],
        out_specs=tok(D),
        out_shape=jax.ShapeDtypeStruct((N, D), F32),
        compiler_params=_params("parallel"),
        name="gated_merge",
    )(x2, g, wg, osb, odf, ocv, wsb, wdf, wcv, wout)


def _memkv_kernel(mem_ref, g_ref, wkv_ref, o_ref):
    mb = _rmsnorm(mem_ref[0], g_ref[...]).astype(BF16)
    o_ref[0] = _dot(mb, wkv_ref[...]).astype(o_ref.dtype)


def _memkv(mem, g, wkv):
    B, M, D = mem.shape
    return pl.pallas_call(
        _memkv_kernel,
        grid=(B,),
        in_specs=[pl.BlockSpec((1, M, D), lambda b: (b, 0, 0)), _const_spec((1, D)), _const_spec((D, 2 * XA_WIDTH))],
        out_specs=pl.BlockSpec((1, M, 2 * XA_WIDTH), lambda b: (b, 0, 0)),
        out_shape=jax.ShapeDtypeStruct((B, M, 2 * XA_WIDTH), BF16),
        compiler_params=_params("parallel"),
        name="mem_kv",
    )(mem, g, wkv)


def _xattn_kernel(x_ref, g_ref, wq_ref, kv_ref, wo_ref, o_ref):
    x = x_ref[0]
    xb = _rmsnorm(x, g_ref[...]).astype(BF16)
    q = _dot(xb, wq_ref[...]).astype(BF16)
    heads = []
    for h in range(XA_HEADS):
        hs = slice(h * XA_HEAD_DIM, (h + 1) * XA_HEAD_DIM)
        k = kv_ref[0, :, hs]
        v = kv_ref[0, :, XA_WIDTH + h * XA_HEAD_DIM:XA_WIDTH + (h + 1) * XA_HEAD_DIM]
        s = _dot_nt(q[:, hs], k) * (XA_HEAD_DIM ** -0.5)
        p = jnp.exp(s - jnp.max(s, axis=1, keepdims=True))
        p = p / jnp.sum(p, axis=1, keepdims=True)
        heads.append(_dot(p.astype(BF16), v).astype(BF16))
    o = jnp.concatenate(heads, axis=1)
    o_ref[0] = x + _dot(o, wo_ref[...])


def _xattn(x, g, wq, kv, wo, tm):
    B, S, D = x.shape
    M = kv.shape[1]
    return pl.pallas_call(
        _xattn_kernel,
        grid=(B, S // tm),
        in_specs=[
            pl.BlockSpec((1, tm, D), lambda b, i: (b, i, 0)),
            _const_spec((1, D)),
            _const_spec((D, XA_WIDTH)),
            pl.BlockSpec((1, M, 2 * XA_WIDTH), lambda b, i: (b, 0, 0)),
            _const_spec((XA_WIDTH, D)),
        ],
        out_specs=pl.BlockSpec((1, tm, D), lambda b, i: (b, i, 0)),
        out_shape=jax.ShapeDtypeStruct((B, S, D), F32),
        compiler_params=_params("parallel", "parallel"),
        name="cross_attention",
    )(x, g, wq, kv, wo)


def _mlp_kernel(x_ref, g_ref, wup_ref, wdown_ref, gf_ref, o_ref, *, fc, final):
    x = x_ref[...]
    xb = _rmsnorm(x, g_ref[...]).astype(BF16)
    acc = x
    for c in range(wup_ref.shape[1] // fc):
        cs = slice(c * fc, (c + 1) * fc)
        h = jnp.square(jnp.maximum(_dot(xb, wup_ref[:, cs]), 0.0)).astype(BF16)
        acc = acc + _dot(h, wdown_ref[cs, :])
    o_ref[...] = _rmsnorm(acc, gf_ref[...]) if final else acc


def _mlp(x2, g, wup, wdown, gf, tm, final):
    N, D = x2.shape
    F = wup.shape[1]
    return pl.pallas_call(
        functools.partial(_mlp_kernel, fc=1024, final=final),
        grid=(N // tm,),
        in_specs=[pl.BlockSpec((tm, D), lambda i: (i, 0)), _const_spec((1, D)),
                  _const_spec((D, F)), _const_spec((F, D)), _const_spec((1, D))],
        out_specs=pl.BlockSpec((tm, D), lambda i: (i, 0)),
        out_shape=jax.ShapeDtypeStruct((N, D), F32),
        compiler_params=_params("parallel"),
        name="sqrelu_mlp",
    )(x2, g, wup, wdown, gf)


def kernel(x, mem, rel_bias, ln_mix, w_in, diff_lambda, diff_subln, conv_w, conv_b, conv_ln_g, conv_ln_b,
           w_sb_proj, w_diff_proj, w_conv_proj, w_out, ln_xattn, ln_mem, xa_w_q, xa_w_kv, xa_w_o,
           ln_mlp, w_up, w_down, ln_final):
    B, S, D = x.shape
    depth = w_in.shape[0]
    blk = ATT_BLOCK
    tm = min(512, S)
    tc = min(256, S)
    assert S % SB_QBLOCK == 0 and S % DF_QBLOCK == 0 and S % tm == 0 and S % tc == 0 and D == 1024

    row2 = lambda a: a.reshape(1, -1).astype(F32)
    tri = jnp.asarray(np.triu(np.ones((blk, blk), np.float32), k=1), BF16)
    bias = _bias_tiles(rel_bias, blk)
    scale = HEAD_DIM ** -0.5

    for l in range(depth):
        w = w_in[l]
        sbq, sbk, sbv = w[:, 0:512] * scale, w[:, 512:1024], w[:, 1024:1536]
        dfq, dfk, dfv = w[:, 1536:2048] * scale, w[:, 2048:2560], w[:, 2560:3072]
        wrow = jnp.concatenate([w[:, 3072:4096], sbk, dfk], axis=1).astype(BF16)
        wtr = jnp.concatenate([sbq, sbv, dfq, dfv], axis=1).T.astype(BF16)
        wg = w[:, 4096:].astype(BF16)

        row, tr, va = _inproj(x, row2(ln_mix[l]), wrow, wtr, tm)
        kn = _key_norm_max(row)
        o_sb = _sb_attention(row, tr, tri, kn)
        lam_init = 0.8 - 0.6 * math.exp(-0.3 * l)
        o_df = _diff_attention(row, tr, va, bias, kn, diff_lambda[l].astype(F32), row2(diff_subln[l]), lam_init)
        o_cv = _conv(row, conv_w[l].reshape(CONV_WIDTH, CONV_CH).astype(F32), row2(conv_b[l]),
                     row2(conv_ln_g[l]), row2(conv_ln_b[l]), tc)

        N = B * S
        x2 = _merge(x.reshape(N, D), row2(ln_mix[l]), wg,
                    o_sb.reshape(N, -1), o_df.reshape(N, -1), o_cv.reshape(N, -1),
                    w_sb_proj[l].astype(BF16), w_diff_proj[l].astype(BF16), w_conv_proj[l].astype(BF16),
                    w_out[l].astype(BF16), tm)

        kv = _memkv(mem, row2(ln_mem[l]), xa_w_kv[l].astype(BF16))
        x3 = _xattn(x2.reshape(B, S, D), row2(ln_xattn[l]),
                    xa_w_q[l].astype(BF16), kv, xa_w_o[l].astype(BF16), tm)

        x = _mlp(x3.reshape(N, D), row2(ln_mlp[l]), w_up[l].astype(BF16), w_down[l].astype(BF16),
                 row2(ln_final), tm, final=(l == depth - 1)).reshape(B, S, D)
    return x
```

```python
import functools
import math

import numpy as np
import jax
import jax.numpy as jnp
from jax import lax
from jax.experimental import pallas as pl
from jax.experimental.pallas import tpu as pltpu

F32 = jnp.float32
BF16 = jnp.bfloat16

HEAD_DIM = 64
SB_WIDTH = 512
DIFF_WIDTH = 512
DIFF_HEADS = 4
CONV_CH = 512
CONV_WIDTH = 31
XA_HEADS = 4
XA_HEAD_DIM = 128
XA_WIDTH = XA_HEADS * XA_HEAD_DIM
NUM_BUCKETS = 32
MAX_EXACT = NUM_BUCKETS // 2
MAX_DISTANCE = 128
EPS = 1e-6
NEG_INF = -1e30

LANES = 128
SUBLANES = 8
ATT_BLOCK = 256
CONV_HALO = 32
CONV_ROWS = 64
VMEM_LIMIT = 56 * 1024 * 1024

SB_QBLOCK = 2048
DF_QBLOCK = 2048
SB_EXIT = 110.0
DF_UNROLL = 4
DF_SPREAD = 80.0
LOG2E = 1.4426950408889634
ONES_ROWS = 16

ROW_COLS = 2 * CONV_CH + 2 * 512
ROW_SBK, ROW_DFK = 8, 12
TR_ROWS = 3 * 512
TR_SBQ, TR_SBV, TR_DFQ = 0, 4, 8
VA_ROWS = LANES + ONES_ROWS


def _params(*sem):
    return pltpu.CompilerParams(dimension_semantics=sem, vmem_limit_bytes=VMEM_LIMIT)


def _const_spec(shape):
    nd = len(shape)
    return pl.BlockSpec(shape, lambda *_: (0,) * nd)


def _rmsnorm(x, g):
    return x * lax.rsqrt(jnp.mean(x * x, axis=-1, keepdims=True) + EPS) * g


def _dot(a, b):
    return jnp.dot(a, b, preferred_element_type=F32)


def _dot_nt(a, b):
    return lax.dot_general(a, b, (((1,), (1,)), ((), ())), preferred_element_type=F32)


def _inproj_kernel(x_ref, g_ref, wrow_ref, wtr_ref, sel_ref, row_ref, tr_ref, va_ref, kn_ref):
    xb = _rmsnorm(x_ref[0], g_ref[...]).astype(BF16)
    tm = xb.shape[0]
    n2 = None
    for c in range(ROW_COLS // 512):
        cs = slice(c * 512, (c + 1) * 512)
        out = _dot(xb, wrow_ref[:, cs]).astype(BF16)
        row_ref[0, :, cs] = out
        if c * 512 >= ROW_SBK * LANES:
            kf = out.astype(F32)
            ks = slice(c * 512 - ROW_SBK * LANES, (c + 1) * 512 - ROW_SBK * LANES)
            part = _dot((kf * kf).astype(BF16), sel_ref[ks, :])
            n2 = part if n2 is None else n2 + part
    kn_ref[0, 0] = jnp.sqrt(jnp.max(n2, axis=0, keepdims=True))
    for c in range(TR_ROWS // 512):
        cs = slice(c * 512, (c + 1) * 512)
        tr_ref[0, cs, :] = _dot_nt(wtr_ref[cs, :], xb).astype(BF16)
    dfv = _dot_nt(wtr_ref[TR_ROWS:, :], xb).astype(BF16)
    for h in range(DIFF_HEADS):
        va_ref[0, h * VA_ROWS:h * VA_ROWS + LANES, :] = dfv[h * LANES:(h + 1) * LANES, :]
        va_ref[0, h * VA_ROWS + LANES:(h + 1) * VA_ROWS, :] = jnp.ones((ONES_ROWS, tm), BF16)


def _inproj(x, g, wrow, wtr, tm):
    B, S, D = x.shape
    width = SB_WIDTH + DIFF_WIDTH
    sel = np.zeros((width, LANES), np.float32)
    sel[np.arange(width), np.arange(width) // HEAD_DIM] = 1.0
    return pl.pallas_call(
        _inproj_kernel,
        grid=(B, S // tm),
        in_specs=[
            pl.BlockSpec((1, tm, D), lambda b, i: (b, i, 0)),
            _const_spec((1, D)),
            _const_spec((D, ROW_COLS)),
            _const_spec((TR_ROWS + DIFF_WIDTH, D)),
            _const_spec((width, LANES)),
        ],
        out_specs=[
            pl.BlockSpec((1, tm, ROW_COLS), lambda b, i: (b, i, 0)),
            pl.BlockSpec((1, TR_ROWS, tm), lambda b, i: (b, 0, i)),
            pl.BlockSpec((1, DIFF_HEADS * VA_ROWS, tm), lambda b, i: (b, 0, i)),
            pl.BlockSpec((1, 1, 1, LANES), lambda b, i: (b, i, 0, 0)),
        ],
        out_shape=[
            jax.ShapeDtypeStruct((B, S, ROW_COLS), BF16),
            jax.ShapeDtypeStruct((B, TR_ROWS, S), BF16),
            jax.ShapeDtypeStruct((B, DIFF_HEADS * VA_ROWS, S), BF16),
            jax.ShapeDtypeStruct((B, S // tm, 1, LANES), F32),
        ],
        compiler_params=_params("parallel", "parallel"),
        name="inproj",
    )(x, g, wrow, wtr, jnp.asarray(sel, BF16))


def _sb_kernel(qt_ref, k_ref, vt_ref, tri_ref, kn_ref, o_ref):
    kb, qb = ATT_BLOCK, SB_QBLOCK
    nq = qb // kb
    p, i = pl.program_id(1), pl.program_id(2)
    qt = qt_ref[0]
    sub = lax.broadcasted_iota(jnp.int32, qt.shape, 0)
    zero = jnp.zeros_like(qt)
    qtm = (jnp.where(sub < HEAD_DIM, qt, zero), jnp.where(sub >= HEAD_DIM, qt, zero))
    tri = tri_ref[...]

    lane = lax.broadcasted_iota(jnp.int32, (1, LANES), 1)
    kn = jnp.max(kn_ref[0], axis=0)
    zbound = []
    for h in range(2):
        knh = jnp.max(jnp.where(lane == 2 * p + h, kn, 0.0), axis=1, keepdims=True)
        qf = qtm[h].astype(F32)
        zbound.append(jnp.sqrt(jnp.sum(qf * qf, axis=0, keepdims=True)) * (knh * 1.01))

    def splice(old, new, lo, hi):
        parts = ([old[:, :lo]] if lo else []) + [new] + ([old[:, hi:]] if hi < qb else [])
        return parts[0] if len(parts) == 1 else jnp.concatenate(parts, axis=1)

    def tiles(jobs, st):
        carry, acc = [st[0], st[2]], [st[1], st[3]]
        ks = [pl.ds(pl.multiple_of(j * kb, kb), kb) for j, _, _, _ in jobs]
        units = [(n, h) for n in range(len(jobs)) for h in range(2)]
        z = {(n, h): _dot(k_ref[0, ks[n], :], qtm[h][:, jobs[n][1]:jobs[n][2]]) for n, h in units}
        sp = {u: jnp.maximum(z[u], 0.0) + jnp.log(1.0 + jnp.exp2(jnp.abs(z[u]) * -LOG2E)) for u in units}
        own = {u: z[u] - sp[u] for u in units}
        keep = []
        for n, (_, lo, hi, valid) in enumerate(jobs):
            m = (lax.broadcasted_iota(jnp.int32, (kb, hi - lo), 0)
                 < lax.broadcasted_iota(jnp.int32, (kb, hi - lo), 1)) if valid is None else valid
            keep.append(m)
            for h in range(2):
                sp[n, h] = jnp.where(m, sp[n, h], 0.0)
        cs = {u: _dot(tri, sp[u].astype(BF16)) for u in units}
        oldest_row = {u: sp[u][0:1, :] for u in units}
        for n, (_, lo, hi, _) in enumerate(jobs):
            vt = vt_ref[0, :, ks[n]]
            c = [cs[n, h] + carry[h][:, lo:hi] for h in range(2)]
            a = [jnp.where(keep[n], jnp.exp(own[n, h] - c[h]), 0.0) for h in range(2)]
            pv = [_dot(vt[h * HEAD_DIM:(h + 1) * HEAD_DIM, :], a[h].astype(BF16)) for h in range(2)]
            for h in range(2):
                carry[h] = splice(carry[h], c[h][0:1, :] + oldest_row[n, h], lo, hi)
                acc[h] = splice(acc[h], acc[h][:, lo:hi] + pv[h], lo, hi)
        return carry[0], acc[0], carry[1], acc[1]

    zc = jnp.zeros((1, qb), F32)
    za = jnp.zeros((HEAD_DIM, qb), F32)
    newest = [(nq * i + c, c * kb, min(c + 2, nq) * kb, None) for c in reversed(range(nq))]

    st = lax.cond(i > 0, lambda: tiles(newest + [(nq * i - 1, 0, kb, True)], (zc, za, zc, za)),
                  lambda: tiles(newest, (zc, za, zc, za)))

    def strip_step(j, cols, s):
        ks = pl.ds(pl.multiple_of(j * kb, kb), kb)
        k, vt = k_ref[0, ks, :], vt_ref[0, :, ks]
        z = [_dot(k, qtm[h][:, cols]) for h in range(2)]
        sp = [jnp.maximum(z[h], 0.0) + jnp.log(1.0 + jnp.exp2(jnp.abs(z[h]) * -LOG2E)) for h in range(2)]
        c = [_dot(tri, sp[h].astype(BF16)) + s[2 * h] for h in range(2)]
        a = [jnp.exp((z[h] - sp[h]) - c[h]).astype(BF16) for h in range(2)]
        pv = [_dot(vt[h * HEAD_DIM:(h + 1) * HEAD_DIM, :], a[h]) for h in range(2)]
        return (c[0][0:1, :] + sp[0][0:1, :], s[1] + pv[0], c[1][0:1, :] + sp[1][0:1, :], s[3] + pv[1])

    out = []
    for n in range(nq):
        cols = slice(n * kb, (n + 1) * kb)
        zb = (zbound[0][:, cols], zbound[1][:, cols])

        def unfinished(s, zb=zb):
            return (jnp.min(jnp.minimum(s[0] - zb[0], s[2] - zb[1])) <= SB_EXIT).astype(jnp.int32)

        def body(s, n=n, cols=cols, unfinished=unfinished):
            new = strip_step(nq * i + n - s[0], cols, s[2:])
            return (s[0] + 1, unfinished(new)) + new

        s0 = tuple(x[:, cols] for x in st)
        out.append(lax.while_loop(lambda s, n=n: jnp.logical_and(s[0] <= nq * i + n, s[1] > 0), body,
                                  (jnp.int32(2), unfinished(s0)) + s0)[2:])
    acc = [jnp.concatenate([o[1 + 2 * h] for o in out], axis=1) for h in range(2)]
    o_ref[0] = jnp.concatenate(acc, axis=0).T.astype(o_ref.dtype)


def _sb_attention(row, tr, tri, kn):
    B, S, _ = row.shape
    qb = SB_QBLOCK
    return pl.pallas_call(
        _sb_kernel,
        grid=(B, SB_WIDTH // LANES, S // qb),
        in_specs=[
            pl.BlockSpec((1, LANES, qb), lambda b, p, i: (b, TR_SBQ + p, i)),
            pl.BlockSpec((1, S, LANES), lambda b, p, i: (b, 0, ROW_SBK + p)),
            pl.BlockSpec((1, LANES, S), lambda b, p, i: (b, TR_SBV + p, 0)),
            _const_spec((ATT_BLOCK, ATT_BLOCK)),
            pl.BlockSpec((1,) + kn.shape[1:], lambda b, p, i: (b, 0, 0, 0)),
        ],
        out_specs=pl.BlockSpec((1, qb, LANES), lambda b, p, i: (b, i, p)),
        out_shape=jax.ShapeDtypeStruct((B, S, SB_WIDTH), BF16),
        compiler_params=_params("parallel", "parallel", "arbitrary"),
        name="sb_attention",
    )(tr, row, tr, tri, kn)


def _bucket_tiles(blk):
    r = np.arange(blk)[None, :] - np.arange(blk)[:, None]
    rel = np.stack([r, r + blk, r + 2 * blk]).astype(np.int64)
    n = np.maximum(rel, 0)
    nf = np.maximum(n, 1).astype(np.float32)
    large = MAX_EXACT + (np.log(nf / np.float32(MAX_EXACT)) / np.float32(math.log(MAX_DISTANCE / MAX_EXACT))
                         * np.float32(NUM_BUCKETS - MAX_EXACT)).astype(np.int32)
    large = np.minimum(large, NUM_BUCKETS - 1)
    idx = np.where(n < MAX_EXACT, n, large).astype(np.int32)
    assert (idx[2] == NUM_BUCKETS - 1).all() and blk >= MAX_DISTANCE
    return idx


def _bias_kernel(table_ref, idx_ref, o_ref):
    h = pl.program_id(0)
    idx = idx_ref[0]
    acc = jnp.zeros(idx.shape, F32)
    for b in range(NUM_BUCKETS):
        acc = jnp.where(idx == b, table_ref[b, h], acc)
    o_ref[0, 0] = acc


def _bias_tiles(rel_bias, blk):
    idx = jnp.asarray(_bucket_tiles(blk))
    return pl.pallas_call(
        _bias_kernel,
        grid=(DIFF_HEADS, 3),
        in_specs=[
            pl.BlockSpec(memory_space=pltpu.SMEM),
            pl.BlockSpec((1, blk, blk), lambda h, d: (d, 0, 0)),
        ],
        out_specs=pl.BlockSpec((1, 1, blk, blk), lambda h, d: (h, d, 0, 0)),
        out_shape=jax.ShapeDtypeStruct((DIFF_HEADS, 3, blk, blk), F32),
        name="bias_tiles",
    )(rel_bias.astype(F32), idx)


def _diff_kernel(qt_ref, k_ref, va_ref, bias_ref, kn_ref, lam_ref, subln_ref, o_ref, *, lam_init):
    kb, qb = ATT_BLOCK, DF_QBLOCK
    nq = qb // kb
    h, i = pl.program_id(1), pl.program_id(2)
    qt = qt_ref[0]
    sub = lax.broadcasted_iota(jnp.int32, qt.shape, 0)
    zero = jnp.zeros_like(qt)
    qtm = (jnp.where(sub < HEAD_DIM, qt, zero), jnp.where(sub >= HEAD_DIM, qt, zero))
    far = bias_ref[0, 2][0:1, 0:1]

    def scores(j, lo):
        k = k_ref[0, pl.ds(pl.multiple_of(j * kb, kb), kb), :]
        return [_dot(k, qtm[m][:, lo:]) for m in range(2)]

    def update(z, j, st, lo, bias, mask, online):
        vta = va_ref[0, :, pl.ds(pl.multiple_of(j * kb, kb), kb)]
        if bias is not None:
            z = [x + bias for x in z]
        if mask is not None:
            z = [jnp.where(mask, x, NEG_INF) for x in z]
        mnew = [st[m][0][:, lo:] for m in range(2)]
        scale = None
        if online:
            top = [jnp.max(x, axis=0, keepdims=True) for x in z]
            if bias is None:
                top = [t + far for t in top]
            mold, mnew = mnew, [jnp.maximum(mnew[m], top[m]) for m in range(2)]
            scale = [jnp.exp(mold[m] - mnew[m]) for m in range(2)]
        shift = [mn - far for mn in mnew] if bias is None else mnew
        p = [jnp.exp(z[m] - shift[m]) for m in range(2)]
        pv = [_dot(vta, p[m].astype(BF16)) for m in range(2)]
        out = []
        for m in range(2):
            l, acc = st[m][1][:, lo:], st[m][2][:, lo:]
            if online:
                l, acc = scale[m] * l, scale[m] * acc
            new = (mnew[m], l + pv[m][LANES:LANES + 1, :], acc + pv[m][:LANES, :])
            if lo:
                new = tuple(jnp.concatenate([old[:, :lo], x], axis=1) for old, x in zip(st[m], new))
            out.append(new)
        return tuple(out)

    def run(stab, online):
        zl, za = jnp.zeros((1, qb), F32), jnp.zeros((LANES, qb), F32)
        st = ((stab[0], zl, za), (stab[1], zl, za))
        for r in reversed(range(nq)):
            n = qb - r * kb
            bias = jnp.concatenate([bias_ref[0, min(c - r, 2)] for c in range(r, nq)], axis=1)
            causal = lax.broadcasted_iota(jnp.int32, (kb, n), 0) <= lax.broadcasted_iota(jnp.int32, (kb, n), 1)
            st = update(scores(nq * i + r, r * kb), nq * i + r, st, r * kb, bias, causal, online)
        def older(st):
            bias = jnp.concatenate([bias_ref[0, min(c + 1, 2)] for c in range(nq)], axis=1)
            st = update(scores(nq * i - 1, 0), nq * i - 1, st, 0, bias, None, online)
            for d in range(2, DF_UNROLL + 1):
                st = update(scores(nq * i - d, 0), nq * i - d, st, 0, None, None, online)

            def body(jj, s):
                for d in range(DF_UNROLL):
                    j = nq * i - DF_UNROLL * (jj + 1) - 1 - d
                    s = update(scores(j, 0), j, s, 0, None, None, online)
                return s

            return lax.fori_loop(0, (nq * i) // DF_UNROLL - 1, body, st)

        return lax.cond(i > 0, older, lambda s: s, st)

    lane = lax.broadcasted_iota(jnp.int32, (1, LANES), 1)
    kn = jnp.max(kn_ref[0], axis=0)
    allb = bias_ref[0]
    bmax = jnp.max(jnp.max(allb, axis=0), axis=(0, 1), keepdims=True)
    bmin = jnp.min(jnp.min(allb, axis=0), axis=(0, 1), keepdims=True)
    qk = []
    for m in range(2):
        knm = jnp.max(jnp.where(lane == SB_WIDTH // HEAD_DIM + 2 * h + m, kn, 0.0), axis=1, keepdims=True)
        qf = qtm[m].astype(F32)
        qk.append(jnp.sqrt(jnp.sum(qf * qf, axis=0, keepdims=True)) * (knm * 1.01))
    spread = 2.0 * jnp.maximum(qk[0], qk[1]) + (bmax - bmin)
    bound = [x + bmax for x in qk]
    neg = jnp.full((1, qb), NEG_INF, F32)
    st = lax.cond(jnp.max(spread) <= DF_SPREAD,
                  lambda: run(bound, False), lambda: run([neg, neg], True))

    lp = lam_ref[...]
    lam = (jnp.exp(jnp.sum(lp[0:1] * lp[1:2], axis=1, keepdims=True))
           - jnp.exp(jnp.sum(lp[2:3] * lp[3:4], axis=1, keepdims=True)) + lam_init)
    (_, l0, acc0), (_, l1, acc1) = st
    o = (acc0 / l0 - lam * (acc1 / l1)).T
    o_ref[0] = (_rmsnorm(o, subln_ref[...]) * (1.0 - lam_init)).astype(o_ref.dtype)


def _diff_attention(row, tr, va, bias, kn, lam_p, subln, lam_init):
    B, S, _ = row.shape
    kb, qb = ATT_BLOCK, DF_QBLOCK
    return pl.pallas_call(
        functools.partial(_diff_kernel, lam_init=lam_init),
        grid=(B, DIFF_HEADS, S // qb),
        in_specs=[
            pl.BlockSpec((1, LANES, qb), lambda b, h, i: (b, TR_DFQ + h, i)),
            pl.BlockSpec((1, S, LANES), lambda b, h, i: (b, 0, ROW_DFK + h)),
            pl.BlockSpec((1, VA_ROWS, S), lambda b, h, i: (b, h, 0)),
            pl.BlockSpec((1, 3, kb, kb), lambda b, h, i: (h, 0, 0, 0)),
            pl.BlockSpec((1,) + kn.shape[1:], lambda b, h, i: (b, 0, 0, 0)),
            _const_spec((4, HEAD_DIM)),
            _const_spec((1, 2 * HEAD_DIM)),
        ],
        out_specs=pl.BlockSpec((1, qb, LANES), lambda b, h, i: (b, i, h)),
        out_shape=jax.ShapeDtypeStruct((B, S, DIFF_WIDTH), BF16),
        compiler_params=_params("parallel", "parallel", "arbitrary"),
        name="diff_attention",
    )(tr, row, va, bias, kn, lam_p, subln)


def _conv_kernel(u_ref, uprev_ref, cw_ref, cb_ref, lg_ref, lb_ref, o_ref, h_ref, hs_ref, *, tc):
    i = pl.program_id(1)

    def glu(u):
        u = u.astype(F32)
        return u[:, :CONV_CH] * jax.nn.sigmoid(u[:, CONV_CH:])

    h_ref[0:CONV_HALO, :] = jnp.where(i == 0, 0.0, glu(uprev_ref[0]))
    h_ref[CONV_HALO:CONV_HALO + tc, :] = glu(u_ref[0])
    for b in range(1, SUBLANES):
        hs_ref[b - 1] = h_ref[b:b + tc + CONV_HALO - SUBLANES, :]
    first = CONV_HALO - (CONV_WIDTH - 1)
    for c in range(tc // CONV_ROWS):
        acc = jnp.zeros((CONV_ROWS, CONV_CH), F32)
        for w in range(CONV_WIDTH):
            a, b = divmod(first + w, SUBLANES)
            r0 = c * CONV_ROWS + a * SUBLANES
            src = h_ref[r0:r0 + CONV_ROWS, :] if b == 0 else hs_ref[b - 1, r0:r0 + CONV_ROWS, :]
            acc = acc + src * cw_ref[w:w + 1, :]
        y = acc + cb_ref[...]
        mu = jnp.mean(y, axis=-1, keepdims=True)
        var = jnp.mean(jnp.square(y - mu), axis=-1, keepdims=True)
        y = (y - mu) * lax.rsqrt(var + EPS) * lg_ref[...] + lb_ref[...]
        o_ref[0, c * CONV_ROWS:(c + 1) * CONV_ROWS, :] = (y * jax.nn.sigmoid(y)).astype(o_ref.dtype)


def _conv(row, cw, cb, lg, lb, tc):
    B, S, _ = row.shape
    ucol = 0
    per = tc // CONV_HALO
    return pl.pallas_call(
        functools.partial(_conv_kernel, tc=tc),
        grid=(B, S // tc),
        in_specs=[
            pl.BlockSpec((1, tc, 2 * CONV_CH), lambda b, i: (b, i, ucol)),
            pl.BlockSpec((1, CONV_HALO, 2 * CONV_CH), lambda b, i: (b, jnp.maximum(i * per - 1, 0), ucol)),
            _const_spec((CONV_WIDTH, CONV_CH)),
            _const_spec((1, CONV_CH)),
            _const_spec((1, CONV_CH)),
            _const_spec((1, CONV_CH)),
        ],
        out_specs=pl.BlockSpec((1, tc, CONV_CH), lambda b, i: (b, i, 0)),
        out_shape=jax.ShapeDtypeStruct((B, S, CONV_CH), BF16),
        scratch_shapes=[pltpu.VMEM((CONV_HALO + tc, CONV_CH), F32),
                        pltpu.VMEM((SUBLANES - 1, CONV_HALO + tc - SUBLANES, CONV_CH), F32)],
        compiler_params=_params("parallel", "parallel"),
        name="conformer_conv",
    )(row, row, cw, cb, lg, lb)


def _merge_kernel(x_ref, g_ref, wg_ref, osb_ref, odf_ref, ocv_ref, wsb_ref, wdf_ref, wcv_ref, wout_ref, o_ref):
    x = x_ref[...]
    D = x.shape[1]
    xb = _rmsnorm(x, g_ref[...]).astype(BF16)
    y = None
    for n, (o_br, w_br) in enumerate(((osb_ref, wsb_ref), (odf_ref, wdf_ref), (ocv_ref, wcv_ref))):
        gate = jax.nn.sigmoid(_dot(xb, wg_ref[:, n * D:(n + 1) * D]))
        t = gate * _dot(o_br[...], w_br[...])
        y = t if y is None else y + t
    o_ref[...] = x + _dot(y.astype(BF16), wout_ref[...])


def _merge(x2, g, wg, osb, odf, ocv, wsb, wdf, wcv, wout, tm):
    N, D = x2.shape
    W = osb.shape[1]
    tok = lambda w: pl.BlockSpec((tm, w), lambda i: (i, 0))
    return pl.pallas_call(
        _merge_kernel,
        grid=(N // tm,),
        in_specs=[tok(D), _const_spec((1, D)), _const_spec((D, 3 * D)), tok(W), tok(W), tok(W),
                  _const_spec((W, D)), _const_spec((W, D)), _const_spec((W, D)), _const_spec((D, D))],
        out_specs=tok(D),
        out_shape=jax.ShapeDtypeStruct((N, D), F32),
        compiler_params=_params("parallel"),
        name="gated_merge",
    )(x2, g, wg, osb, odf, ocv, wsb, wdf, wcv, wout)


def _memkv_kernel(mem_ref, g_ref, wkv_ref, o_ref):
    mb = _rmsnorm(mem_ref[0], g_ref[...]).astype(BF16)
    o_ref[0] = _dot(mb, wkv_ref[...]).astype(o_ref.dtype)


def _memkv(mem, g, wkv):
    B, M, D = mem.shape
    return pl.pallas_call(
        _memkv_kernel,
        grid=(B,),
        in_specs=[pl.BlockSpec((1, M, D), lambda b: (b, 0, 0)), _const_spec((1, D)), _const_spec((D, 2 * XA_WIDTH))],
        out_specs=pl.BlockSpec((1, M, 2 * XA_WIDTH), lambda b: (b, 0, 0)),
        out_shape=jax.ShapeDtypeStruct((B, M, 2 * XA_WIDTH), BF16),
        compiler_params=_params("parallel"),
        name="mem_kv",
    )(mem, g, wkv)


def _xattn_kernel(x_ref, g_ref, wq_ref, kv_ref, wo_ref, o_ref):
    x = x_ref[0]
    xb = _rmsnorm(x, g_ref[...]).astype(BF16)
    q = _dot(xb, wq_ref[...]).astype(BF16)
    heads = []
    for h in range(XA_HEADS):
        hs = slice(h * XA_HEAD_DIM, (h + 1) * XA_HEAD_DIM)
        k = kv_ref[0, :, hs]
        v = kv_ref[0, :, XA_WIDTH + h * XA_HEAD_DIM:XA_WIDTH + (h + 1) * XA_HEAD_DIM]
        s = _dot_nt(q[:, hs], k) * (XA_HEAD_DIM ** -0.5)
        p = jnp.exp(s - jnp.max(s, axis=1, keepdims=True))
        p = p / jnp.sum(p, axis=1, keepdims=True)
        heads.append(_dot(p.astype(BF16), v).astype(BF16))
    o = jnp.concatenate(heads, axis=1)
    o_ref[0] = x + _dot(o, wo_ref[...])


def _xattn(x, g, wq, kv, wo, tm):
    B, S, D = x.shape
    M = kv.shape[1]
    return pl.pallas_call(
        _xattn_kernel,
        grid=(B, S // tm),
        in_specs=[
            pl.BlockSpec((1, tm, D), lambda b, i: (b, i, 0)),
            _const_spec((1, D)),
            _const_spec((D, XA_WIDTH)),
            pl.BlockSpec((1, M, 2 * XA_WIDTH), lambda b, i: (b, 0, 0)),
            _const_spec((XA_WIDTH, D)),
        ],
        out_specs=pl.BlockSpec((1, tm, D), lambda b, i: (b, i, 0)),
        out_shape=jax.ShapeDtypeStruct((B, S, D), F32),
        compiler_params=_params("parallel", "parallel"),
        name="cross_attention",
    )(x, g, wq, kv, wo)


def _mlp_kernel(x_ref, g_ref, wup_ref, wdown_ref, gf_ref, o_ref, *, fc, final):
    x = x_ref[...]
    xb = _rmsnorm(x, g_ref[...]).astype(BF16)
    acc = x
    for c in range(wup_ref.shape[1] // fc):
        cs = slice(c * fc, (c + 1) * fc)
        h = jnp.square(jnp.maximum(_dot(xb, wup_ref[:, cs]), 0.0)).astype(BF16)
        acc = acc + _dot(h, wdown_ref[cs, :])
    o_ref[...] = _rmsnorm(acc, gf_ref[...]) if final else acc


def _mlp(x2, g, wup, wdown, gf, tm, final):
    N, D = x2.shape
    F = wup.shape[1]
    return pl.pallas_call(
        functools.partial(_mlp_kernel, fc=1024, final=final),
        grid=(N // tm,),
        in_specs=[pl.BlockSpec((tm, D), lambda i: (i, 0)), _const_spec((1, D)),
                  _const_spec((D, F)), _const_spec((F, D)), _const_spec((1, D))],
        out_specs=pl.BlockSpec((tm, D), lambda i: (i, 0)),
        out_shape=jax.ShapeDtypeStruct((N, D), F32),
        compiler_params=_params("parallel"),
        name="sqrelu_mlp",
    )(x2, g, wup, wdown, gf)


def kernel(x, mem, rel_bias, ln_mix, w_in, diff_lambda, diff_subln, conv_w, conv_b, conv_ln_g, conv_ln_b,
           w_sb_proj, w_diff_proj, w_conv_proj, w_out, ln_xattn, ln_mem, xa_w_q, xa_w_kv, xa_w_o,
           ln_mlp, w_up, w_down, ln_final):
    B, S, D = x.shape
    depth = w_in.shape[0]
    blk = ATT_BLOCK
    tm = min(512, S)
    tc = min(256, S)
    assert S % SB_QBLOCK == 0 and S % DF_QBLOCK == 0 and S % tm == 0 and S % tc == 0 and D == 1024

    row2 = lambda a: a.reshape(1, -1).astype(F32)
    tri = jnp.asarray(np.triu(np.ones((blk, blk), np.float32), k=1), BF16)
    bias = _bias_tiles(rel_bias, blk)
    scale = HEAD_DIM ** -0.5

    for l in range(depth):
        w = w_in[l]
        sbq, sbk, sbv = w[:, 0:512] * scale, w[:, 512:1024], w[:, 1024:1536]
        dfq, dfk, dfv = w[:, 1536:2048] * scale, w[:, 2048:2560], w[:, 2560:3072]
        wrow = jnp.concatenate([w[:, 3072:4096], sbk, dfk], axis=1).astype(BF16)
        wtr = jnp.concatenate([sbq, sbv, dfq, dfv], axis=1).T.astype(BF16)
        wg = w[:, 4096:].astype(BF16)

        row, tr, va, kn = _inproj(x, row2(ln_mix[l]), wrow, wtr, tm)
        o_sb = _sb_attention(row, tr, tri, kn)
        lam_init = 0.8 - 0.6 * math.exp(-0.3 * l)
        o_df = _diff_attention(row, tr, va, bias, kn, diff_lambda[l].astype(F32), row2(diff_subln[l]), lam_init)
        o_cv = _conv(row, conv_w[l].reshape(CONV_WIDTH, CONV_CH).astype(F32), row2(conv_b[l]),
                     row2(conv_ln_g[l]), row2(conv_ln_b[l]), tc)

        N = B * S
        x2 = _merge(x.reshape(N, D), row2(ln_mix[l]), wg,
                    o_sb.reshape(N, -1), o_df.reshape(N, -1), o_cv.reshape(N, -1),
                    w_sb_proj[l].astype(BF16), w_diff_proj[l].astype(BF16), w_conv_proj[l].astype(BF16),
                    w_out[l].astype(BF16), tm)

        kv = _memkv(mem, row2(ln_mem[l]), xa_w_kv[l].astype(BF16))
        x3 = _xattn(x2.reshape(B, S, D), row2(ln_xattn[l]),
                    xa_w_q[l].astype(BF16), kv, xa_w_o[l].astype(BF16), tm)

        x = _mlp(x3.reshape(N, D), row2(ln_mlp[l]), w_up[l].astype(BF16), w_down[l].astype(BF16),
                 row2(ln_final), tm, final=(l == depth - 1)).reshape(B, S, D)
    return x
```

```python
import functools
import math

import numpy as np
import jax
import jax.numpy as jnp
from jax import lax
from jax.experimental import pallas as pl
from jax.experimental.pallas import tpu as pltpu

F32 = jnp.float32
BF16 = jnp.bfloat16

HEAD_DIM = 64
SB_WIDTH = 512
DIFF_WIDTH = 512
DIFF_HEADS = 4
CONV_CH = 512
CONV_WIDTH = 31
XA_HEADS = 4
XA_HEAD_DIM = 128
XA_WIDTH = XA_HEADS * XA_HEAD_DIM
NUM_BUCKETS = 32
MAX_EXACT = NUM_BUCKETS // 2
MAX_DISTANCE = 128
EPS = 1e-6
NEG_INF = -1e30

LANES = 128
SUBLANES = 8
ATT_BLOCK = 256
CONV_HALO = 32
CONV_ROWS = 64
VMEM_LIMIT = 56 * 1024 * 1024

SB_QBLOCK = 2048
DF_QBLOCK = 2048
SB_EXIT = 110.0
DF_UNROLL = 4
DF_SPREAD = 80.0
LOG2E = 1.4426950408889634
ONES_ROWS = 16

ROW_COLS = 2 * CONV_CH + 2 * 512
ROW_SBK, ROW_DFK = 8, 12
TR_ROWS = 3 * 512
TR_SBQ, TR_SBV, TR_DFQ = 0, 4, 8
VA_ROWS = LANES + ONES_ROWS


def _params(*sem):
    return pltpu.CompilerParams(dimension_semantics=sem, vmem_limit_bytes=VMEM_LIMIT)


def _const_spec(shape):
    nd = len(shape)
    return pl.BlockSpec(shape, lambda *_: (0,) * nd)


def _rmsnorm(x, g):
    return x * lax.rsqrt(jnp.mean(x * x, axis=-1, keepdims=True) + EPS) * g


def _dot(a, b):
    return jnp.dot(a, b, preferred_element_type=F32)


def _dot_nt(a, b):
    return lax.dot_general(a, b, (((1,), (1,)), ((), ())), preferred_element_type=F32)


def _inproj_kernel(x_ref, g_ref, wrow_ref, wtr_ref, row_ref, tr_ref, va_ref):
    xb = _rmsnorm(x_ref[0], g_ref[...]).astype(BF16)
    tm = xb.shape[0]
    for c in range(ROW_COLS // 512):
        cs = slice(c * 512, (c + 1) * 512)
        row_ref[0, :, cs] = _dot(xb, wrow_ref[:, cs]).astype(BF16)
    for c in range(TR_ROWS // 512):
        cs = slice(c * 512, (c + 1) * 512)
        tr_ref[0, cs, :] = _dot_nt(wtr_ref[cs, :], xb).astype(BF16)
    dfv = _dot_nt(wtr_ref[TR_ROWS:, :], xb).astype(BF16)
    for h in range(DIFF_HEADS):
        va_ref[0, h * VA_ROWS:h * VA_ROWS + LANES, :] = dfv[h * LANES:(h + 1) * LANES, :]
        va_ref[0, h * VA_ROWS + LANES:(h + 1) * VA_ROWS, :] = jnp.ones((ONES_ROWS, tm), BF16)


def _inproj(x, g, wrow, wtr, tm):
    B, S, D = x.shape
    return pl.pallas_call(
        _inproj_kernel,
        grid=(B, S // tm),
        in_specs=[
            pl.BlockSpec((1, tm, D), lambda b, i: (b, i, 0)),
            _const_spec((1, D)),
            _const_spec((D, ROW_COLS)),
            _const_spec((TR_ROWS + DIFF_WIDTH, D)),
        ],
        out_specs=[
            pl.BlockSpec((1, tm, ROW_COLS), lambda b, i: (b, i, 0)),
            pl.BlockSpec((1, TR_ROWS, tm), lambda b, i: (b, 0, i)),
            pl.BlockSpec((1, DIFF_HEADS * VA_ROWS, tm), lambda b, i: (b, 0, i)),
        ],
        out_shape=[
            jax.ShapeDtypeStruct((B, S, ROW_COLS), BF16),
            jax.ShapeDtypeStruct((B, TR_ROWS, S), BF16),
            jax.ShapeDtypeStruct((B, DIFF_HEADS * VA_ROWS, S), BF16),
        ],
        compiler_params=_params("parallel", "parallel"),
        name="inproj",
    )(x, g, wrow, wtr)


def _knorm_kernel(k_ref, sel_ref, o_ref):
    k = k_ref[0].astype(F32)
    n2 = _dot((k * k).astype(BF16), sel_ref[...])
    o_ref[0] = jnp.sqrt(jnp.max(n2, axis=0, keepdims=True))


def _key_norm_max(row):
    B, S, _ = row.shape
    width = SB_WIDTH + DIFF_WIDTH
    sel = np.zeros((width, LANES), np.float32)
    sel[np.arange(width), np.arange(width) // HEAD_DIM] = 1.0
    return pl.pallas_call(
        _knorm_kernel,
        grid=(B,),
        in_specs=[pl.BlockSpec((1, S, width), lambda b: (b, 0, ROW_SBK * LANES // width)),
                  _const_spec((width, LANES))],
        out_specs=pl.BlockSpec((1, 1, LANES), lambda b: (b, 0, 0)),
        out_shape=jax.ShapeDtypeStruct((B, 1, LANES), F32),
        compiler_params=_params("parallel"),
        name="key_norm_max",
    )(row, jnp.asarray(sel, BF16))


def _sb_kernel(qt_ref, k_ref, vt_ref, tri_ref, kn_ref, o_ref):
    kb, qb = ATT_BLOCK, SB_QBLOCK
    nq = qb // kb
    p, i = pl.program_id(1), pl.program_id(2)
    qt = qt_ref[0]
    sub = lax.broadcasted_iota(jnp.int32, qt.shape, 0)
    zero = jnp.zeros_like(qt)
    qtm = (jnp.where(sub < HEAD_DIM, qt, zero), jnp.where(sub >= HEAD_DIM, qt, zero))
    tri = tri_ref[...]

    lane = lax.broadcasted_iota(jnp.int32, (1, LANES), 1)
    kn = kn_ref[0]
    zbound = []
    for h in range(2):
        knh = jnp.max(jnp.where(lane == 2 * p + h, kn, 0.0), axis=1, keepdims=True)
        qf = qtm[h].astype(F32)
        zbound.append(jnp.sqrt(jnp.sum(qf * qf, axis=0, keepdims=True)) * (knh * 1.01))

    def splice(old, new, lo, hi):
        parts = ([old[:, :lo]] if lo else []) + [new] + ([old[:, hi:]] if hi < qb else [])
        return parts[0] if len(parts) == 1 else jnp.concatenate(parts, axis=1)

    def tiles(jobs, st):
        carry, acc = [st[0], st[2]], [st[1], st[3]]
        ks = [pl.ds(pl.multiple_of(j * kb, kb), kb) for j, _, _, _ in jobs]
        units = [(n, h) for n in range(len(jobs)) for h in range(2)]
        z = {(n, h): _dot(k_ref[0, ks[n], :], qtm[h][:, jobs[n][1]:jobs[n][2]]) for n, h in units}
        sp = {u: jnp.maximum(z[u], 0.0) + jnp.log(1.0 + jnp.exp2(jnp.abs(z[u]) * -LOG2E)) for u in units}
        own = {u: z[u] - sp[u] for u in units}
        keep = []
        for n, (_, lo, hi, valid) in enumerate(jobs):
            m = (lax.broadcasted_iota(jnp.int32, (kb, hi - lo), 0)
                 < lax.broadcasted_iota(jnp.int32, (kb, hi - lo), 1)) if valid is None else valid
            keep.append(m)
            for h in range(2):
                sp[n, h] = jnp.where(m, sp[n, h], 0.0)
        cs = {u: _dot(tri, sp[u].astype(BF16)) for u in units}
        oldest_row = {u: sp[u][0:1, :] for u in units}
        for n, (_, lo, hi, _) in enumerate(jobs):
            vt = vt_ref[0, :, ks[n]]
            c = [cs[n, h] + carry[h][:, lo:hi] for h in range(2)]
            a = [jnp.where(keep[n], jnp.exp(own[n, h] - c[h]), 0.0) for h in range(2)]
            pv = [_dot(vt[h * HEAD_DIM:(h + 1) * HEAD_DIM, :], a[h].astype(BF16)) for h in range(2)]
            for h in range(2):
                carry[h] = splice(carry[h], c[h][0:1, :] + oldest_row[n, h], lo, hi)
                acc[h] = splice(acc[h], acc[h][:, lo:hi] + pv[h], lo, hi)
        return carry[0], acc[0], carry[1], acc[1]

    zc = jnp.zeros((1, qb), F32)
    za = jnp.zeros((HEAD_DIM, qb), F32)
    newest = [(nq * i + c, c * kb, min(c + 2, nq) * kb, None) for c in reversed(range(nq))]

    st = lax.cond(i > 0, lambda: tiles(newest + [(nq * i - 1, 0, kb, True)], (zc, za, zc, za)),
                  lambda: tiles(newest, (zc, za, zc, za)))

    def strip_step(j, cols, s):
        ks = pl.ds(pl.multiple_of(j * kb, kb), kb)
        k, vt = k_ref[0, ks, :], vt_ref[0, :, ks]
        z = [_dot(k, qtm[h][:, cols]) for h in range(2)]
        sp = [jnp.maximum(z[h], 0.0) + jnp.log(1.0 + jnp.exp2(jnp.abs(z[h]) * -LOG2E)) for h in range(2)]
        c = [_dot(tri, sp[h].astype(BF16)) + s[2 * h] for h in range(2)]
        a = [jnp.exp((z[h] - sp[h]) - c[h]).astype(BF16) for h in range(2)]
        pv = [_dot(vt[h * HEAD_DIM:(h + 1) * HEAD_DIM, :], a[h]) for h in range(2)]
        return (c[0][0:1, :] + sp[0][0:1, :], s[1] + pv[0], c[1][0:1, :] + sp[1][0:1, :], s[3] + pv[1])

    out = []
    for n in range(nq):
        cols = slice(n * kb, (n + 1) * kb)
        zb = (zbound[0][:, cols], zbound[1][:, cols])

        def unfinished(s, zb=zb):
            return (jnp.min(jnp.minimum(s[0] - zb[0], s[2] - zb[1])) <= SB_EXIT).astype(jnp.int32)

        def body(s, n=n, cols=cols, unfinished=unfinished):
            new = strip_step(nq * i + n - s[0], cols, s[2:])
            return (s[0] + 1, unfinished(new)) + new

        s0 = tuple(x[:, cols] for x in st)
        out.append(lax.while_loop(lambda s, n=n: jnp.logical_and(s[0] <= nq * i + n, s[1] > 0), body,
                                  (jnp.int32(2), unfinished(s0)) + s0)[2:])
    acc = [jnp.concatenate([o[1 + 2 * h] for o in out], axis=1) for h in range(2)]
    o_ref[0] = jnp.concatenate(acc, axis=0).T.astype(o_ref.dtype)


def _sb_attention(row, tr, tri, kn):
    B, S, _ = row.shape
    qb = SB_QBLOCK
    return pl.pallas_call(
        _sb_kernel,
        grid=(B, SB_WIDTH // LANES, S // qb),
        in_specs=[
            pl.BlockSpec((1, LANES, qb), lambda b, p, i: (b, TR_SBQ + p, i)),
            pl.BlockSpec((1, S, LANES), lambda b, p, i: (b, 0, ROW_SBK + p)),
            pl.BlockSpec((1, LANES, S), lambda b, p, i: (b, TR_SBV + p, 0)),
            _const_spec((ATT_BLOCK, ATT_BLOCK)),
            pl.BlockSpec((1, 1, LANES), lambda b, p, i: (b, 0, 0)),
        ],
        out_specs=pl.BlockSpec((1, qb, LANES), lambda b, p, i: (b, i, p)),
        out_shape=jax.ShapeDtypeStruct((B, S, SB_WIDTH), BF16),
        compiler_params=_params("parallel", "parallel", "arbitrary"),
        name="sb_attention",
    )(tr, row, tr, tri, kn)


def _bucket_tiles(blk):
    r = np.arange(blk)[None, :] - np.arange(blk)[:, None]
    rel = np.stack([r, r + blk, r + 2 * blk]).astype(np.int64)
    n = np.maximum(rel, 0)
    nf = np.maximum(n, 1).astype(np.float32)
    large = MAX_EXACT + (np.log(nf / np.float32(MAX_EXACT)) / np.float32(math.log(MAX_DISTANCE / MAX_EXACT))
                         * np.float32(NUM_BUCKETS - MAX_EXACT)).astype(np.int32)
    large = np.minimum(large, NUM_BUCKETS - 1)
    idx = np.where(n < MAX_EXACT, n, large).astype(np.int32)
    assert (idx[2] == NUM_BUCKETS - 1).all() and blk >= MAX_DISTANCE
    return idx


def _bias_kernel(table_ref, idx_ref, o_ref):
    h = pl.program_id(0)
    idx = idx_ref[0]
    acc = jnp.zeros(idx.shape, F32)
    for b in range(NUM_BUCKETS):
        acc = jnp.where(idx == b, table_ref[b, h], acc)
    o_ref[0, 0] = acc


def _bias_tiles(rel_bias, blk):
    idx = jnp.asarray(_bucket_tiles(blk))
    return pl.pallas_call(
        _bias_kernel,
        grid=(DIFF_HEADS, 3),
        in_specs=[
            pl.BlockSpec(memory_space=pltpu.SMEM),
            pl.BlockSpec((1, blk, blk), lambda h, d: (d, 0, 0)),
        ],
        out_specs=pl.BlockSpec((1, 1, blk, blk), lambda h, d: (h, d, 0, 0)),
        out_shape=jax.ShapeDtypeStruct((DIFF_HEADS, 3, blk, blk), F32),
        name="bias_tiles",
    )(rel_bias.astype(F32), idx)


def _diff_kernel(qt_ref, k_ref, va_ref, bias_ref, kn_ref, lam_ref, subln_ref, o_ref, *, lam_init):
    kb, qb = ATT_BLOCK, DF_QBLOCK
    nq = qb // kb
    h, i = pl.program_id(1), pl.program_id(2)
    qt = qt_ref[0]
    sub = lax.broadcasted_iota(jnp.int32, qt.shape, 0)
    zero = jnp.zeros_like(qt)
    qtm = (jnp.where(sub < HEAD_DIM, qt, zero), jnp.where(sub >= HEAD_DIM, qt, zero))
    far = bias_ref[0, 2][0:1, 0:1]

    def scores(j, lo):
        k = k_ref[0, pl.ds(pl.multiple_of(j * kb, kb), kb), :]
        return [_dot(k, qtm[m][:, lo:]) for m in range(2)]

    def update(z, j, st, lo, bias, mask, online):
        vta = va_ref[0, :, pl.ds(pl.multiple_of(j * kb, kb), kb)]
        if bias is not None:
            z = [x + bias for x in z]
        if mask is not None:
            z = [jnp.where(mask, x, NEG_INF) for x in z]
        mnew = [st[m][0][:, lo:] for m in range(2)]
        scale = None
        if online:
            top = [jnp.max(x, axis=0, keepdims=True) for x in z]
            if bias is None:
                top = [t + far for t in top]
            mold, mnew = mnew, [jnp.maximum(mnew[m], top[m]) for m in range(2)]
            scale = [jnp.exp(mold[m] - mnew[m]) for m in range(2)]
        shift = [mn - far for mn in mnew] if bias is None else mnew
        p = [jnp.exp(z[m] - shift[m]) for m in range(2)]
        pv = [_dot(vta, p[m].astype(BF16)) for m in range(2)]
        out = []
        for m in range(2):
            l, acc = st[m][1][:, lo:], st[m][2][:, lo:]
            if online:
                l, acc = scale[m] * l, scale[m] * acc
            new = (mnew[m], l + pv[m][LANES:LANES + 1, :], acc + pv[m][:LANES, :])
            if lo:
                new = tuple(jnp.concatenate([old[:, :lo], x], axis=1) for old, x in zip(st[m], new))
            out.append(new)
        return tuple(out)

    def run(stab, online):
        zl, za = jnp.zeros((1, qb), F32), jnp.zeros((LANES, qb), F32)
        st = ((stab[0], zl, za), (stab[1], zl, za))
        for r in reversed(range(nq)):
            n = qb - r * kb
            bias = jnp.concatenate([bias_ref[0, min(c - r, 2)] for c in range(r, nq)], axis=1)
            causal = lax.broadcasted_iota(jnp.int32, (kb, n), 0) <= lax.broadcasted_iota(jnp.int32, (kb, n), 1)
            st = update(scores(nq * i + r, r * kb), nq * i + r, st, r * kb, bias, causal, online)
        def older(st):
            bias = jnp.concatenate([bias_ref[0, min(c + 1, 2)] for c in range(nq)], axis=1)
            st = update(scores(nq * i - 1, 0), nq * i - 1, st, 0, bias, None, online)
            for d in range(2, DF_UNROLL + 1):
                st = update(scores(nq * i - d, 0), nq * i - d, st, 0, None, None, online)

            def body(jj, s):
                for d in range(DF_UNROLL):
                    j = nq * i - DF_UNROLL * (jj + 1) - 1 - d
                    s = update(scores(j, 0), j, s, 0, None, None, online)
                return s

            return lax.fori_loop(0, (nq * i) // DF_UNROLL - 1, body, st)

        return lax.cond(i > 0, older, lambda s: s, st)

    lane = lax.broadcasted_iota(jnp.int32, (1, LANES), 1)
    kn = kn_ref[0]
    allb = bias_ref[0]
    bmax = jnp.max(jnp.max(allb, axis=0), axis=(0, 1), keepdims=True)
    bmin = jnp.min(jnp.min(allb, axis=0), axis=(0, 1), keepdims=True)
    qk = []
    for m in range(2):
        knm = jnp.max(jnp.where(lane == SB_WIDTH // HEAD_DIM + 2 * h + m, kn, 0.0), axis=1, keepdims=True)
        qf = qtm[m].astype(F32)
        qk.append(jnp.sqrt(jnp.sum(qf * qf, axis=0, keepdims=True)) * (knm * 1.01))
    spread = 2.0 * jnp.maximum(qk[0], qk[1]) + (bmax - bmin)
    bound = [x + bmax for x in qk]
    neg = jnp.full((1, qb), NEG_INF, F32)
    st = lax.cond(jnp.max(spread) <= DF_SPREAD,
                  lambda: run(bound, False), lambda: run([neg, neg], True))

    lp = lam_ref[...]
    lam = (jnp.exp(jnp.sum(lp[0:1] * lp[1:2], axis=1, keepdims=True))
           - jnp.exp(jnp.sum(lp[2:3] * lp[3:4], axis=1, keepdims=True)) + lam_init)
    (_, l0, acc0), (_, l1, acc1) = st
    o = (acc0 / l0 - lam * (acc1 / l1)).T
    o_ref[0] = (_rmsnorm(o, subln_ref[...]) * (1.0 - lam_init)).astype(o_ref.dtype)


def _diff_attention(row, tr, va, bias, kn, lam_p, subln, lam_init):
    B, S, _ = row.shape
    kb, qb = ATT_BLOCK, DF_QBLOCK
    return pl.pallas_call(
        functools.partial(_diff_kernel, lam_init=lam_init),
        grid=(B, DIFF_HEADS, S // qb),
        in_specs=[
            pl.BlockSpec((1, LANES, qb), lambda b, h, i: (b, TR_DFQ + h, i)),
            pl.BlockSpec((1, S, LANES), lambda b, h, i: (b, 0, ROW_DFK + h)),
            pl.BlockSpec((1, VA_ROWS, S), lambda b, h, i: (b, h, 0)),
            pl.BlockSpec((1, 3, kb, kb), lambda b, h, i: (h, 0, 0, 0)),
            pl.BlockSpec((1, 1, LANES), lambda b, h, i: (b, 0, 0)),
            _const_spec((4, HEAD_DIM)),
            _const_spec((1, 2 * HEAD_DIM)),
        ],
        out_specs=pl.BlockSpec((1, qb, LANES), lambda b, h, i: (b, i, h)),
        out_shape=jax.ShapeDtypeStruct((B, S, DIFF_WIDTH), BF16),
        compiler_params=_params("parallel", "parallel", "arbitrary"),
        name="diff_attention",
    )(tr, row, va, bias, kn, lam_p, subln)


def _conv_kernel(u_ref, uprev_ref, cw_ref, cb_ref, lg_ref, lb_ref, o_ref, h_ref, hs_ref, *, tc):
    i = pl.program_id(1)

    def glu(u):
        u = u.astype(F32)
        return u[:, :CONV_CH] * jax.nn.sigmoid(u[:, CONV_CH:])

    h_ref[0:CONV_HALO, :] = jnp.where(i == 0, 0.0, glu(uprev_ref[0]))
    h_ref[CONV_HALO:CONV_HALO + tc, :] = glu(u_ref[0])
    for b in range(1, SUBLANES):
        hs_ref[b - 1] = h_ref[b:b + tc + CONV_HALO - SUBLANES, :]
    first = CONV_HALO - (CONV_WIDTH - 1)
    for c in range(tc // CONV_ROWS):
        acc = jnp.zeros((CONV_ROWS, CONV_CH), F32)
        for w in range(CONV_WIDTH):
            a, b = divmod(first + w, SUBLANES)
            r0 = c * CONV_ROWS + a * SUBLANES
            src = h_ref[r0:r0 + CONV_ROWS, :] if b == 0 else hs_ref[b - 1, r0:r0 + CONV_ROWS, :]
            acc = acc + src * cw_ref[w:w + 1, :]
        y = acc + cb_ref[...]
        mu = jnp.mean(y, axis=-1, keepdims=True)
        var = jnp.mean(jnp.square(y - mu), axis=-1, keepdims=True)
        y = (y - mu) * lax.rsqrt(var + EPS) * lg_ref[...] + lb_ref[...]
        o_ref[0, c * CONV_ROWS:(c + 1) * CONV_ROWS, :] = (y * jax.nn.sigmoid(y)).astype(o_ref.dtype)


def _conv(row, cw, cb, lg, lb, tc):
    B, S, _ = row.shape
    ucol = 0
    per = tc // CONV_HALO
    return pl.pallas_call(
        functools.partial(_conv_kernel, tc=tc),
        grid=(B, S // tc),
        in_specs=[
            pl.BlockSpec((1, tc, 2 * CONV_CH), lambda b, i: (b, i, ucol)),
            pl.BlockSpec((1, CONV_HALO, 2 * CONV_CH), lambda b, i: (b, jnp.maximum(i * per - 1, 0), ucol)),
            _const_spec((CONV_WIDTH, CONV_CH)),
            _const_spec((1, CONV_CH)),
            _const_spec((1, CONV_CH)),
            _const_spec((1, CONV_CH)),
        ],
        out_specs=pl.BlockSpec((1, tc, CONV_CH), lambda b, i: (b, i, 0)),
        out_shape=jax.ShapeDtypeStruct((B, S, CONV_CH), BF16),
        scratch_shapes=[pltpu.VMEM((CONV_HALO + tc, CONV_CH), F32),
                        pltpu.VMEM((SUBLANES - 1, CONV_HALO + tc - SUBLANES, CONV_CH), F32)],
        compiler_params=_params("parallel", "parallel"),
        name="conformer_conv",
    )(row, row, cw, cb, lg, lb)


def _merge_kernel(x_ref, g_ref, wg_ref, osb_ref, odf_ref, ocv_ref, wsb_ref, wdf_ref, wcv_ref, wout_ref, o_ref):
    x = x_ref[...]
    D = x.shape[1]
    xb = _rmsnorm(x, g_ref[...]).astype(BF16)
    y = None
    for n, (o_br, w_br) in enumerate(((osb_ref, wsb_ref), (odf_ref, wdf_ref), (ocv_ref, wcv_ref))):
        gate = jax.nn.sigmoid(_dot(xb, wg_ref[:, n * D:(n + 1) * D]))
        t = gate * _dot(o_br[...], w_br[...])
        y = t if y is None else y + t
    o_ref[...] = x + _dot(y.astype(BF16), wout_ref[...])


def _merge(x2, g, wg, osb, odf, ocv, wsb, wdf, wcv, wout, tm):
    N, D = x2.shape
    W = osb.shape[1]
    tok = lambda w: pl.BlockSpec((tm, w), lambda i: (i, 0))
    return pl.pallas_call(
        _merge_kernel,
        grid=(N // tm,),
        in_specs=[tok(D), _const_spec((1, D)), _const_spec((D, 3 * D)), tok(W), tok(W), tok(W),
                  _const_spec((W, D)), _const_spec((W, D)), _const_spec((W, D)), _const_spec((D, D))],
        out_specs=tok(D),
        out_shape=jax.ShapeDtypeStruct((N, D), F32),
        compiler_params=_params("parallel"),
        name="gated_merge",
    )(x2, g, wg, osb, odf, ocv, wsb, wdf, wcv, wout)


def _memkv_kernel(mem_ref, g_ref, wkv_ref, o_ref):
    mb = _rmsnorm(mem_ref[0], g_ref[...]).astype(BF16)
    o_ref[0] = _dot(mb, wkv_ref[...]).astype(o_ref.dtype)


def _memkv(mem, g, wkv):
    B, M, D = mem.shape
    return pl.pallas_call(
        _memkv_kernel,
        grid=(B,),
        in_specs=[pl.BlockSpec((1, M, D), lambda b: (b, 0, 0)), _const_spec((1, D)), _const_spec((D, 2 * XA_WIDTH))],
        out_specs=pl.BlockSpec((1, M, 2 * XA_WIDTH), lambda b: (b, 0, 0)),
        out_shape=jax.ShapeDtypeStruct((B, M, 2 * XA_WIDTH), BF16),
        compiler_params=_params("parallel"),
        name="mem_kv",
    )(mem, g, wkv)


def _xattn_kernel(x_ref, g_ref, wq_ref, kv_ref, wo_ref, o_ref):
    x = x_ref[0]
    xb = _rmsnorm(x, g_ref[...]).astype(BF16)
    q = _dot(xb, wq_ref[...]).astype(BF16)
    heads = []
    for h in range(XA_HEADS):
        hs = slice(h * XA_HEAD_DIM, (h + 1) * XA_HEAD_DIM)
        k = kv_ref[0, :, hs]
        v = kv_ref[0, :, XA_WIDTH + h * XA_HEAD_DIM:XA_WIDTH + (h + 1) * XA_HEAD_DIM]
        s = _dot_nt(q[:, hs], k) * (XA_HEAD_DIM ** -0.5)
        p = jnp.exp(s - jnp.max(s, axis=1, keepdims=True))
        p = p / jnp.sum(p, axis=1, keepdims=True)
        heads.append(_dot(p.astype(BF16), v).astype(BF16))
    o = jnp.concatenate(heads, axis=1)
    o_ref[0] = x + _dot(o, wo_ref[...])


def _xattn(x, g, wq, kv, wo, tm):
    B, S, D = x.shape
    M = kv.shape[1]
    return pl.pallas_call(
        _xattn_kernel,
        grid=(B, S // tm),
        in_specs=[
            pl.BlockSpec((1, tm, D), lambda b, i: (b, i, 0)),
            _const_spec((1, D)),
            _const_spec((D, XA_WIDTH)),
            pl.BlockSpec((1, M, 2 * XA_WIDTH), lambda b, i: (b, 0, 0)),
            _const_spec((XA_WIDTH, D)),
        ],
        out_specs=pl.BlockSpec((1, tm, D), lambda b, i: (b, i, 0)),
        out_shape=jax.ShapeDtypeStruct((B, S, D), F32),
        compiler_params=_params("parallel", "parallel"),
        name="cross_attention",
    )(x, g, wq, kv, wo)


def _mlp_kernel(x_ref, g_ref, wup_ref, wdown_ref, gf_ref, o_ref, *, fc, final):
    x = x_ref[...]
    xb = _rmsnorm(x, g_ref[...]).astype(BF16)
    acc = x
    for c in range(wup_ref.shape[1] // fc):
        cs = slice(c * fc, (c + 1) * fc)
        h = jnp.square(jnp.maximum(_dot(xb, wup_ref[:, cs]), 0.0)).astype(BF16)
        acc = acc + _dot(h, wdown_ref[cs, :])
    o_ref[...] = _rmsnorm(acc, gf_ref[...]) if final else acc


def _mlp(x2, g, wup, wdown, gf, tm, final):
    N, D = x2.shape
    F = wup.shape[1]
    return pl.pallas_call(
        functools.partial(_mlp_kernel, fc=1024, final=final),
        grid=(N // tm,),
        in_specs=[pl.BlockSpec((tm, D), lambda i: (i, 0)), _const_spec((1, D)),
                  _const_spec((D, F)), _const_spec((F, D)), _const_spec((1, D))],
        out_specs=pl.BlockSpec((tm, D), lambda i: (i, 0)),
        out_shape=jax.ShapeDtypeStruct((N, D), F32),
        compiler_params=_params("parallel"),
        name="sqrelu_mlp",
    )(x2, g, wup, wdown, gf)


def kernel(x, mem, rel_bias, ln_mix, w_in, diff_lambda, diff_subln, conv_w, conv_b, conv_ln_g, conv_ln_b,
           w_sb_proj, w_diff_proj, w_conv_proj, w_out, ln_xattn, ln_mem, xa_w_q, xa_w_kv, xa_w_o,
           ln_mlp, w_up, w_down, ln_final):
    B, S, D = x.shape
    depth = w_in.shape[0]
    blk = ATT_BLOCK
    tm = min(512, S)
    tc = min(256, S)
    assert S % SB_QBLOCK == 0 and S % DF_QBLOCK == 0 and S % tm == 0 and S % tc == 0 and D == 1024

    row2 = lambda a: a.reshape(1, -1).astype(F32)
    tri = jnp.asarray(np.triu(np.ones((blk, blk), np.float32), k=1), BF16)
    bias = _bias_tiles(rel_bias, blk)
    scale = HEAD_DIM ** -0.5

    for l in range(depth):
        w = w_in[l]
        sbq, sbk, sbv = w[:, 0:512] * scale, w[:, 512:1024], w[:, 1024:1536]
        dfq, dfk, dfv = w[:, 1536:2048] * scale, w[:, 2048:2560], w[:, 2560:3072]
        wrow = jnp.concatenate([w[:, 3072:4096], sbk, dfk], axis=1).astype(BF16)
        wtr = jnp.concatenate([sbq, sbv, dfq, dfv], axis=1).T.astype(BF16)
        wg = w[:, 4096:].astype(BF16)

        row, tr, va = _inproj(x, row2(ln_mix[l]), wrow, wtr, tm)
        kn = _key_norm_max(row)
        o_sb = _sb_attention(row, tr, tri, kn)
        lam_init = 0.8 - 0.6 * math.exp(-0.3 * l)
        o_df = _diff_attention(row, tr, va, bias, kn, diff_lambda[l].astype(F32), row2(diff_subln[l]), lam_init)
        o_cv = _conv(row, conv_w[l].reshape(CONV_WIDTH, CONV_CH).astype(F32), row2(conv_b[l]),
                     row2(conv_ln_g[l]), row2(conv_ln_b[l]), tc)

        N = B * S
        x2 = _merge(x.reshape(N, D), row2(ln_mix[l]), wg,
                    o_sb.reshape(N, -1), o_df.reshape(N, -1), o_cv.reshape(N, -1),
                    w_sb_proj[l].astype(BF16), w_diff_proj[l].astype(BF16), w_conv_proj[l].astype(BF16),
                    w_out[l].astype(BF16), tm)

        kv = _memkv(mem, row2(ln_mem[l]), xa_w_kv[l].astype(BF16))
        x3 = _xattn(x2.reshape(B, S, D), row2(ln_xattn[l]),
                    xa_w_q[l].astype(BF16), kv, xa_w_o[l].astype(BF16), tm)

        x = _mlp(x3.reshape(N, D), row2(ln_mlp[l]), w_up[l].astype(BF16), w_down[l].astype(BF16),
                 row2(ln_final), tm, final=(l == depth - 1)).reshape(B, S, D)
    return x
```

```python
import functools
import math

import numpy as np
import jax
import jax.numpy as jnp
from jax import lax
from jax.experimental import pallas as pl
from jax.experimental.pallas import tpu as pltpu

F32 = jnp.float32
BF16 = jnp.bfloat16

HEAD_DIM = 64
SB_WIDTH = 512
DIFF_WIDTH = 512
DIFF_HEADS = 4
CONV_CH = 512
CONV_WIDTH = 31
XA_HEADS = 4
XA_HEAD_DIM = 128
XA_WIDTH = XA_HEADS * XA_HEAD_DIM
NUM_BUCKETS = 32
MAX_EXACT = NUM_BUCKETS // 2
MAX_DISTANCE = 128
EPS = 1e-6
NEG_INF = -1e30

LANES = 128
SUBLANES = 8
ATT_BLOCK = 256
CONV_HALO = 32
CONV_ROWS = 64
VMEM_LIMIT = 56 * 1024 * 1024
TOKEN_TILE = 512
CONV_TILE = 256
PROJ_CHUNK = 512
MLP_CHUNK = 1024
NORM_SLACK = 1.01

SB_QBLOCK = 2048
DF_QBLOCK = 2048
SB_EXIT = 110.0
DF_UNROLL = 4
DF_SPREAD = 80.0
LOG2E = 1.4426950408889634
ONES_ROWS = 16

ROW_COLS = 2 * CONV_CH + SB_WIDTH + DIFF_WIDTH
ROW_SBK, ROW_DFK = 8, 12
TR_ROWS = 2 * SB_WIDTH + DIFF_WIDTH
TR_SBQ, TR_SBV, TR_DFQ = 0, 4, 8
VA_ROWS = LANES + ONES_ROWS


def _params(*sem):
    return pltpu.CompilerParams(dimension_semantics=sem, vmem_limit_bytes=VMEM_LIMIT)


def _const_spec(shape):
    nd = len(shape)
    return pl.BlockSpec(shape, lambda *_: (0,) * nd)


def _rmsnorm(x, g):
    return x * lax.rsqrt(jnp.mean(x * x, axis=-1, keepdims=True) + EPS) * g


def _dot(a, b):
    return jnp.dot(a, b, preferred_element_type=F32)


def _dot_nt(a, b):
    return lax.dot_general(a, b, (((1,), (1,)), ((), ())), preferred_element_type=F32)


def _inproj_kernel(x_ref, g_ref, wrow_ref, wtr_ref, row_ref, tr_ref, va_ref):
    xb = _rmsnorm(x_ref[0], g_ref[...]).astype(BF16)
    tm = xb.shape[0]
    for c in range(ROW_COLS // PROJ_CHUNK):
        cs = slice(c * PROJ_CHUNK, (c + 1) * PROJ_CHUNK)
        row_ref[0, :, cs] = _dot(xb, wrow_ref[:, cs]).astype(BF16)
    for c in range(TR_ROWS // PROJ_CHUNK):
        cs = slice(c * PROJ_CHUNK, (c + 1) * PROJ_CHUNK)
        tr_ref[0, cs, :] = _dot_nt(wtr_ref[cs, :], xb).astype(BF16)
    dfv = _dot_nt(wtr_ref[TR_ROWS:, :], xb).astype(BF16)
    for h in range(DIFF_HEADS):
        va_ref[0, h * VA_ROWS:h * VA_ROWS + LANES, :] = dfv[h * LANES:(h + 1) * LANES, :]
        va_ref[0, h * VA_ROWS + LANES:(h + 1) * VA_ROWS, :] = jnp.ones((ONES_ROWS, tm), BF16)


def _inproj(x, g, wrow, wtr, tm):
    B, S, D = x.shape
    return pl.pallas_call(
        _inproj_kernel,
        grid=(B, S // tm),
        in_specs=[
            pl.BlockSpec((1, tm, D), lambda b, i: (b, i, 0)),
            _const_spec((1, D)),
            _const_spec((D, ROW_COLS)),
            _const_spec((TR_ROWS + DIFF_WIDTH, D)),
        ],
        out_specs=[
            pl.BlockSpec((1, tm, ROW_COLS), lambda b, i: (b, i, 0)),
            pl.BlockSpec((1, TR_ROWS, tm), lambda b, i: (b, 0, i)),
            pl.BlockSpec((1, DIFF_HEADS * VA_ROWS, tm), lambda b, i: (b, 0, i)),
        ],
        out_shape=[
            jax.ShapeDtypeStruct((B, S, ROW_COLS), BF16),
            jax.ShapeDtypeStruct((B, TR_ROWS, S), BF16),
            jax.ShapeDtypeStruct((B, DIFF_HEADS * VA_ROWS, S), BF16),
        ],
        compiler_params=_params("parallel", "parallel"),
        name="inproj",
    )(x, g, wrow, wtr)


def _knorm_kernel(k_ref, sel_ref, o_ref):
    k = k_ref[0].astype(F32)
    n2 = _dot((k * k).astype(BF16), sel_ref[...])
    o_ref[0] = jnp.sqrt(jnp.max(n2, axis=0, keepdims=True))


def _key_norm_max(row):
    B, S, _ = row.shape
    width = SB_WIDTH + DIFF_WIDTH
    sel = np.zeros((width, LANES), np.float32)
    sel[np.arange(width), np.arange(width) // HEAD_DIM] = 1.0
    return pl.pallas_call(
        _knorm_kernel,
        grid=(B,),
        in_specs=[pl.BlockSpec((1, S, width), lambda b: (b, 0, ROW_SBK * LANES // width)),
                  _const_spec((width, LANES))],
        out_specs=pl.BlockSpec((1, 1, LANES), lambda b: (b, 0, 0)),
        out_shape=jax.ShapeDtypeStruct((B, 1, LANES), F32),
        compiler_params=_params("parallel"),
        name="key_norm_max",
    )(row, jnp.asarray(sel, BF16))


def _sb_kernel(qt_ref, k_ref, vt_ref, tri_ref, kn_ref, o_ref):
    kb, qb = ATT_BLOCK, SB_QBLOCK
    nq = qb // kb
    p, i = pl.program_id(1), pl.program_id(2)
    qt = qt_ref[0]
    sub = lax.broadcasted_iota(jnp.int32, qt.shape, 0)
    zero = jnp.zeros_like(qt)
    qtm = (jnp.where(sub < HEAD_DIM, qt, zero), jnp.where(sub >= HEAD_DIM, qt, zero))
    tri = tri_ref[...]

    lane = lax.broadcasted_iota(jnp.int32, (1, LANES), 1)
    kn = kn_ref[0]
    zbound = []
    for h in range(2):
        knh = jnp.max(jnp.where(lane == 2 * p + h, kn, 0.0), axis=1, keepdims=True)
        qf = qtm[h].astype(F32)
        zbound.append(jnp.sqrt(jnp.sum(qf * qf, axis=0, keepdims=True)) * (knh * NORM_SLACK))

    def splice(old, new, lo, hi):
        parts = ([old[:, :lo]] if lo else []) + [new] + ([old[:, hi:]] if hi < qb else [])
        return parts[0] if len(parts) == 1 else jnp.concatenate(parts, axis=1)

    def tiles(jobs, st):
        carry, acc = [st[0], st[2]], [st[1], st[3]]
        ks = [pl.ds(pl.multiple_of(j * kb, kb), kb) for j, _, _, _ in jobs]
        units = [(n, h) for n in range(len(jobs)) for h in range(2)]
        z = {(n, h): _dot(k_ref[0, ks[n], :], qtm[h][:, jobs[n][1]:jobs[n][2]]) for n, h in units}
        sp = {u: jnp.maximum(z[u], 0.0) + jnp.log(1.0 + jnp.exp2(jnp.abs(z[u]) * -LOG2E)) for u in units}
        own = {u: z[u] - sp[u] for u in units}
        keep = []
        for n, (_, lo, hi, valid) in enumerate(jobs):
            m = (lax.broadcasted_iota(jnp.int32, (kb, hi - lo), 0)
                 < lax.broadcasted_iota(jnp.int32, (kb, hi - lo), 1)) if valid is None else valid
            keep.append(m)
            for h in range(2):
                sp[n, h] = jnp.where(m, sp[n, h], 0.0)
        cs = {u: _dot(tri, sp[u].astype(BF16)) for u in units}
        oldest_row = {u: sp[u][0:1, :] for u in units}
        for n, (_, lo, hi, _) in enumerate(jobs):
            vt = vt_ref[0, :, ks[n]]
            c = [cs[n, h] + carry[h][:, lo:hi] for h in range(2)]
            a = [jnp.where(keep[n], jnp.exp(own[n, h] - c[h]), 0.0) for h in range(2)]
            pv = [_dot(vt[h * HEAD_DIM:(h + 1) * HEAD_DIM, :], a[h].astype(BF16)) for h in range(2)]
            for h in range(2):
                carry[h] = splice(carry[h], c[h][0:1, :] + oldest_row[n, h], lo, hi)
                acc[h] = splice(acc[h], acc[h][:, lo:hi] + pv[h], lo, hi)
        return carry[0], acc[0], carry[1], acc[1]

    zc = jnp.zeros((1, qb), F32)
    za = jnp.zeros((HEAD_DIM, qb), F32)
    newest = [(nq * i + c, c * kb, min(c + 2, nq) * kb, None) for c in reversed(range(nq))]

    st = lax.cond(i > 0, lambda: tiles(newest + [(nq * i - 1, 0, kb, True)], (zc, za, zc, za)),
                  lambda: tiles(newest, (zc, za, zc, za)))

    def strip_step(j, cols, s):
        ks = pl.ds(pl.multiple_of(j * kb, kb), kb)
        k, vt = k_ref[0, ks, :], vt_ref[0, :, ks]
        z = [_dot(k, qtm[h][:, cols]) for h in range(2)]
        sp = [jnp.maximum(z[h], 0.0) + jnp.log(1.0 + jnp.exp2(jnp.abs(z[h]) * -LOG2E)) for h in range(2)]
        c = [_dot(tri, sp[h].astype(BF16)) + s[2 * h] for h in range(2)]
        a = [jnp.exp((z[h] - sp[h]) - c[h]).astype(BF16) for h in range(2)]
        pv = [_dot(vt[h * HEAD_DIM:(h + 1) * HEAD_DIM, :], a[h]) for h in range(2)]
        return (c[0][0:1, :] + sp[0][0:1, :], s[1] + pv[0], c[1][0:1, :] + sp[1][0:1, :], s[3] + pv[1])

    out = []
    for n in range(nq):
        cols = slice(n * kb, (n + 1) * kb)
        zb = (zbound[0][:, cols], zbound[1][:, cols])

        def unfinished(s, zb=zb):
            return (jnp.min(jnp.minimum(s[0] - zb[0], s[2] - zb[1])) <= SB_EXIT).astype(jnp.int32)

        def body(s, n=n, cols=cols, unfinished=unfinished):
            new = strip_step(nq * i + n - s[0], cols, s[2:])
            return (s[0] + 1, unfinished(new)) + new

        s0 = tuple(x[:, cols] for x in st)
        out.append(lax.while_loop(lambda s, n=n: jnp.logical_and(s[0] <= nq * i + n, s[1] > 0), body,
                                  (jnp.int32(2), unfinished(s0)) + s0)[2:])
    acc = [jnp.concatenate([o[1 + 2 * h] for o in out], axis=1) for h in range(2)]
    o_ref[0] = jnp.concatenate(acc, axis=0).T.astype(o_ref.dtype)


def _sb_attention(row, tr, tri, kn):
    B, S, _ = row.shape
    qb = SB_QBLOCK
    return pl.pallas_call(
        _sb_kernel,
        grid=(B, SB_WIDTH // LANES, S // qb),
        in_specs=[
            pl.BlockSpec((1, LANES, qb), lambda b, p, i: (b, TR_SBQ + p, i)),
            pl.BlockSpec((1, S, LANES), lambda b, p, i: (b, 0, ROW_SBK + p)),
            pl.BlockSpec((1, LANES, S), lambda b, p, i: (b, TR_SBV + p, 0)),
            _const_spec((ATT_BLOCK, ATT_BLOCK)),
            pl.BlockSpec((1, 1, LANES), lambda b, p, i: (b, 0, 0)),
        ],
        out_specs=pl.BlockSpec((1, qb, LANES), lambda b, p, i: (b, i, p)),
        out_shape=jax.ShapeDtypeStruct((B, S, SB_WIDTH), BF16),
        compiler_params=_params("parallel", "parallel", "arbitrary"),
        name="sb_attention",
    )(tr, row, tr, tri, kn)


def _bucket_tiles(blk):
    r = np.arange(blk)[None, :] - np.arange(blk)[:, None]
    rel = np.stack([r, r + blk, r + 2 * blk]).astype(np.int64)
    n = np.maximum(rel, 0)
    nf = np.maximum(n, 1).astype(np.float32)
    large = MAX_EXACT + (np.log(nf / np.float32(MAX_EXACT)) / np.float32(math.log(MAX_DISTANCE / MAX_EXACT))
                         * np.float32(NUM_BUCKETS - MAX_EXACT)).astype(np.int32)
    large = np.minimum(large, NUM_BUCKETS - 1)
    idx = np.where(n < MAX_EXACT, n, large).astype(np.int32)
    assert (idx[2] == NUM_BUCKETS - 1).all() and blk >= MAX_DISTANCE
    return idx


def _bias_kernel(table_ref, idx_ref, o_ref):
    h = pl.program_id(0)
    idx = idx_ref[0]
    acc = jnp.zeros(idx.shape, F32)
    for b in range(NUM_BUCKETS):
        acc = jnp.where(idx == b, table_ref[b, h], acc)
    o_ref[0, 0] = acc


def _bias_tiles(rel_bias, blk):
    idx = jnp.asarray(_bucket_tiles(blk))
    return pl.pallas_call(
        _bias_kernel,
        grid=(DIFF_HEADS, 3),
        in_specs=[
            pl.BlockSpec(memory_space=pltpu.SMEM),
            pl.BlockSpec((1, blk, blk), lambda h, d: (d, 0, 0)),
        ],
        out_specs=pl.BlockSpec((1, 1, blk, blk), lambda h, d: (h, d, 0, 0)),
        out_shape=jax.ShapeDtypeStruct((DIFF_HEADS, 3, blk, blk), F32),
        name="bias_tiles",
    )(rel_bias.astype(F32), idx)


def _diff_kernel(qt_ref, k_ref, va_ref, bias_ref, kn_ref, lam_ref, subln_ref, o_ref, *, lam_init):
    kb, qb = ATT_BLOCK, DF_QBLOCK
    nq = qb // kb
    h, i = pl.program_id(1), pl.program_id(2)
    qt = qt_ref[0]
    sub = lax.broadcasted_iota(jnp.int32, qt.shape, 0)
    zero = jnp.zeros_like(qt)
    qtm = (jnp.where(sub < HEAD_DIM, qt, zero), jnp.where(sub >= HEAD_DIM, qt, zero))
    far = bias_ref[0, 2][0:1, 0:1]

    def scores(j, lo):
        k = k_ref[0, pl.ds(pl.multiple_of(j * kb, kb), kb), :]
        return [_dot(k, qtm[m][:, lo:]) for m in range(2)]

    def update(z, j, st, lo, bias, mask, online):
        vta = va_ref[0, :, pl.ds(pl.multiple_of(j * kb, kb), kb)]
        if bias is not None:
            z = [x + bias for x in z]
        if mask is not None:
            z = [jnp.where(mask, x, NEG_INF) for x in z]
        mnew = [st[m][0][:, lo:] for m in range(2)]
        scale = None
        if online:
            top = [jnp.max(x, axis=0, keepdims=True) for x in z]
            if bias is None:
                top = [t + far for t in top]
            mold, mnew = mnew, [jnp.maximum(mnew[m], top[m]) for m in range(2)]
            scale = [jnp.exp(mold[m] - mnew[m]) for m in range(2)]
        shift = [mn - far for mn in mnew] if bias is None else mnew
        p = [jnp.exp(z[m] - shift[m]) for m in range(2)]
        pv = [_dot(vta, p[m].astype(BF16)) for m in range(2)]
        out = []
        for m in range(2):
            l, acc = st[m][1][:, lo:], st[m][2][:, lo:]
            if online:
                l, acc = scale[m] * l, scale[m] * acc
            new = (mnew[m], l + pv[m][LANES:LANES + 1, :], acc + pv[m][:LANES, :])
            if lo:
                new = tuple(jnp.concatenate([old[:, :lo], x], axis=1) for old, x in zip(st[m], new))
            out.append(new)
        return tuple(out)

    def run(stab, online):
        zl, za = jnp.zeros((1, qb), F32), jnp.zeros((LANES, qb), F32)
        st = ((stab[0], zl, za), (stab[1], zl, za))
        for r in reversed(range(nq)):
            n = qb - r * kb
            bias = jnp.concatenate([bias_ref[0, min(c - r, 2)] for c in range(r, nq)], axis=1)
            causal = lax.broadcasted_iota(jnp.int32, (kb, n), 0) <= lax.broadcasted_iota(jnp.int32, (kb, n), 1)
            st = update(scores(nq * i + r, r * kb), nq * i + r, st, r * kb, bias, causal, online)
        def older(st):
            bias = jnp.concatenate([bias_ref[0, min(c + 1, 2)] for c in range(nq)], axis=1)
            st = update(scores(nq * i - 1, 0), nq * i - 1, st, 0, bias, None, online)
            for d in range(2, DF_UNROLL + 1):
                st = update(scores(nq * i - d, 0), nq * i - d, st, 0, None, None, online)

            def body(jj, s):
                for d in range(DF_UNROLL):
                    j = nq * i - DF_UNROLL * (jj + 1) - 1 - d
                    s = update(scores(j, 0), j, s, 0, None, None, online)
                return s

            return lax.fori_loop(0, (nq * i) // DF_UNROLL - 1, body, st)

        return lax.cond(i > 0, older, lambda s: s, st)

    lane = lax.broadcasted_iota(jnp.int32, (1, LANES), 1)
    kn = kn_ref[0]
    allb = bias_ref[0]
    bmax = jnp.max(jnp.max(allb, axis=0), axis=(0, 1), keepdims=True)
    bmin = jnp.min(jnp.min(allb, axis=0), axis=(0, 1), keepdims=True)
    qk = []
    for m in range(2):
        knm = jnp.max(jnp.where(lane == SB_WIDTH // HEAD_DIM + 2 * h + m, kn, 0.0), axis=1, keepdims=True)
        qf = qtm[m].astype(F32)
        qk.append(jnp.sqrt(jnp.sum(qf * qf, axis=0, keepdims=True)) * (knm * NORM_SLACK))
    spread = 2.0 * jnp.maximum(qk[0], qk[1]) + (bmax - bmin)
    bound = [x + bmax for x in qk]
    neg = jnp.full((1, qb), NEG_INF, F32)
    st = lax.cond(jnp.max(spread) <= DF_SPREAD,
                  lambda: run(bound, False), lambda: run([neg, neg], True))

    lp = lam_ref[...]
    lam = (jnp.exp(jnp.sum(lp[0:1] * lp[1:2], axis=1, keepdims=True))
           - jnp.exp(jnp.sum(lp[2:3] * lp[3:4], axis=1, keepdims=True)) + lam_init)
    (_, l0, acc0), (_, l1, acc1) = st
    o = (acc0 / l0 - lam * (acc1 / l1)).T
    o_ref[0] = (_rmsnorm(o, subln_ref[...]) * (1.0 - lam_init)).astype(o_ref.dtype)


def _diff_attention(row, tr, va, bias, kn, lam_p, subln, lam_init):
    B, S, _ = row.shape
    kb, qb = ATT_BLOCK, DF_QBLOCK
    return pl.pallas_call(
        functools.partial(_diff_kernel, lam_init=lam_init),
        grid=(B, DIFF_HEADS, S // qb),
        in_specs=[
            pl.BlockSpec((1, LANES, qb), lambda b, h, i: (b, TR_DFQ + h, i)),
            pl.BlockSpec((1, S, LANES), lambda b, h, i: (b, 0, ROW_DFK + h)),
            pl.BlockSpec((1, VA_ROWS, S), lambda b, h, i: (b, h, 0)),
            pl.BlockSpec((1, 3, kb, kb), lambda b, h, i: (h, 0, 0, 0)),
            pl.BlockSpec((1, 1, LANES), lambda b, h, i: (b, 0, 0)),
            _const_spec((4, HEAD_DIM)),
            _const_spec((1, 2 * HEAD_DIM)),
        ],
        out_specs=pl.BlockSpec((1, qb, LANES), lambda b, h, i: (b, i, h)),
        out_shape=jax.ShapeDtypeStruct((B, S, DIFF_WIDTH), BF16),
        compiler_params=_params("parallel", "parallel", "arbitrary"),
        name="diff_attention",
    )(tr, row, va, bias, kn, lam_p, subln)


def _conv_kernel(u_ref, uprev_ref, cw_ref, cb_ref, lg_ref, lb_ref, o_ref, h_ref, hs_ref, *, tc):
    i = pl.program_id(1)

    def glu(u):
        u = u.astype(F32)
        return u[:, :CONV_CH] * jax.nn.sigmoid(u[:, CONV_CH:])

    h_ref[0:CONV_HALO, :] = jnp.where(i == 0, 0.0, glu(uprev_ref[0]))
    h_ref[CONV_HALO:CONV_HALO + tc, :] = glu(u_ref[0])
    for b in range(1, SUBLANES):
        hs_ref[b - 1] = h_ref[b:b + tc + CONV_HALO - SUBLANES, :]
    first = CONV_HALO - (CONV_WIDTH - 1)
    for c in range(tc // CONV_ROWS):
        acc = jnp.zeros((CONV_ROWS, CONV_CH), F32)
        for w in range(CONV_WIDTH):
            a, b = divmod(first + w, SUBLANES)
            r0 = c * CONV_ROWS + a * SUBLANES
            src = h_ref[r0:r0 + CONV_ROWS, :] if b == 0 else hs_ref[b - 1, r0:r0 + CONV_ROWS, :]
            acc = acc + src * cw_ref[w:w + 1, :]
        y = acc + cb_ref[...]
        mu = jnp.mean(y, axis=-1, keepdims=True)
        var = jnp.mean(jnp.square(y - mu), axis=-1, keepdims=True)
        y = (y - mu) * lax.rsqrt(var + EPS) * lg_ref[...] + lb_ref[...]
        o_ref[0, c * CONV_ROWS:(c + 1) * CONV_ROWS, :] = (y * jax.nn.sigmoid(y)).astype(o_ref.dtype)


def _conv(row, cw, cb, lg, lb, tc):
    B, S, _ = row.shape
    ucol = 0
    per = tc // CONV_HALO
    return pl.pallas_call(
        functools.partial(_conv_kernel, tc=tc),
        grid=(B, S // tc),
        in_specs=[
            pl.BlockSpec((1, tc, 2 * CONV_CH), lambda b, i: (b, i, ucol)),
            pl.BlockSpec((1, CONV_HALO, 2 * CONV_CH), lambda b, i: (b, jnp.maximum(i * per - 1, 0), ucol)),
            _const_spec((CONV_WIDTH, CONV_CH)),
            _const_spec((1, CONV_CH)),
            _const_spec((1, CONV_CH)),
            _const_spec((1, CONV_CH)),
        ],
        out_specs=pl.BlockSpec((1, tc, CONV_CH), lambda b, i: (b, i, 0)),
        out_shape=jax.ShapeDtypeStruct((B, S, CONV_CH), BF16),
        scratch_shapes=[pltpu.VMEM((CONV_HALO + tc, CONV_CH), F32),
                        pltpu.VMEM((SUBLANES - 1, CONV_HALO + tc - SUBLANES, CONV_CH), F32)],
        compiler_params=_params("parallel", "parallel"),
        name="conformer_conv",
    )(row, row, cw, cb, lg, lb)


def _merge_kernel(x_ref, g_ref, wg_ref, osb_ref, odf_ref, ocv_ref, wsb_ref, wdf_ref, wcv_ref, wout_ref, o_ref):
    x = x_ref[...]
    D = x.shape[1]
    xb = _rmsnorm(x, g_ref[...]).astype(BF16)
    y = None
    for n, (o_br, w_br) in enumerate(((osb_ref, wsb_ref), (odf_ref, wdf_ref), (ocv_ref, wcv_ref))):
        gate = jax.nn.sigmoid(_dot(xb, wg_ref[:, n * D:(n + 1) * D]))
        t = gate * _dot(o_br[...], w_br[...])
        y = t if y is None else y + t
    o_ref[...] = x + _dot(y.astype(BF16), wout_ref[...])


def _merge(x2, g, wg, osb, odf, ocv, wsb, wdf, wcv, wout, tm):
    N, D = x2.shape
    W = osb.shape[1]
    tok = lambda w: pl.BlockSpec((tm, w), lambda i: (i, 0))
    return pl.pallas_call(
        _merge_kernel,
        grid=(N // tm,),
        in_specs=[tok(D), _const_spec((1, D)), _const_spec((D, 3 * D)), tok(W), tok(W), tok(W),
                  _const_spec((W, D)), _const_spec((W, D)), _const_spec((W, D)), _const_spec((D, D))],
        out_specs=tok(D),
        out_shape=jax.ShapeDtypeStruct((N, D), F32),
        compiler_params=_params("parallel"),
        name="gated_merge",
    )(x2, g, wg, osb, odf, ocv, wsb, wdf, wcv, wout)


def _memkv_kernel(mem_ref, g_ref, wkv_ref, o_ref):
    mb = _rmsnorm(mem_ref[0], g_ref[...]).astype(BF16)
    o_ref[0] = _dot(mb, wkv_ref[...]).astype(o_ref.dtype)


def _memkv(mem, g, wkv):
    B, M, D = mem.shape
    return pl.pallas_call(
        _memkv_kernel,
        grid=(B,),
        in_specs=[pl.BlockSpec((1, M, D), lambda b: (b, 0, 0)), _const_spec((1, D)), _const_spec((D, 2 * XA_WIDTH))],
        out_specs=pl.BlockSpec((1, M, 2 * XA_WIDTH), lambda b: (b, 0, 0)),
        out_shape=jax.ShapeDtypeStruct((B, M, 2 * XA_WIDTH), BF16),
        compiler_params=_params("parallel"),
        name="mem_kv",
    )(mem, g, wkv)


def _xattn_kernel(x_ref, g_ref, wq_ref, kv_ref, wo_ref, o_ref):
    x = x_ref[0]
    xb = _rmsnorm(x, g_ref[...]).astype(BF16)
    q = _dot(xb, wq_ref[...]).astype(BF16)
    heads = []
    for h in range(XA_HEADS):
        hs = slice(h * XA_HEAD_DIM, (h + 1) * XA_HEAD_DIM)
        k = kv_ref[0, :, hs]
        v = kv_ref[0, :, XA_WIDTH + h * XA_HEAD_DIM:XA_WIDTH + (h + 1) * XA_HEAD_DIM]
        s = _dot_nt(q[:, hs], k) * (XA_HEAD_DIM ** -0.5)
        p = jnp.exp(s - jnp.max(s, axis=1, keepdims=True))
        p = p / jnp.sum(p, axis=1, keepdims=True)
        heads.append(_dot(p.astype(BF16), v).astype(BF16))
    o = jnp.concatenate(heads, axis=1)
    o_ref[0] = x + _dot(o, wo_ref[...])


def _xattn(x, g, wq, kv, wo, tm):
    B, S, D = x.shape
    M = kv.shape[1]
    return pl.pallas_call(
        _xattn_kernel,
        grid=(B, S // tm),
        in_specs=[
            pl.BlockSpec((1, tm, D), lambda b, i: (b, i, 0)),
            _const_spec((1, D)),
            _const_spec((D, XA_WIDTH)),
            pl.BlockSpec((1, M, 2 * XA_WIDTH), lambda b, i: (b, 0, 0)),
            _const_spec((XA_WIDTH, D)),
        ],
        out_specs=pl.BlockSpec((1, tm, D), lambda b, i: (b, i, 0)),
        out_shape=jax.ShapeDtypeStruct((B, S, D), F32),
        compiler_params=_params("parallel", "parallel"),
        name="cross_attention",
    )(x, g, wq, kv, wo)


def _mlp_kernel(x_ref, g_ref, wup_ref, wdown_ref, gf_ref, o_ref, *, fc, final):
    x = x_ref[...]
    xb = _rmsnorm(x, g_ref[...]).astype(BF16)
    acc = x
    for c in range(wup_ref.shape[1] // fc):
        cs = slice(c * fc, (c + 1) * fc)
        h = jnp.square(jnp.maximum(_dot(xb, wup_ref[:, cs]), 0.0)).astype(BF16)
        acc = acc + _dot(h, wdown_ref[cs, :])
    o_ref[...] = _rmsnorm(acc, gf_ref[...]) if final else acc


def _mlp(x2, g, wup, wdown, gf, tm, final):
    N, D = x2.shape
    F = wup.shape[1]
    return pl.pallas_call(
        functools.partial(_mlp_kernel, fc=MLP_CHUNK, final=final),
        grid=(N // tm,),
        in_specs=[pl.BlockSpec((tm, D), lambda i: (i, 0)), _const_spec((1, D)),
                  _const_spec((D, F)), _const_spec((F, D)), _const_spec((1, D))],
        out_specs=pl.BlockSpec((tm, D), lambda i: (i, 0)),
        out_shape=jax.ShapeDtypeStruct((N, D), F32),
        compiler_params=_params("parallel"),
        name="sqrelu_mlp",
    )(x2, g, wup, wdown, gf)


def kernel(x, mem, rel_bias, ln_mix, w_in, diff_lambda, diff_subln, conv_w, conv_b, conv_ln_g, conv_ln_b,
           w_sb_proj, w_diff_proj, w_conv_proj, w_out, ln_xattn, ln_mem, xa_w_q, xa_w_kv, xa_w_o,
           ln_mlp, w_up, w_down, ln_final):
    B, S, D = x.shape
    depth = w_in.shape[0]
    blk = ATT_BLOCK
    tm = min(TOKEN_TILE, S)
    tc = min(CONV_TILE, S)
    assert S % SB_QBLOCK == 0 and S % DF_QBLOCK == 0 and S % tm == 0 and S % tc == 0
    assert w_in.shape[1:] == (D, 3 * SB_WIDTH + 3 * DIFF_WIDTH + 2 * CONV_CH + 3 * D)

    row2 = lambda a: a.reshape(1, -1).astype(F32)
    tri = jnp.asarray(np.triu(np.ones((blk, blk), np.float32), k=1), BF16)
    bias = _bias_tiles(rel_bias, blk)
    scale = HEAD_DIM ** -0.5

    for l in range(depth):
        w = w_in[l]
        edges = np.cumsum([0] + 3 * [SB_WIDTH] + 3 * [DIFF_WIDTH] + [2 * CONV_CH])
        sbq, sbk, sbv, dfq, dfk, dfv, glu = (w[:, edges[n]:edges[n + 1]] for n in range(7))
        wrow = jnp.concatenate([glu, sbk, dfk], axis=1).astype(BF16)
        wtr = jnp.concatenate([sbq * scale, sbv, dfq * scale, dfv], axis=1).T.astype(BF16)
        wg = w[:, edges[7]:].astype(BF16)

        row, tr, va = _inproj(x, row2(ln_mix[l]), wrow, wtr, tm)
        kn = _key_norm_max(row)
        o_sb = _sb_attention(row, tr, tri, kn)
        lam_init = 0.8 - 0.6 * math.exp(-0.3 * l)
        o_df = _diff_attention(row, tr, va, bias, kn, diff_lambda[l].astype(F32), row2(diff_subln[l]), lam_init)
        o_cv = _conv(row, conv_w[l].reshape(CONV_WIDTH, CONV_CH).astype(F32), row2(conv_b[l]),
                     row2(conv_ln_g[l]), row2(conv_ln_b[l]), tc)

        N = B * S
        x2 = _merge(x.reshape(N, D), row2(ln_mix[l]), wg,
                    o_sb.reshape(N, -1), o_df.reshape(N, -1), o_cv.reshape(N, -1),
                    w_sb_proj[l].astype(BF16), w_diff_proj[l].astype(BF16), w_conv_proj[l].astype(BF16),
                    w_out[l].astype(BF16), tm)

        kv = _memkv(mem, row2(ln_mem[l]), xa_w_kv[l].astype(BF16))
        x3 = _xattn(x2.reshape(B, S, D), row2(ln_xattn[l]),
                    xa_w_q[l].astype(BF16), kv, xa_w_o[l].astype(BF16), tm)

        x = _mlp(x3.reshape(N, D), row2(ln_mlp[l]), w_up[l].astype(BF16), w_down[l].astype(BF16),
                 row2(ln_final), tm, final=(l == depth - 1)).reshape(B, S, D)
    return x
```

```python
import functools
import math

import numpy as np
import jax
import jax.numpy as jnp
from jax import lax
from jax.experimental import pallas as pl
from jax.experimental.pallas import tpu as pltpu

F32 = jnp.float32
BF16 = jnp.bfloat16

HEAD_DIM = 64
SB_WIDTH = 512
DIFF_WIDTH = 512
DIFF_HEADS = 4
CONV_CH = 512
CONV_WIDTH = 31
XA_HEADS = 4
XA_HEAD_DIM = 128
XA_WIDTH = XA_HEADS * XA_HEAD_DIM
NUM_BUCKETS = 32
MAX_EXACT = NUM_BUCKETS // 2
MAX_DISTANCE = 128
EPS = 1e-6
NEG_INF = -1e30

LANES = 128
SUBLANES = 8
ATT_BLOCK = 256
CONV_HALO = 32
CONV_ROWS = 64
VMEM_LIMIT = 56 * 1024 * 1024
TOKEN_TILE = 512
CONV_TILE = 256
PROJ_CHUNK = 512
MLP_CHUNK = 1024
NORM_SLACK = 1.01

SB_QBLOCK = 2048
DF_QBLOCK = 2048
SB_EXIT = 110.0
DF_UNROLL = 4
DF_SPREAD = 80.0
LOG2E = 1.4426950408889634
ONES_ROWS = 16

ROW_COLS = 2 * CONV_CH + SB_WIDTH + DIFF_WIDTH
ROW_SBK, ROW_DFK = 8, 12
TR_ROWS = 2 * SB_WIDTH + DIFF_WIDTH
TR_SBQ, TR_SBV, TR_DFQ = 0, 4, 8
VA_ROWS = LANES + ONES_ROWS


def _params(*sem):
    return pltpu.CompilerParams(dimension_semantics=sem, vmem_limit_bytes=VMEM_LIMIT)


def _const_spec(shape):
    nd = len(shape)
    return pl.BlockSpec(shape, lambda *_: (0,) * nd)


def _rmsnorm(x, g):
    return x * lax.rsqrt(jnp.mean(x * x, axis=-1, keepdims=True) + EPS) * g


def _dot(a, b):
    return jnp.dot(a, b, preferred_element_type=F32)


def _dot_nt(a, b):
    return lax.dot_general(a, b, (((1,), (1,)), ((), ())), preferred_element_type=F32)


def _inproj_kernel(x_ref, g_ref, wrow_ref, wtr_ref, row_ref, tr_ref, va_ref):
    xb = _rmsnorm(x_ref[0], g_ref[...]).astype(BF16)
    tm = xb.shape[0]
    for c in range(ROW_COLS // PROJ_CHUNK):
        cs = slice(c * PROJ_CHUNK, (c + 1) * PROJ_CHUNK)
        row_ref[0, :, cs] = _dot(xb, wrow_ref[:, cs]).astype(BF16)
    for c in range(TR_ROWS // PROJ_CHUNK):
        cs = slice(c * PROJ_CHUNK, (c + 1) * PROJ_CHUNK)
        tr_ref[0, cs, :] = _dot_nt(wtr_ref[cs, :], xb).astype(BF16)
    dfv = _dot_nt(wtr_ref[TR_ROWS:, :], xb).astype(BF16)
    for h in range(DIFF_HEADS):
        va_ref[0, h * VA_ROWS:h * VA_ROWS + LANES, :] = dfv[h * LANES:(h + 1) * LANES, :]
        va_ref[0, h * VA_ROWS + LANES:(h + 1) * VA_ROWS, :] = jnp.ones((ONES_ROWS, tm), BF16)


def _inproj(x, g, wrow, wtr, tm):
    B, S, D = x.shape
    return pl.pallas_call(
        _inproj_kernel,
        grid=(B, S // tm),
        in_specs=[
            pl.BlockSpec((1, tm, D), lambda b, i: (b, i, 0)),
            _const_spec((1, D)),
            _const_spec((D, ROW_COLS)),
            _const_spec((TR_ROWS + DIFF_WIDTH, D)),
        ],
        out_specs=[
            pl.BlockSpec((1, tm, ROW_COLS), lambda b, i: (b, i, 0)),
            pl.BlockSpec((1, TR_ROWS, tm), lambda b, i: (b, 0, i)),
            pl.BlockSpec((1, DIFF_HEADS * VA_ROWS, tm), lambda b, i: (b, 0, i)),
        ],
        out_shape=[
            jax.ShapeDtypeStruct((B, S, ROW_COLS), BF16),
            jax.ShapeDtypeStruct((B, TR_ROWS, S), BF16),
            jax.ShapeDtypeStruct((B, DIFF_HEADS * VA_ROWS, S), BF16),
        ],
        compiler_params=_params("parallel", "parallel"),
        name="inproj",
    )(x, g, wrow, wtr)


def _knorm_kernel(k_ref, sel_ref, o_ref):
    k = k_ref[0].astype(F32)
    n2 = _dot((k * k).astype(BF16), sel_ref[...])
    o_ref[0] = jnp.sqrt(jnp.max(n2, axis=0, keepdims=True))


def _key_norm_max(row):
    B, S, _ = row.shape
    width = SB_WIDTH + DIFF_WIDTH
    sel = np.zeros((width, LANES), np.float32)
    sel[np.arange(width), np.arange(width) // HEAD_DIM] = 1.0
    return pl.pallas_call(
        _knorm_kernel,
        grid=(B,),
        in_specs=[pl.BlockSpec((1, S, width), lambda b: (b, 0, ROW_SBK * LANES // width)),
                  _const_spec((width, LANES))],
        out_specs=pl.BlockSpec((1, 1, LANES), lambda b: (b, 0, 0)),
        out_shape=jax.ShapeDtypeStruct((B, 1, LANES), F32),
        compiler_params=_params("parallel"),
        name="key_norm_max",
    )(row, jnp.asarray(sel, BF16))


def _sb_kernel(qt_ref, k_ref, vt_ref, tri_ref, kn_ref, o_ref):
    kb, qb = ATT_BLOCK, SB_QBLOCK
    nq = qb // kb
    p, i = pl.program_id(1), pl.program_id(2)
    qt = qt_ref[0]
    sub = lax.broadcasted_iota(jnp.int32, qt.shape, 0)
    zero = jnp.zeros_like(qt)
    qtm = (jnp.where(sub < HEAD_DIM, qt, zero), jnp.where(sub >= HEAD_DIM, qt, zero))
    tri = tri_ref[...]

    lane = lax.broadcasted_iota(jnp.int32, (1, LANES), 1)
    kn = kn_ref[0]
    zbound = []
    for h in range(2):
        knh = jnp.max(jnp.where(lane == 2 * p + h, kn, 0.0), axis=1, keepdims=True)
        qf = qtm[h].astype(F32)
        zbound.append(jnp.sqrt(jnp.sum(qf * qf, axis=0, keepdims=True)) * (knh * NORM_SLACK))

    def splice(old, new, lo, hi):
        parts = ([old[:, :lo]] if lo else []) + [new] + ([old[:, hi:]] if hi < qb else [])
        return parts[0] if len(parts) == 1 else jnp.concatenate(parts, axis=1)

    def tiles(jobs, st):
        carry, acc = [st[0], st[2]], [st[1], st[3]]
        ks = [pl.ds(pl.multiple_of(j * kb, kb), kb) for j, _, _, _ in jobs]
        units = [(n, h) for n in range(len(jobs)) for h in range(2)]
        z = {(n, h): _dot(k_ref[0, ks[n], :], qtm[h][:, jobs[n][1]:jobs[n][2]]) for n, h in units}
        sp = {u: jnp.maximum(z[u], 0.0) + jnp.log(1.0 + jnp.exp2(jnp.abs(z[u]) * -LOG2E)) for u in units}
        own = {u: z[u] - sp[u] for u in units}
        keep = []
        for n, (_, lo, hi, valid) in enumerate(jobs):
            m = (lax.broadcasted_iota(jnp.int32, (kb, hi - lo), 0)
                 < lax.broadcasted_iota(jnp.int32, (kb, hi - lo), 1)) if valid is None else valid
            keep.append(m)
            for h in range(2):
                sp[n, h] = jnp.where(m, sp[n, h], 0.0)
        cs = {u: _dot(tri, sp[u].astype(BF16)) for u in units}
        oldest_row = {u: sp[u][0:1, :] for u in units}
        for n, (_, lo, hi, _) in enumerate(jobs):
            vt = vt_ref[0, :, ks[n]]
            c = [cs[n, h] + carry[h][:, lo:hi] for h in range(2)]
            a = [jnp.where(keep[n], jnp.exp(own[n, h] - c[h]), 0.0) for h in range(2)]
            pv = [_dot(vt[h * HEAD_DIM:(h + 1) * HEAD_DIM, :], a[h].astype(BF16)) for h in range(2)]
            for h in range(2):
                carry[h] = splice(carry[h], c[h][0:1, :] + oldest_row[n, h], lo, hi)
                acc[h] = splice(acc[h], acc[h][:, lo:hi] + pv[h], lo, hi)
        return carry[0], acc[0], carry[1], acc[1]

    zc = jnp.zeros((1, qb), F32)
    za = jnp.zeros((HEAD_DIM, qb), F32)
    newest = [(nq * i + c, c * kb, min(c + 2, nq) * kb, None) for c in reversed(range(nq))]

    st = lax.cond(i > 0, lambda: tiles(newest + [(nq * i - 1, 0, kb, True)], (zc, za, zc, za)),
                  lambda: tiles(newest, (zc, za, zc, za)))

    def strip_step(j, cols, s):
        ks = pl.ds(pl.multiple_of(j * kb, kb), kb)
        k, vt = k_ref[0, ks, :], vt_ref[0, :, ks]
        z = [_dot(k, qtm[h][:, cols]) for h in range(2)]
        sp = [jnp.maximum(z[h], 0.0) + jnp.log(1.0 + jnp.exp2(jnp.abs(z[h]) * -LOG2E)) for h in range(2)]
        c = [_dot(tri, sp[h].astype(BF16)) + s[2 * h] for h in range(2)]
        a = [jnp.exp((z[h] - sp[h]) - c[h]).astype(BF16) for h in range(2)]
        pv = [_dot(vt[h * HEAD_DIM:(h + 1) * HEAD_DIM, :], a[h]) for h in range(2)]
        return (c[0][0:1, :] + sp[0][0:1, :], s[1] + pv[0], c[1][0:1, :] + sp[1][0:1, :], s[3] + pv[1])

    def unfinished(s, n):
        cols = slice(n * kb, (n + 1) * kb)
        slack = jnp.minimum(s[0] - zbound[0][:, cols], s[2] - zbound[1][:, cols])
        return (jnp.min(slack) <= SB_EXIT).astype(jnp.int32)

    pending = sum(unfinished(tuple(x[:, n * kb:(n + 1) * kb] for x in st), n) for n in range(nq))
    joint = jnp.logical_and(pending >= nq // 2, nq * i + nq > 2)

    def second(s):
        jobs = lambda first: [(nq * i + c - 2, c * kb, (c + 1) * kb, True) for c in reversed(range(first, nq))]
        return lax.cond(i > 0, lambda: tiles(jobs(0), s), lambda: tiles(jobs(2), s))

    st = lax.cond(joint, second, lambda s: s, st)
    depth0 = 2 + joint.astype(jnp.int32)

    out = []
    for n in range(nq):
        cols = slice(n * kb, (n + 1) * kb)

        def body(s, n=n, cols=cols):
            new = strip_step(nq * i + n - s[0], cols, s[2:])
            return (s[0] + 1, unfinished(new, n)) + new

        s0 = tuple(x[:, cols] for x in st)
        out.append(lax.while_loop(lambda s, n=n: jnp.logical_and(s[0] <= nq * i + n, s[1] > 0), body,
                                  (depth0, unfinished(s0, n)) + s0)[2:])
    acc = [jnp.concatenate([o[1 + 2 * h] for o in out], axis=1) for h in range(2)]
    o_ref[0] = jnp.concatenate(acc, axis=0).T.astype(o_ref.dtype)


def _sb_attention(row, tr, tri, kn):
    B, S, _ = row.shape
    qb = SB_QBLOCK
    return pl.pallas_call(
        _sb_kernel,
        grid=(B, SB_WIDTH // LANES, S // qb),
        in_specs=[
            pl.BlockSpec((1, LANES, qb), lambda b, p, i: (b, TR_SBQ + p, i)),
            pl.BlockSpec((1, S, LANES), lambda b, p, i: (b, 0, ROW_SBK + p)),
            pl.BlockSpec((1, LANES, S), lambda b, p, i: (b, TR_SBV + p, 0)),
            _const_spec((ATT_BLOCK, ATT_BLOCK)),
            pl.BlockSpec((1, 1, LANES), lambda b, p, i: (b, 0, 0)),
        ],
        out_specs=pl.BlockSpec((1, qb, LANES), lambda b, p, i: (b, i, p)),
        out_shape=jax.ShapeDtypeStruct((B, S, SB_WIDTH), BF16),
        compiler_params=_params("parallel", "parallel", "arbitrary"),
        name="sb_attention",
    )(tr, row, tr, tri, kn)


def _bucket_tiles(blk):
    r = np.arange(blk)[None, :] - np.arange(blk)[:, None]
    rel = np.stack([r, r + blk, r + 2 * blk]).astype(np.int64)
    n = np.maximum(rel, 0)
    nf = np.maximum(n, 1).astype(np.float32)
    large = MAX_EXACT + (np.log(nf / np.float32(MAX_EXACT)) / np.float32(math.log(MAX_DISTANCE / MAX_EXACT))
                         * np.float32(NUM_BUCKETS - MAX_EXACT)).astype(np.int32)
    large = np.minimum(large, NUM_BUCKETS - 1)
    idx = np.where(n < MAX_EXACT, n, large).astype(np.int32)
    assert (idx[2] == NUM_BUCKETS - 1).all() and blk >= MAX_DISTANCE
    return idx


def _bias_kernel(table_ref, idx_ref, o_ref):
    h = pl.program_id(0)
    idx = idx_ref[0]
    acc = jnp.zeros(idx.shape, F32)
    for b in range(NUM_BUCKETS):
        acc = jnp.where(idx == b, table_ref[b, h], acc)
    o_ref[0, 0] = acc


def _bias_tiles(rel_bias, blk):
    idx = jnp.asarray(_bucket_tiles(blk))
    return pl.pallas_call(
        _bias_kernel,
        grid=(DIFF_HEADS, 3),
        in_specs=[
            pl.BlockSpec(memory_space=pltpu.SMEM),
            pl.BlockSpec((1, blk, blk), lambda h, d: (d, 0, 0)),
        ],
        out_specs=pl.BlockSpec((1, 1, blk, blk), lambda h, d: (h, d, 0, 0)),
        out_shape=jax.ShapeDtypeStruct((DIFF_HEADS, 3, blk, blk), F32),
        name="bias_tiles",
    )(rel_bias.astype(F32), idx)


def _diff_kernel(qt_ref, k_ref, va_ref, bias_ref, kn_ref, lam_ref, subln_ref, o_ref, *, lam_init):
    kb, qb = ATT_BLOCK, DF_QBLOCK
    nq = qb // kb
    h, i = pl.program_id(1), pl.program_id(2)
    qt = qt_ref[0]
    sub = lax.broadcasted_iota(jnp.int32, qt.shape, 0)
    zero = jnp.zeros_like(qt)
    qtm = (jnp.where(sub < HEAD_DIM, qt, zero), jnp.where(sub >= HEAD_DIM, qt, zero))
    far = bias_ref[0, 2][0:1, 0:1]

    def scores(j, lo):
        k = k_ref[0, pl.ds(pl.multiple_of(j * kb, kb), kb), :]
        return [_dot(k, qtm[m][:, lo:]) for m in range(2)]

    def update(z, j, st, lo, bias, mask, online):
        vta = va_ref[0, :, pl.ds(pl.multiple_of(j * kb, kb), kb)]
        if bias is not None:
            z = [x + bias for x in z]
        if mask is not None:
            z = [jnp.where(mask, x, NEG_INF) for x in z]
        mnew = [st[m][0][:, lo:] for m in range(2)]
        scale = None
        if online:
            top = [jnp.max(x, axis=0, keepdims=True) for x in z]
            if bias is None:
                top = [t + far for t in top]
            mold, mnew = mnew, [jnp.maximum(mnew[m], top[m]) for m in range(2)]
            scale = [jnp.exp(mold[m] - mnew[m]) for m in range(2)]
        shift = [mn - far for mn in mnew] if bias is None else mnew
        p = [jnp.exp(z[m] - shift[m]) for m in range(2)]
        pv = [_dot(vta, p[m].astype(BF16)) for m in range(2)]
        out = []
        for m in range(2):
            l, acc = st[m][1][:, lo:], st[m][2][:, lo:]
            if online:
                l, acc = scale[m] * l, scale[m] * acc
            new = (mnew[m], l + pv[m][LANES:LANES + 1, :], acc + pv[m][:LANES, :])
            if lo:
                new = tuple(jnp.concatenate([old[:, :lo], x], axis=1) for old, x in zip(st[m], new))
            out.append(new)
        return tuple(out)

    def run(stab, online):
        zl, za = jnp.zeros((1, qb), F32), jnp.zeros((LANES, qb), F32)
        st = ((stab[0], zl, za), (stab[1], zl, za))
        for r in reversed(range(nq)):
            n = qb - r * kb
            bias = jnp.concatenate([bias_ref[0, min(c - r, 2)] for c in range(r, nq)], axis=1)
            causal = lax.broadcasted_iota(jnp.int32, (kb, n), 0) <= lax.broadcasted_iota(jnp.int32, (kb, n), 1)
            st = update(scores(nq * i + r, r * kb), nq * i + r, st, r * kb, bias, causal, online)
        def older(st):
            bias = jnp.concatenate([bias_ref[0, min(c + 1, 2)] for c in range(nq)], axis=1)
            st = update(scores(nq * i - 1, 0), nq * i - 1, st, 0, bias, None, online)
            for d in range(2, DF_UNROLL + 1):
                st = update(scores(nq * i - d, 0), nq * i - d, st, 0, None, None, online)

            def body(jj, s):
                for d in range(DF_UNROLL):
                    j = nq * i - DF_UNROLL * (jj + 1) - 1 - d
                    s = update(scores(j, 0), j, s, 0, None, None, online)
                return s

            return lax.fori_loop(0, (nq * i) // DF_UNROLL - 1, body, st)

        return lax.cond(i > 0, older, lambda s: s, st)

    lane = lax.broadcasted_iota(jnp.int32, (1, LANES), 1)
    kn = kn_ref[0]
    allb = bias_ref[0]
    bmax = jnp.max(jnp.max(allb, axis=0), axis=(0, 1), keepdims=True)
    bmin = jnp.min(jnp.min(allb, axis=0), axis=(0, 1), keepdims=True)
    qk = []
    for m in range(2):
        knm = jnp.max(jnp.where(lane == SB_WIDTH // HEAD_DIM + 2 * h + m, kn, 0.0), axis=1, keepdims=True)
        qf = qtm[m].astype(F32)
        qk.append(jnp.sqrt(jnp.sum(qf * qf, axis=0, keepdims=True)) * (knm * NORM_SLACK))
    spread = 2.0 * jnp.maximum(qk[0], qk[1]) + (bmax - bmin)
    bound = [x + bmax for x in qk]
    neg = jnp.full((1, qb), NEG_INF, F32)
    st = lax.cond(jnp.max(spread) <= DF_SPREAD,
                  lambda: run(bound, False), lambda: run([neg, neg], True))

    lp = lam_ref[...]
    lam = (jnp.exp(jnp.sum(lp[0:1] * lp[1:2], axis=1, keepdims=True))
           - jnp.exp(jnp.sum(lp[2:3] * lp[3:4], axis=1, keepdims=True)) + lam_init)
    (_, l0, acc0), (_, l1, acc1) = st
    o = (acc0 / l0 - lam * (acc1 / l1)).T
    o_ref[0] = (_rmsnorm(o, subln_ref[...]) * (1.0 - lam_init)).astype(o_ref.dtype)


def _diff_attention(row, tr, va, bias, kn, lam_p, subln, lam_init):
    B, S, _ = row.shape
    kb, qb = ATT_BLOCK, DF_QBLOCK
    return pl.pallas_call(
        functools.partial(_diff_kernel, lam_init=lam_init),
        grid=(B, DIFF_HEADS, S // qb),
        in_specs=[
            pl.BlockSpec((1, LANES, qb), lambda b, h, i: (b, TR_DFQ + h, i)),
            pl.BlockSpec((1, S, LANES), lambda b, h, i: (b, 0, ROW_DFK + h)),
            pl.BlockSpec((1, VA_ROWS, S), lambda b, h, i: (b, h, 0)),
            pl.BlockSpec((1, 3, kb, kb), lambda b, h, i: (h, 0, 0, 0)),
            pl.BlockSpec((1, 1, LANES), lambda b, h, i: (b, 0, 0)),
            _const_spec((4, HEAD_DIM)),
            _const_spec((1, 2 * HEAD_DIM)),
        ],
        out_specs=pl.BlockSpec((1, qb, LANES), lambda b, h, i: (b, i, h)),
        out_shape=jax.ShapeDtypeStruct((B, S, DIFF_WIDTH), BF16),
        compiler_params=_params("parallel", "parallel", "arbitrary"),
        name="diff_attention",
    )(tr, row, va, bias, kn, lam_p, subln)


def _conv_kernel(u_ref, uprev_ref, cw_ref, cb_ref, lg_ref, lb_ref, o_ref, h_ref, hs_ref, *, tc):
    i = pl.program_id(1)

    def glu(u):
        u = u.astype(F32)
        return u[:, :CONV_CH] * jax.nn.sigmoid(u[:, CONV_CH:])

    h_ref[0:CONV_HALO, :] = jnp.where(i == 0, 0.0, glu(uprev_ref[0]))
    h_ref[CONV_HALO:CONV_HALO + tc, :] = glu(u_ref[0])
    for b in range(1, SUBLANES):
        hs_ref[b - 1] = h_ref[b:b + tc + CONV_HALO - SUBLANES, :]
    first = CONV_HALO - (CONV_WIDTH - 1)
    for c in range(tc // CONV_ROWS):
        acc = jnp.zeros((CONV_ROWS, CONV_CH), F32)
        for w in range(CONV_WIDTH):
            a, b = divmod(first + w, SUBLANES)
            r0 = c * CONV_ROWS + a * SUBLANES
            src = h_ref[r0:r0 + CONV_ROWS, :] if b == 0 else hs_ref[b - 1, r0:r0 + CONV_ROWS, :]
            acc = acc + src * cw_ref[w:w + 1, :]
        y = acc + cb_ref[...]
        mu = jnp.mean(y, axis=-1, keepdims=True)
        var = jnp.mean(jnp.square(y - mu), axis=-1, keepdims=True)
        y = (y - mu) * lax.rsqrt(var + EPS) * lg_ref[...] + lb_ref[...]
        o_ref[0, c * CONV_ROWS:(c + 1) * CONV_ROWS, :] = (y * jax.nn.sigmoid(y)).astype(o_ref.dtype)


def _conv(row, cw, cb, lg, lb, tc):
    B, S, _ = row.shape
    ucol = 0
    per = tc // CONV_HALO
    return pl.pallas_call(
        functools.partial(_conv_kernel, tc=tc),
        grid=(B, S // tc),
        in_specs=[
            pl.BlockSpec((1, tc, 2 * CONV_CH), lambda b, i: (b, i, ucol)),
            pl.BlockSpec((1, CONV_HALO, 2 * CONV_CH), lambda b, i: (b, jnp.maximum(i * per - 1, 0), ucol)),
            _const_spec((CONV_WIDTH, CONV_CH)),
            _const_spec((1, CONV_CH)),
            _const_spec((1, CONV_CH)),
            _const_spec((1, CONV_CH)),
        ],
        out_specs=pl.BlockSpec((1, tc, CONV_CH), lambda b, i: (b, i, 0)),
        out_shape=jax.ShapeDtypeStruct((B, S, CONV_CH), BF16),
        scratch_shapes=[pltpu.VMEM((CONV_HALO + tc, CONV_CH), F32),
                        pltpu.VMEM((SUBLANES - 1, CONV_HALO + tc - SUBLANES, CONV_CH), F32)],
        compiler_params=_params("parallel", "parallel"),
        name="conformer_conv",
    )(row, row, cw, cb, lg, lb)


def _merge_kernel(x_ref, g_ref, wg_ref, osb_ref, odf_ref, ocv_ref, wsb_ref, wdf_ref, wcv_ref, wout_ref, o_ref):
    x = x_ref[...]
    D = x.shape[1]
    xb = _rmsnorm(x, g_ref[...]).astype(BF16)
    y = None
    for n, (o_br, w_br) in enumerate(((osb_ref, wsb_ref), (odf_ref, wdf_ref), (ocv_ref, wcv_ref))):
        gate = jax.nn.sigmoid(_dot(xb, wg_ref[:, n * D:(n + 1) * D]))
        t = gate * _dot(o_br[...], w_br[...])
        y = t if y is None else y + t
    o_ref[...] = x + _dot(y.astype(BF16), wout_ref[...])


def _merge(x2, g, wg, osb, odf, ocv, wsb, wdf, wcv, wout, tm):
    N, D = x2.shape
    W = osb.shape[1]
    tok = lambda w: pl.BlockSpec((tm, w), lambda i: (i, 0))
    return pl.pallas_call(
        _merge_kernel,
        grid=(N // tm,),
        in_specs=[tok(D), _const_spec((1, D)), _const_spec((D, 3 * D)), tok(W), tok(W), tok(W),
                  _const_spec((W, D)), _const_spec((W, D)), _const_spec((W, D)), _const_spec((D, D))],
        out_specs=tok(D),
        out_shape=jax.ShapeDtypeStruct((N, D), F32),
        compiler_params=_params("parallel"),
        name="gated_merge",
    )(x2, g, wg, osb, odf, ocv, wsb, wdf, wcv, wout)


def _memkv_kernel(mem_ref, g_ref, wkv_ref, o_ref):
    mb = _rmsnorm(mem_ref[0], g_ref[...]).astype(BF16)
    o_ref[0] = _dot(mb, wkv_ref[...]).astype(o_ref.dtype)


def _memkv(mem, g, wkv):
    B, M, D = mem.shape
    return pl.pallas_call(
        _memkv_kernel,
        grid=(B,),
        in_specs=[pl.BlockSpec((1, M, D), lambda b: (b, 0, 0)), _const_spec((1, D)), _const_spec((D, 2 * XA_WIDTH))],
        out_specs=pl.BlockSpec((1, M, 2 * XA_WIDTH), lambda b: (b, 0, 0)),
        out_shape=jax.ShapeDtypeStruct((B, M, 2 * XA_WIDTH), BF16),
        compiler_params=_params("parallel"),
        name="mem_kv",
    )(mem, g, wkv)


def _xattn_kernel(x_ref, g_ref, wq_ref, kv_ref, wo_ref, o_ref):
    x = x_ref[0]
    xb = _rmsnorm(x, g_ref[...]).astype(BF16)
    q = _dot(xb, wq_ref[...]).astype(BF16)
    heads = []
    for h in range(XA_HEADS):
        hs = slice(h * XA_HEAD_DIM, (h + 1) * XA_HEAD_DIM)
        k = kv_ref[0, :, hs]
        v = kv_ref[0, :, XA_WIDTH + h * XA_HEAD_DIM:XA_WIDTH + (h + 1) * XA_HEAD_DIM]
        s = _dot_nt(q[:, hs], k) * (XA_HEAD_DIM ** -0.5)
        p = jnp.exp(s - jnp.max(s, axis=1, keepdims=True))
        p = p / jnp.sum(p, axis=1, keepdims=True)
        heads.append(_dot(p.astype(BF16), v).astype(BF16))
    o = jnp.concatenate(heads, axis=1)
    o_ref[0] = x + _dot(o, wo_ref[...])


def _xattn(x, g, wq, kv, wo, tm):
    B, S, D = x.shape
    M = kv.shape[1]
    return pl.pallas_call(
        _xattn_kernel,
        grid=(B, S // tm),
        in_specs=[
            pl.BlockSpec((1, tm, D), lambda b, i: (b, i, 0)),
            _const_spec((1, D)),
            _const_spec((D, XA_WIDTH)),
            pl.BlockSpec((1, M, 2 * XA_WIDTH), lambda b, i: (b, 0, 0)),
            _const_spec((XA_WIDTH, D)),
        ],
        out_specs=pl.BlockSpec((1, tm, D), lambda b, i: (b, i, 0)),
        out_shape=jax.ShapeDtypeStruct((B, S, D), F32),
        compiler_params=_params("parallel", "parallel"),
        name="cross_attention",
    )(x, g, wq, kv, wo)


def _mlp_kernel(x_ref, g_ref, wup_ref, wdown_ref, gf_ref, o_ref, *, fc, final):
    x = x_ref[...]
    xb = _rmsnorm(x, g_ref[...]).astype(BF16)
    acc = x
    for c in range(wup_ref.shape[1] // fc):
        cs = slice(c * fc, (c + 1) * fc)
        h = jnp.square(jnp.maximum(_dot(xb, wup_ref[:, cs]), 0.0)).astype(BF16)
        acc = acc + _dot(h, wdown_ref[cs, :])
    o_ref[...] = _rmsnorm(acc, gf_ref[...]) if final else acc


def _mlp(x2, g, wup, wdown, gf, tm, final):
    N, D = x2.shape
    F = wup.shape[1]
    return pl.pallas_call(
        functools.partial(_mlp_kernel, fc=MLP_CHUNK, final=final),
        grid=(N // tm,),
        in_specs=[pl.BlockSpec((tm, D), lambda i: (i, 0)), _const_spec((1, D)),
                  _const_spec((D, F)), _const_spec((F, D)), _const_spec((1, D))],
        out_specs=pl.BlockSpec((tm, D), lambda i: (i, 0)),
        out_shape=jax.ShapeDtypeStruct((N, D), F32),
        compiler_params=_params("parallel"),
        name="sqrelu_mlp",
    )(x2, g, wup, wdown, gf)


def kernel(x, mem, rel_bias, ln_mix, w_in, diff_lambda, diff_subln, conv_w, conv_b, conv_ln_g, conv_ln_b,
           w_sb_proj, w_diff_proj, w_conv_proj, w_out, ln_xattn, ln_mem, xa_w_q, xa_w_kv, xa_w_o,
           ln_mlp, w_up, w_down, ln_final):
    B, S, D = x.shape
    depth = w_in.shape[0]
    blk = ATT_BLOCK
    tm = min(TOKEN_TILE, S)
    tc = min(CONV_TILE, S)
    assert S % SB_QBLOCK == 0 and S % DF_QBLOCK == 0 and S % tm == 0 and S % tc == 0
    assert w_in.shape[1:] == (D, 3 * SB_WIDTH + 3 * DIFF_WIDTH + 2 * CONV_CH + 3 * D)

    row2 = lambda a: a.reshape(1, -1).astype(F32)
    tri = jnp.asarray(np.triu(np.ones((blk, blk), np.float32), k=1), BF16)
    bias = _bias_tiles(rel_bias, blk)
    scale = HEAD_DIM ** -0.5

    for l in range(depth):
        w = w_in[l]
        edges = np.cumsum([0] + 3 * [SB_WIDTH] + 3 * [DIFF_WIDTH] + [2 * CONV_CH])
        sbq, sbk, sbv, dfq, dfk, dfv, glu = (w[:, edges[n]:edges[n + 1]] for n in range(7))
        wrow = jnp.concatenate([glu, sbk, dfk], axis=1).astype(BF16)
        wtr = jnp.concatenate([sbq * scale, sbv, dfq * scale, dfv], axis=1).T.astype(BF16)
        wg = w[:, edges[7]:].astype(BF16)

        row, tr, va = _inproj(x, row2(ln_mix[l]), wrow, wtr, tm)
        kn = _key_norm_max(row)
        o_sb = _sb_attention(row, tr, tri, kn)
        lam_init = 0.8 - 0.6 * math.exp(-0.3 * l)
        o_df = _diff_attention(row, tr, va, bias, kn, diff_lambda[l].astype(F32), row2(diff_subln[l]), lam_init)
        o_cv = _conv(row, conv_w[l].reshape(CONV_WIDTH, CONV_CH).astype(F32), row2(conv_b[l]),
                     row2(conv_ln_g[l]), row2(conv_ln_b[l]), tc)

        N = B * S
        x2 = _merge(x.reshape(N, D), row2(ln_mix[l]), wg,
                    o_sb.reshape(N, -1), o_df.reshape(N, -1), o_cv.reshape(N, -1),
                    w_sb_proj[l].astype(BF16), w_diff_proj[l].astype(BF16), w_conv_proj[l].astype(BF16),
                    w_out[l].astype(BF16), tm)

        kv = _memkv(mem, row2(ln_mem[l]), xa_w_kv[l].astype(BF16))
        x3 = _xattn(x2.reshape(B, S, D), row2(ln_xattn[l]),
                    xa_w_q[l].astype(BF16), kv, xa_w_o[l].astype(BF16), tm)

        x = _mlp(x3.reshape(N, D), row2(ln_mlp[l]), w_up[l].astype(BF16), w_down[l].astype(BF16),
                 row2(ln_final), tm, final=(l == depth - 1)).reshape(B, S, D)
    return x
```

```python
import functools
import math

import numpy as np
import jax
import jax.numpy as jnp
from jax import lax
from jax.experimental import pallas as pl
from jax.experimental.pallas import tpu as pltpu

F32 = jnp.float32
BF16 = jnp.bfloat16

HEAD_DIM = 64
SB_WIDTH = 512
DIFF_WIDTH = 512
DIFF_HEADS = 4
CONV_CH = 512
CONV_WIDTH = 31
XA_HEADS = 4
XA_HEAD_DIM = 128
XA_WIDTH = XA_HEADS * XA_HEAD_DIM
NUM_BUCKETS = 32
MAX_EXACT = NUM_BUCKETS // 2
MAX_DISTANCE = 128
EPS = 1e-6
NEG_INF = -1e30

LANES = 128
SUBLANES = 8
ATT_BLOCK = 256
CONV_HALO = 32
CONV_ROWS = 64
VMEM_LIMIT = 56 * 1024 * 1024
TOKEN_TILE = 512
CONV_TILE = 256
PROJ_CHUNK = 512
MLP_CHUNK = 1024
NORM_SLACK = 1.01

SB_QBLOCK = 2048
DF_QBLOCK = 2048
SB_EXIT = 110.0
DF_UNROLL = 4
DF_SPREAD = 80.0
LOG2E = 1.4426950408889634
ONES_ROWS = 16

ROW_COLS = 2 * CONV_CH + SB_WIDTH + DIFF_WIDTH
ROW_SBK, ROW_DFK = 8, 12
TR_ROWS = 2 * SB_WIDTH + DIFF_WIDTH
TR_SBQ, TR_SBV, TR_DFQ = 0, 4, 8
VA_ROWS = LANES + ONES_ROWS


def _params(*sem):
    return pltpu.CompilerParams(dimension_semantics=sem, vmem_limit_bytes=VMEM_LIMIT)


def _const_spec(shape):
    nd = len(shape)
    return pl.BlockSpec(shape, lambda *_: (0,) * nd)


def _layer_spec(a, l):
    return pl.BlockSpec((None,) + a.shape[1:], lambda *_: (l, 0, 0))


def _rmsnorm(x, g):
    return x * lax.rsqrt(jnp.mean(x * x, axis=-1, keepdims=True) + EPS) * g


def _dot(a, b):
    return jnp.dot(a, b, preferred_element_type=F32)


def _dot_nt(a, b):
    return lax.dot_general(a, b, (((1,), (1,)), ((), ())), preferred_element_type=F32)


def _inproj_kernel(x_ref, g_ref, wrow_ref, wtr_ref, row_ref, tr_ref, va_ref):
    xb = _rmsnorm(x_ref[0], g_ref[...]).astype(BF16)
    tm = xb.shape[0]
    for c in range(ROW_COLS // PROJ_CHUNK):
        cs = slice(c * PROJ_CHUNK, (c + 1) * PROJ_CHUNK)
        row_ref[0, :, cs] = _dot(xb, wrow_ref[:, cs]).astype(BF16)
    for c in range(TR_ROWS // PROJ_CHUNK):
        cs = slice(c * PROJ_CHUNK, (c + 1) * PROJ_CHUNK)
        tr_ref[0, cs, :] = _dot_nt(wtr_ref[cs, :], xb).astype(BF16)
    dfv = _dot_nt(wtr_ref[TR_ROWS:, :], xb).astype(BF16)
    for h in range(DIFF_HEADS):
        va_ref[0, h * VA_ROWS:h * VA_ROWS + LANES, :] = dfv[h * LANES:(h + 1) * LANES, :]
        va_ref[0, h * VA_ROWS + LANES:(h + 1) * VA_ROWS, :] = jnp.ones((ONES_ROWS, tm), BF16)


def _inproj(x, g, l, wrow, wtr, tm):
    B, S, D = x.shape
    return pl.pallas_call(
        _inproj_kernel,
        grid=(B, S // tm),
        in_specs=[
            pl.BlockSpec((1, tm, D), lambda b, i: (b, i, 0)),
            _layer_spec(g, l),
            _const_spec((D, ROW_COLS)),
            _const_spec((TR_ROWS + DIFF_WIDTH, D)),
        ],
        out_specs=[
            pl.BlockSpec((1, tm, ROW_COLS), lambda b, i: (b, i, 0)),
            pl.BlockSpec((1, TR_ROWS, tm), lambda b, i: (b, 0, i)),
            pl.BlockSpec((1, DIFF_HEADS * VA_ROWS, tm), lambda b, i: (b, 0, i)),
        ],
        out_shape=[
            jax.ShapeDtypeStruct((B, S, ROW_COLS), BF16),
            jax.ShapeDtypeStruct((B, TR_ROWS, S), BF16),
            jax.ShapeDtypeStruct((B, DIFF_HEADS * VA_ROWS, S), BF16),
        ],
        compiler_params=_params("parallel", "parallel"),
        name="inproj",
    )(x, g, wrow, wtr)


def _knorm_kernel(k_ref, sel_ref, o_ref):
    k = k_ref[0].astype(F32)
    n2 = _dot((k * k).astype(BF16), sel_ref[...])
    o_ref[0] = jnp.sqrt(jnp.max(n2, axis=0, keepdims=True))


def _key_norm_max(row):
    B, S, _ = row.shape
    width = SB_WIDTH + DIFF_WIDTH
    sel = np.zeros((width, LANES), np.float32)
    sel[np.arange(width), np.arange(width) // HEAD_DIM] = 1.0
    return pl.pallas_call(
        _knorm_kernel,
        grid=(B,),
        in_specs=[pl.BlockSpec((1, S, width), lambda b: (b, 0, ROW_SBK * LANES // width)),
                  _const_spec((width, LANES))],
        out_specs=pl.BlockSpec((1, 1, LANES), lambda b: (b, 0, 0)),
        out_shape=jax.ShapeDtypeStruct((B, 1, LANES), F32),
        compiler_params=_params("parallel"),
        name="key_norm_max",
    )(row, jnp.asarray(sel, BF16))


def _sb_kernel(qt_ref, k_ref, vt_ref, tri_ref, kn_ref, o_ref):
    kb, qb = ATT_BLOCK, SB_QBLOCK
    nq = qb // kb
    p, i = pl.program_id(1), pl.program_id(2)
    qt = qt_ref[0]
    sub = lax.broadcasted_iota(jnp.int32, qt.shape, 0)
    zero = jnp.zeros_like(qt)
    qtm = (jnp.where(sub < HEAD_DIM, qt, zero), jnp.where(sub >= HEAD_DIM, qt, zero))
    tri = tri_ref[...]

    lane = lax.broadcasted_iota(jnp.int32, (1, LANES), 1)
    kn = kn_ref[0]
    zbound = []
    for h in range(2):
        knh = jnp.max(jnp.where(lane == 2 * p + h, kn, 0.0), axis=1, keepdims=True)
        qf = qtm[h].astype(F32)
        zbound.append(jnp.sqrt(jnp.sum(qf * qf, axis=0, keepdims=True)) * (knh * NORM_SLACK))

    def splice(old, new, lo, hi):
        parts = ([old[:, :lo]] if lo else []) + [new] + ([old[:, hi:]] if hi < qb else [])
        return parts[0] if len(parts) == 1 else jnp.concatenate(parts, axis=1)

    def tiles(jobs, st):
        carry, acc = [st[0], st[2]], [st[1], st[3]]
        ks = [pl.ds(pl.multiple_of(j * kb, kb), kb) for j, _, _, _ in jobs]
        units = [(n, h) for n in range(len(jobs)) for h in range(2)]
        z = {(n, h): _dot(k_ref[0, ks[n], :], qtm[h][:, jobs[n][1]:jobs[n][2]]) for n, h in units}
        sp = {u: jnp.maximum(z[u], 0.0) + jnp.log(1.0 + jnp.exp2(jnp.abs(z[u]) * -LOG2E)) for u in units}
        own = {u: z[u] - sp[u] for u in units}
        keep = []
        for n, (_, lo, hi, valid) in enumerate(jobs):
            m = (lax.broadcasted_iota(jnp.int32, (kb, hi - lo), 0)
                 < lax.broadcasted_iota(jnp.int32, (kb, hi - lo), 1)) if valid is None else valid
            keep.append(m)
            for h in range(2):
                sp[n, h] = jnp.where(m, sp[n, h], 0.0)
        cs = {u: _dot(tri, sp[u].astype(BF16)) for u in units}
        oldest_row = {u: sp[u][0:1, :] for u in units}
        for n, (_, lo, hi, _) in enumerate(jobs):
            vt = vt_ref[0, :, ks[n]]
            c = [cs[n, h] + carry[h][:, lo:hi] for h in range(2)]
            a = [jnp.where(keep[n], jnp.exp(own[n, h] - c[h]), 0.0) for h in range(2)]
            pv = [_dot(vt[h * HEAD_DIM:(h + 1) * HEAD_DIM, :], a[h].astype(BF16)) for h in range(2)]
            for h in range(2):
                carry[h] = splice(carry[h], c[h][0:1, :] + oldest_row[n, h], lo, hi)
                acc[h] = splice(acc[h], acc[h][:, lo:hi] + pv[h], lo, hi)
        return carry[0], acc[0], carry[1], acc[1]

    zc = jnp.zeros((1, qb), F32)
    za = jnp.zeros((HEAD_DIM, qb), F32)
    newest = [(nq * i + c, c * kb, min(c + 2, nq) * kb, None) for c in reversed(range(nq))]

    st = lax.cond(i > 0, lambda: tiles(newest + [(nq * i - 1, 0, kb, True)], (zc, za, zc, za)),
                  lambda: tiles(newest, (zc, za, zc, za)))

    def strip_step(j, cols, s):
        ks = pl.ds(pl.multiple_of(j * kb, kb), kb)
        k, vt = k_ref[0, ks, :], vt_ref[0, :, ks]
        z = [_dot(k, qtm[h][:, cols]) for h in range(2)]
        sp = [jnp.maximum(z[h], 0.0) + jnp.log(1.0 + jnp.exp2(jnp.abs(z[h]) * -LOG2E)) for h in range(2)]
        c = [_dot(tri, sp[h].astype(BF16)) + s[2 * h] for h in range(2)]
        a = [jnp.exp((z[h] - sp[h]) - c[h]).astype(BF16) for h in range(2)]
        pv = [_dot(vt[h * HEAD_DIM:(h + 1) * HEAD_DIM, :], a[h]) for h in range(2)]
        return (c[0][0:1, :] + sp[0][0:1, :], s[1] + pv[0], c[1][0:1, :] + sp[1][0:1, :], s[3] + pv[1])

    out = []
    for n in range(nq):
        cols = slice(n * kb, (n + 1) * kb)
        zb = (zbound[0][:, cols], zbound[1][:, cols])

        def unfinished(s, zb=zb):
            return (jnp.min(jnp.minimum(s[0] - zb[0], s[2] - zb[1])) <= SB_EXIT).astype(jnp.int32)

        def body(s, n=n, cols=cols, unfinished=unfinished):
            new = strip_step(nq * i + n - s[0], cols, s[2:])
            return (s[0] + 1, unfinished(new)) + new

        s0 = tuple(x[:, cols] for x in st)
        out.append(lax.while_loop(lambda s, n=n: jnp.logical_and(s[0] <= nq * i + n, s[1] > 0), body,
                                  (jnp.int32(2), unfinished(s0)) + s0)[2:])
    acc = [jnp.concatenate([o[1 + 2 * h] for o in out], axis=1) for h in range(2)]
    o_ref[0] = jnp.concatenate(acc, axis=0).T.astype(o_ref.dtype)


def _sb_attention(row, tr, tri, kn):
    B, S, _ = row.shape
    qb = SB_QBLOCK
    return pl.pallas_call(
        _sb_kernel,
        grid=(B, SB_WIDTH // LANES, S // qb),
        in_specs=[
            pl.BlockSpec((1, LANES, qb), lambda b, p, i: (b, TR_SBQ + p, i)),
            pl.BlockSpec((1, S, LANES), lambda b, p, i: (b, 0, ROW_SBK + p)),
            pl.BlockSpec((1, LANES, S), lambda b, p, i: (b, TR_SBV + p, 0)),
            _const_spec((ATT_BLOCK, ATT_BLOCK)),
            pl.BlockSpec((1, 1, LANES), lambda b, p, i: (b, 0, 0)),
        ],
        out_specs=pl.BlockSpec((1, qb, LANES), lambda b, p, i: (b, i, p)),
        out_shape=jax.ShapeDtypeStruct((B, S, SB_WIDTH), BF16),
        compiler_params=_params("parallel", "parallel", "arbitrary"),
        name="sb_attention",
    )(tr, row, tr, tri, kn)


def _bucket_tiles(blk):
    r = np.arange(blk)[None, :] - np.arange(blk)[:, None]
    rel = np.stack([r, r + blk, r + 2 * blk]).astype(np.int64)
    n = np.maximum(rel, 0)
    nf = np.maximum(n, 1).astype(np.float32)
    large = MAX_EXACT + (np.log(nf / np.float32(MAX_EXACT)) / np.float32(math.log(MAX_DISTANCE / MAX_EXACT))
                         * np.float32(NUM_BUCKETS - MAX_EXACT)).astype(np.int32)
    large = np.minimum(large, NUM_BUCKETS - 1)
    idx = np.where(n < MAX_EXACT, n, large).astype(np.int32)
    assert (idx[2] == NUM_BUCKETS - 1).all() and blk >= MAX_DISTANCE
    return idx


def _bias_kernel(table_ref, idx_ref, o_ref):
    h = pl.program_id(0)
    idx = idx_ref[0]
    acc = jnp.zeros(idx.shape, F32)
    for b in range(NUM_BUCKETS):
        acc = jnp.where(idx == b, table_ref[b, h], acc)
    o_ref[0, 0] = acc


def _bias_tiles(rel_bias, blk):
    idx = jnp.asarray(_bucket_tiles(blk))
    return pl.pallas_call(
        _bias_kernel,
        grid=(DIFF_HEADS, 3),
        in_specs=[
            pl.BlockSpec(memory_space=pltpu.SMEM),
            pl.BlockSpec((1, blk, blk), lambda h, d: (d, 0, 0)),
        ],
        out_specs=pl.BlockSpec((1, 1, blk, blk), lambda h, d: (h, d, 0, 0)),
        out_shape=jax.ShapeDtypeStruct((DIFF_HEADS, 3, blk, blk), F32),
        name="bias_tiles",
    )(rel_bias.astype(F32), idx)


def _diff_kernel(qt_ref, k_ref, va_ref, bias_ref, kn_ref, lam_ref, subln_ref, o_ref, *, lam_init):
    kb, qb = ATT_BLOCK, DF_QBLOCK
    nq = qb // kb
    h, i = pl.program_id(1), pl.program_id(2)
    qt = qt_ref[0]
    sub = lax.broadcasted_iota(jnp.int32, qt.shape, 0)
    zero = jnp.zeros_like(qt)
    qtm = (jnp.where(sub < HEAD_DIM, qt, zero), jnp.where(sub >= HEAD_DIM, qt, zero))
    far = bias_ref[0, 2][0:1, 0:1]

    def scores(j, lo):
        k = k_ref[0, pl.ds(pl.multiple_of(j * kb, kb), kb), :]
        return [_dot(k, qtm[m][:, lo:]) for m in range(2)]

    def update(z, j, st, lo, bias, mask, online):
        vta = va_ref[0, :, pl.ds(pl.multiple_of(j * kb, kb), kb)]
        if bias is not None:
            z = [x + bias for x in z]
        if mask is not None:
            z = [jnp.where(mask, x, NEG_INF) for x in z]
        mnew = [st[m][0][:, lo:] for m in range(2)]
        scale = None
        if online:
            top = [jnp.max(x, axis=0, keepdims=True) for x in z]
            if bias is None:
                top = [t + far for t in top]
            mold, mnew = mnew, [jnp.maximum(mnew[m], top[m]) for m in range(2)]
            scale = [jnp.exp(mold[m] - mnew[m]) for m in range(2)]
        shift = [mn - far for mn in mnew] if bias is None else mnew
        p = [jnp.exp(z[m] - shift[m]) for m in range(2)]
        pv = [_dot(vta, p[m].astype(BF16)) for m in range(2)]
        out = []
        for m in range(2):
            l, acc = st[m][1][:, lo:], st[m][2][:, lo:]
            if online:
                l, acc = scale[m] * l, scale[m] * acc
            new = (mnew[m], l + pv[m][LANES:LANES + 1, :], acc + pv[m][:LANES, :])
            if lo:
                new = tuple(jnp.concatenate([old[:, :lo], x], axis=1) for old, x in zip(st[m], new))
            out.append(new)
        return tuple(out)

    def run(stab, online):
        zl, za = jnp.zeros((1, qb), F32), jnp.zeros((LANES, qb), F32)
        st = ((stab[0], zl, za), (stab[1], zl, za))
        for r in reversed(range(nq)):
            n = qb - r * kb
            bias = jnp.concatenate([bias_ref[0, min(c - r, 2)] for c in range(r, nq)], axis=1)
            causal = lax.broadcasted_iota(jnp.int32, (kb, n), 0) <= lax.broadcasted_iota(jnp.int32, (kb, n), 1)
            st = update(scores(nq * i + r, r * kb), nq * i + r, st, r * kb, bias, causal, online)
        def older(st):
            bias = jnp.concatenate([bias_ref[0, min(c + 1, 2)] for c in range(nq)], axis=1)
            st = update(scores(nq * i - 1, 0), nq * i - 1, st, 0, bias, None, online)
            for d in range(2, DF_UNROLL + 1):
                st = update(scores(nq * i - d, 0), nq * i - d, st, 0, None, None, online)

            def body(jj, s):
                for d in range(DF_UNROLL):
                    j = nq * i - DF_UNROLL * (jj + 1) - 1 - d
                    s = update(scores(j, 0), j, s, 0, None, None, online)
                return s

            return lax.fori_loop(0, (nq * i) // DF_UNROLL - 1, body, st)

        return lax.cond(i > 0, older, lambda s: s, st)

    lane = lax.broadcasted_iota(jnp.int32, (1, LANES), 1)
    kn = kn_ref[0]
    allb = bias_ref[0]
    bmax = jnp.max(jnp.max(allb, axis=0), axis=(0, 1), keepdims=True)
    bmin = jnp.min(jnp.min(allb, axis=0), axis=(0, 1), keepdims=True)
    qk = []
    for m in range(2):
        knm = jnp.max(jnp.where(lane == SB_WIDTH // HEAD_DIM + 2 * h + m, kn, 0.0), axis=1, keepdims=True)
        qf = qtm[m].astype(F32)
        qk.append(jnp.sqrt(jnp.sum(qf * qf, axis=0, keepdims=True)) * (knm * NORM_SLACK))
    spread = 2.0 * jnp.maximum(qk[0], qk[1]) + (bmax - bmin)
    bound = [x + bmax for x in qk]
    neg = jnp.full((1, qb), NEG_INF, F32)
    st = lax.cond(jnp.max(spread) <= DF_SPREAD,
                  lambda: run(bound, False), lambda: run([neg, neg], True))

    lp = lam_ref[...]
    lam = (jnp.exp(jnp.sum(lp[0:1] * lp[1:2], axis=1, keepdims=True))
           - jnp.exp(jnp.sum(lp[2:3] * lp[3:4], axis=1, keepdims=True)) + lam_init)
    (_, l0, acc0), (_, l1, acc1) = st
    o = (acc0 / l0 - lam * (acc1 / l1)).T
    o_ref[0] = (_rmsnorm(o, subln_ref[...]) * (1.0 - lam_init)).astype(o_ref.dtype)


def _diff_attention(row, tr, va, bias, kn, lam_p, subln, l, lam_init):
    B, S, _ = row.shape
    kb, qb = ATT_BLOCK, DF_QBLOCK
    return pl.pallas_call(
        functools.partial(_diff_kernel, lam_init=lam_init),
        grid=(B, DIFF_HEADS, S // qb),
        in_specs=[
            pl.BlockSpec((1, LANES, qb), lambda b, h, i: (b, TR_DFQ + h, i)),
            pl.BlockSpec((1, S, LANES), lambda b, h, i: (b, 0, ROW_DFK + h)),
            pl.BlockSpec((1, VA_ROWS, S), lambda b, h, i: (b, h, 0)),
            pl.BlockSpec((1, 3, kb, kb), lambda b, h, i: (h, 0, 0, 0)),
            pl.BlockSpec((1, 1, LANES), lambda b, h, i: (b, 0, 0)),
            _layer_spec(lam_p, l),
            _layer_spec(subln, l),
        ],
        out_specs=pl.BlockSpec((1, qb, LANES), lambda b, h, i: (b, i, h)),
        out_shape=jax.ShapeDtypeStruct((B, S, DIFF_WIDTH), BF16),
        compiler_params=_params("parallel", "parallel", "arbitrary"),
        name="diff_attention",
    )(tr, row, va, bias, kn, lam_p, subln)


def _conv_kernel(u_ref, uprev_ref, cw_ref, cb_ref, lg_ref, lb_ref, o_ref, h_ref, hs_ref, *, tc):
    i = pl.program_id(1)

    def glu(u):
        u = u.astype(F32)
        return u[:, :CONV_CH] * jax.nn.sigmoid(u[:, CONV_CH:])

    h_ref[0:CONV_HALO, :] = jnp.where(i == 0, 0.0, glu(uprev_ref[0]))
    h_ref[CONV_HALO:CONV_HALO + tc, :] = glu(u_ref[0])
    for b in range(1, SUBLANES):
        hs_ref[b - 1] = h_ref[b:b + tc + CONV_HALO - SUBLANES, :]
    first = CONV_HALO - (CONV_WIDTH - 1)
    for c in range(tc // CONV_ROWS):
        acc = jnp.zeros((CONV_ROWS, CONV_CH), F32)
        for w in range(CONV_WIDTH):
            a, b = divmod(first + w, SUBLANES)
            r0 = c * CONV_ROWS + a * SUBLANES
            src = h_ref[r0:r0 + CONV_ROWS, :] if b == 0 else hs_ref[b - 1, r0:r0 + CONV_ROWS, :]
            acc = acc + src * cw_ref[w:w + 1, :]
        y = acc + cb_ref[...]
        mu = jnp.mean(y, axis=-1, keepdims=True)
        var = jnp.mean(jnp.square(y - mu), axis=-1, keepdims=True)
        y = (y - mu) * lax.rsqrt(var + EPS) * lg_ref[...] + lb_ref[...]
        o_ref[0, c * CONV_ROWS:(c + 1) * CONV_ROWS, :] = (y * jax.nn.sigmoid(y)).astype(o_ref.dtype)


def _conv(row, cw, cb, lg, lb, l, tc):
    B, S, _ = row.shape
    ucol = 0
    per = tc // CONV_HALO
    return pl.pallas_call(
        functools.partial(_conv_kernel, tc=tc),
        grid=(B, S // tc),
        in_specs=[
            pl.BlockSpec((1, tc, 2 * CONV_CH), lambda b, i: (b, i, ucol)),
            pl.BlockSpec((1, CONV_HALO, 2 * CONV_CH), lambda b, i: (b, jnp.maximum(i * per - 1, 0), ucol)),
            _layer_spec(cw, l),
            _layer_spec(cb, l),
            _layer_spec(lg, l),
            _layer_spec(lb, l),
        ],
        out_specs=pl.BlockSpec((1, tc, CONV_CH), lambda b, i: (b, i, 0)),
        out_shape=jax.ShapeDtypeStruct((B, S, CONV_CH), BF16),
        scratch_shapes=[pltpu.VMEM((CONV_HALO + tc, CONV_CH), F32),
                        pltpu.VMEM((SUBLANES - 1, CONV_HALO + tc - SUBLANES, CONV_CH), F32)],
        compiler_params=_params("parallel", "parallel"),
        name="conformer_conv",
    )(row, row, cw, cb, lg, lb)


def _merge_kernel(x_ref, g_ref, wg_ref, osb_ref, odf_ref, ocv_ref, wsb_ref, wdf_ref, wcv_ref, wout_ref, o_ref):
    x = x_ref[...]
    D = x.shape[1]
    xb = _rmsnorm(x, g_ref[...]).astype(BF16)
    y = None
    for n, (o_br, w_br) in enumerate(((osb_ref, wsb_ref), (odf_ref, wdf_ref), (ocv_ref, wcv_ref))):
        gate = jax.nn.sigmoid(_dot(xb, wg_ref[:, n * D:(n + 1) * D]))
        t = gate * _dot(o_br[...], w_br[...])
        y = t if y is None else y + t
    o_ref[...] = x + _dot(y.astype(BF16), wout_ref[...])


def _merge(x2, g, l, wg, osb, odf, ocv, wsb, wdf, wcv, wout, tm):
    N, D = x2.shape
    W = osb.shape[1]
    tok = lambda w: pl.BlockSpec((tm, w), lambda i: (i, 0))
    return pl.pallas_call(
        _merge_kernel,
        grid=(N // tm,),
        in_specs=[tok(D), _layer_spec(g, l), _const_spec((D, 3 * D)), tok(W), tok(W), tok(W),
                  _const_spec((W, D)), _const_spec((W, D)), _const_spec((W, D)), _const_spec((D, D))],
        out_specs=tok(D),
        out_shape=jax.ShapeDtypeStruct((N, D), F32),
        compiler_params=_params("parallel"),
        name="gated_merge",
    )(x2, g, wg, osb, odf, ocv, wsb, wdf, wcv, wout)


def _memkv_kernel(mem_ref, g_ref, wkv_ref, o_ref):
    mb = _rmsnorm(mem_ref[0], g_ref[...]).astype(BF16)
    o_ref[0] = _dot(mb, wkv_ref[...]).astype(o_ref.dtype)


def _memkv(mem, g, l, wkv):
    B, M, D = mem.shape
    return pl.pallas_call(
        _memkv_kernel,
        grid=(B,),
        in_specs=[pl.BlockSpec((1, M, D), lambda b: (b, 0, 0)), _layer_spec(g, l), _const_spec((D, 2 * XA_WIDTH))],
        out_specs=pl.BlockSpec((1, M, 2 * XA_WIDTH), lambda b: (b, 0, 0)),
        out_shape=jax.ShapeDtypeStruct((B, M, 2 * XA_WIDTH), BF16),
        compiler_params=_params("parallel"),
        name="mem_kv",
    )(mem, g, wkv)


def _xattn_kernel(x_ref, g_ref, wq_ref, kv_ref, wo_ref, o_ref):
    x = x_ref[0]
    xb = _rmsnorm(x, g_ref[...]).astype(BF16)
    q = _dot(xb, wq_ref[...]).astype(BF16)
    heads = []
    for h in range(XA_HEADS):
        hs = slice(h * XA_HEAD_DIM, (h + 1) * XA_HEAD_DIM)
        k = kv_ref[0, :, hs]
        v = kv_ref[0, :, XA_WIDTH + h * XA_HEAD_DIM:XA_WIDTH + (h + 1) * XA_HEAD_DIM]
        s = _dot_nt(q[:, hs], k) * (XA_HEAD_DIM ** -0.5)
        p = jnp.exp(s - jnp.max(s, axis=1, keepdims=True))
        p = p / jnp.sum(p, axis=1, keepdims=True)
        heads.append(_dot(p.astype(BF16), v).astype(BF16))
    o = jnp.concatenate(heads, axis=1)
    o_ref[0] = x + _dot(o, wo_ref[...])


def _xattn(x, g, l, wq, kv, wo, tm):
    B, S, D = x.shape
    M = kv.shape[1]
    return pl.pallas_call(
        _xattn_kernel,
        grid=(B, S // tm),
        in_specs=[
            pl.BlockSpec((1, tm, D), lambda b, i: (b, i, 0)),
            _layer_spec(g, l),
            _const_spec((D, XA_WIDTH)),
            pl.BlockSpec((1, M, 2 * XA_WIDTH), lambda b, i: (b, 0, 0)),
            _const_spec((XA_WIDTH, D)),
        ],
        out_specs=pl.BlockSpec((1, tm, D), lambda b, i: (b, i, 0)),
        out_shape=jax.ShapeDtypeStruct((B, S, D), F32),
        compiler_params=_params("parallel", "parallel"),
        name="cross_attention",
    )(x, g, wq, kv, wo)


def _mlp_kernel(x_ref, g_ref, wup_ref, wdown_ref, gf_ref, o_ref, *, fc, final):
    x = x_ref[...]
    xb = _rmsnorm(x, g_ref[...]).astype(BF16)
    acc = x
    for c in range(wup_ref.shape[1] // fc):
        cs = slice(c * fc, (c + 1) * fc)
        h = jnp.square(jnp.maximum(_dot(xb, wup_ref[:, cs]), 0.0)).astype(BF16)
        acc = acc + _dot(h, wdown_ref[cs, :])
    o_ref[...] = _rmsnorm(acc, gf_ref[...]) if final else acc


def _mlp(x2, g, l, wup, wdown, gf, tm, final):
    N, D = x2.shape
    F = wup.shape[1]
    return pl.pallas_call(
        functools.partial(_mlp_kernel, fc=MLP_CHUNK, final=final),
        grid=(N // tm,),
        in_specs=[pl.BlockSpec((tm, D), lambda i: (i, 0)), _layer_spec(g, l),
                  _const_spec((D, F)), _const_spec((F, D)), _const_spec((1, D))],
        out_specs=pl.BlockSpec((tm, D), lambda i: (i, 0)),
        out_shape=jax.ShapeDtypeStruct((N, D), F32),
        compiler_params=_params("parallel"),
        name="sqrelu_mlp",
    )(x2, g, wup, wdown, gf)


def kernel(x, mem, rel_bias, ln_mix, w_in, diff_lambda, diff_subln, conv_w, conv_b, conv_ln_g, conv_ln_b,
           w_sb_proj, w_diff_proj, w_conv_proj, w_out, ln_xattn, ln_mem, xa_w_q, xa_w_kv, xa_w_o,
           ln_mlp, w_up, w_down, ln_final):
    B, S, D = x.shape
    depth = w_in.shape[0]
    blk = ATT_BLOCK
    tm = min(TOKEN_TILE, S)
    tc = min(CONV_TILE, S)
    assert S % SB_QBLOCK == 0 and S % DF_QBLOCK == 0 and S % tm == 0 and S % tc == 0
    assert w_in.shape[1:] == (D, 3 * SB_WIDTH + 3 * DIFF_WIDTH + 2 * CONV_CH + 3 * D)

    stack = lambda a: a.reshape(depth, 1, -1).astype(F32)
    g_mix, g_xattn, g_mem, g_mlp, subln = (stack(a) for a in (ln_mix, ln_xattn, ln_mem, ln_mlp, diff_subln))
    cb, clg, clb = stack(conv_b), stack(conv_ln_g), stack(conv_ln_b)
    cw = conv_w.reshape(depth, CONV_WIDTH, CONV_CH).astype(F32)
    lam_p = diff_lambda.astype(F32)
    g_final = ln_final.reshape(1, -1).astype(F32)
    tri = jnp.asarray(np.triu(np.ones((blk, blk), np.float32), k=1), BF16)
    bias = _bias_tiles(rel_bias, blk)
    scale = HEAD_DIM ** -0.5

    for l in range(depth):
        w = w_in[l]
        edges = np.cumsum([0] + 3 * [SB_WIDTH] + 3 * [DIFF_WIDTH] + [2 * CONV_CH])
        sbq, sbk, sbv, dfq, dfk, dfv, glu = (w[:, edges[n]:edges[n + 1]] for n in range(7))
        wrow = jnp.concatenate([glu, sbk, dfk], axis=1).astype(BF16)
        wtr = jnp.concatenate([sbq * scale, sbv, dfq * scale, dfv], axis=1).T.astype(BF16)
        wg = w[:, edges[7]:].astype(BF16)

        row, tr, va = _inproj(x, g_mix, l, wrow, wtr, tm)
        kn = _key_norm_max(row)
        o_sb = _sb_attention(row, tr, tri, kn)
        lam_init = 0.8 - 0.6 * math.exp(-0.3 * l)
        o_df = _diff_attention(row, tr, va, bias, kn, lam_p, subln, l, lam_init)
        o_cv = _conv(row, cw, cb, clg, clb, l, tc)

        N = B * S
        x2 = _merge(x.reshape(N, D), g_mix, l, wg,
                    o_sb.reshape(N, -1), o_df.reshape(N, -1), o_cv.reshape(N, -1),
                    w_sb_proj[l].astype(BF16), w_diff_proj[l].astype(BF16), w_conv_proj[l].astype(BF16),
                    w_out[l].astype(BF16), tm)

        kv = _memkv(mem, g_mem, l, xa_w_kv[l].astype(BF16))
        x3 = _xattn(x2.reshape(B, S, D), g_xattn, l, xa_w_q[l].astype(BF16), kv, xa_w_o[l].astype(BF16), tm)

        x = _mlp(x3.reshape(N, D), g_mlp, l, w_up[l].astype(BF16), w_down[l].astype(BF16),
                 g_final, tm, final=(l == depth - 1)).reshape(B, S, D)
    return x
```

```python
import functools
import math

import numpy as np
import jax
import jax.numpy as jnp
from jax import lax
from jax.experimental import pallas as pl
from jax.experimental.pallas import tpu as pltpu

F32 = jnp.float32
BF16 = jnp.bfloat16

HEAD_DIM = 64
SB_WIDTH = 512
DIFF_WIDTH = 512
DIFF_HEADS = 4
CONV_CH = 512
CONV_WIDTH = 31
XA_HEADS = 4
XA_HEAD_DIM = 128
XA_WIDTH = XA_HEADS * XA_HEAD_DIM
NUM_BUCKETS = 32
MAX_EXACT = NUM_BUCKETS // 2
MAX_DISTANCE = 128
EPS = 1e-6
NEG_INF = -1e30

LANES = 128
SUBLANES = 8
ATT_BLOCK = 256
CONV_HALO = 32
CONV_ROWS = 64
VMEM_LIMIT = 56 * 1024 * 1024
TOKEN_TILE = 512
CONV_TILE = 256
PROJ_CHUNK = 512
MLP_CHUNK = 1024
NORM_SLACK = 1.01

SB_QBLOCK = 2048
DF_QBLOCK = 2048
SB_EXIT = 110.0
DF_UNROLL = 4
DF_SPREAD = 80.0
LOG2E = 1.4426950408889634
ONES_ROWS = 16

ROW_COLS = 2 * CONV_CH + SB_WIDTH + DIFF_WIDTH
ROW_SBK, ROW_DFK = 8, 12
TR_ROWS = 2 * SB_WIDTH + DIFF_WIDTH
TR_SBQ, TR_SBV, TR_DFQ = 0, 4, 8
VA_ROWS = LANES + ONES_ROWS


def _params(*sem):
    return pltpu.CompilerParams(dimension_semantics=sem, vmem_limit_bytes=VMEM_LIMIT)


def _const_spec(shape):
    nd = len(shape)
    return pl.BlockSpec(shape, lambda *_: (0,) * nd)


def _layer_spec(a, l):
    return pl.BlockSpec((None,) + a.shape[1:], lambda *_: (l, 0, 0))


def _rmsnorm(x, g):
    return x * lax.rsqrt(jnp.mean(x * x, axis=-1, keepdims=True) + EPS) * g


def _dot(a, b):
    return jnp.dot(a, b, preferred_element_type=F32)


def _dot_nt(a, b):
    return lax.dot_general(a, b, (((1,), (1,)), ((), ())), preferred_element_type=F32)


def _inproj_kernel(x_ref, g_ref, wrow_ref, wtr_ref, row_ref, tr_ref, va_ref):
    xb = _rmsnorm(x_ref[0], g_ref[...]).astype(BF16)
    tm = xb.shape[0]
    for c in range(ROW_COLS // PROJ_CHUNK):
        cs = slice(c * PROJ_CHUNK, (c + 1) * PROJ_CHUNK)
        row_ref[0, :, cs] = _dot(xb, wrow_ref[:, cs]).astype(BF16)
    for c in range(TR_ROWS // PROJ_CHUNK):
        cs = slice(c * PROJ_CHUNK, (c + 1) * PROJ_CHUNK)
        tr_ref[0, cs, :] = _dot_nt(wtr_ref[cs, :], xb).astype(BF16)
    dfv = _dot_nt(wtr_ref[TR_ROWS:, :], xb).astype(BF16)
    for h in range(DIFF_HEADS):
        va_ref[0, h * VA_ROWS:h * VA_ROWS + LANES, :] = dfv[h * LANES:(h + 1) * LANES, :]
        va_ref[0, h * VA_ROWS + LANES:(h + 1) * VA_ROWS, :] = jnp.ones((ONES_ROWS, tm), BF16)


def _inproj(x, g, l, wrow, wtr, tm):
    B, S, D = x.shape
    return pl.pallas_call(
        _inproj_kernel,
        grid=(B, S // tm),
        in_specs=[
            pl.BlockSpec((1, tm, D), lambda b, i: (b, i, 0)),
            _layer_spec(g, l),
            _const_spec((D, ROW_COLS)),
            _const_spec((TR_ROWS + DIFF_WIDTH, D)),
        ],
        out_specs=[
            pl.BlockSpec((1, tm, ROW_COLS), lambda b, i: (b, i, 0)),
            pl.BlockSpec((1, TR_ROWS, tm), lambda b, i: (b, 0, i)),
            pl.BlockSpec((1, DIFF_HEADS * VA_ROWS, tm), lambda b, i: (b, 0, i)),
        ],
        out_shape=[
            jax.ShapeDtypeStruct((B, S, ROW_COLS), BF16),
            jax.ShapeDtypeStruct((B, TR_ROWS, S), BF16),
            jax.ShapeDtypeStruct((B, DIFF_HEADS * VA_ROWS, S), BF16),
        ],
        compiler_params=_params("parallel", "parallel"),
        name="inproj",
    )(x, g, wrow, wtr)


def _knorm_kernel(k_ref, sel_ref, o_ref):
    k = k_ref[0].astype(F32)
    n2 = _dot((k * k).astype(BF16), sel_ref[...])
    o_ref[0] = jnp.sqrt(jnp.max(n2, axis=0, keepdims=True))


def _key_norm_max(row):
    B, S, _ = row.shape
    width = SB_WIDTH + DIFF_WIDTH
    sel = np.zeros((width, LANES), np.float32)
    sel[np.arange(width), np.arange(width) // HEAD_DIM] = 1.0
    return pl.pallas_call(
        _knorm_kernel,
        grid=(B,),
        in_specs=[pl.BlockSpec((1, S, width), lambda b: (b, 0, ROW_SBK * LANES // width)),
                  _const_spec((width, LANES))],
        out_specs=pl.BlockSpec((1, 1, LANES), lambda b: (b, 0, 0)),
        out_shape=jax.ShapeDtypeStruct((B, 1, LANES), F32),
        compiler_params=_params("parallel"),
        name="key_norm_max",
    )(row, jnp.asarray(sel, BF16))


def _sb_kernel(qt_ref, k_ref, vt_ref, tri_ref, kn_ref, o_ref):
    kb, qb = ATT_BLOCK, SB_QBLOCK
    nq = qb // kb
    p, i = pl.program_id(1), pl.program_id(2)
    qt = qt_ref[0]
    sub = lax.broadcasted_iota(jnp.int32, qt.shape, 0)
    zero = jnp.zeros_like(qt)
    qtm = (jnp.where(sub < HEAD_DIM, qt, zero), jnp.where(sub >= HEAD_DIM, qt, zero))
    tri = tri_ref[...]

    lane = lax.broadcasted_iota(jnp.int32, (1, LANES), 1)
    kn = kn_ref[0]
    zbound = []
    for h in range(2):
        knh = jnp.max(jnp.where(lane == 2 * p + h, kn, 0.0), axis=1, keepdims=True)
        qf = qtm[h].astype(F32)
        zbound.append(jnp.sqrt(jnp.sum(qf * qf, axis=0, keepdims=True)) * (knh * NORM_SLACK))

    def splice(old, new, lo, hi):
        parts = ([old[:, :lo]] if lo else []) + [new] + ([old[:, hi:]] if hi < qb else [])
        return parts[0] if len(parts) == 1 else jnp.concatenate(parts, axis=1)

    def tiles(jobs, st):
        carry, acc = [st[0], st[2]], [st[1], st[3]]
        ks = [pl.ds(pl.multiple_of(j * kb, kb), kb) for j, _, _, _ in jobs]
        units = [(n, h) for n in range(len(jobs)) for h in range(2)]
        z = {(n, h): _dot(k_ref[0, ks[n], :], qtm[h][:, jobs[n][1]:jobs[n][2]]) for n, h in units}
        sp = {u: jnp.maximum(z[u], 0.0) + jnp.log(1.0 + jnp.exp2(jnp.abs(z[u]) * -LOG2E)) for u in units}
        own = {u: z[u] - sp[u] for u in units}
        keep = []
        for n, (_, lo, hi, valid) in enumerate(jobs):
            m = (lax.broadcasted_iota(jnp.int32, (kb, hi - lo), 0)
                 < lax.broadcasted_iota(jnp.int32, (kb, hi - lo), 1)) if valid is None else valid
            keep.append(m)
            for h in range(2):
                sp[n, h] = jnp.where(m, sp[n, h], 0.0)
        cs = {u: _dot(tri, sp[u].astype(BF16)) for u in units}
        oldest_row = {u: sp[u][0:1, :] for u in units}
        for n, (_, lo, hi, _) in enumerate(jobs):
            vt = vt_ref[0, :, ks[n]]
            c = [cs[n, h] + carry[h][:, lo:hi] for h in range(2)]
            a = [jnp.where(keep[n], jnp.exp(own[n, h] - c[h]), 0.0) for h in range(2)]
            pv = [_dot(vt[h * HEAD_DIM:(h + 1) * HEAD_DIM, :], a[h].astype(BF16)) for h in range(2)]
            for h in range(2):
                carry[h] = splice(carry[h], c[h][0:1, :] + oldest_row[n, h], lo, hi)
                acc[h] = splice(acc[h], acc[h][:, lo:hi] + pv[h], lo, hi)
        return carry[0], acc[0], carry[1], acc[1]

    zc = jnp.zeros((1, qb), F32)
    za = jnp.zeros((HEAD_DIM, qb), F32)
    newest = [(nq * i + c, c * kb, min(c + 2, nq) * kb, None) for c in reversed(range(nq))]

    st = lax.cond(i > 0, lambda: tiles(newest + [(nq * i - 1, 0, kb, True)], (zc, za, zc, za)),
                  lambda: tiles(newest, (zc, za, zc, za)))

    def strip_step(j, cols, s):
        ks = pl.ds(pl.multiple_of(j * kb, kb), kb)
        k, vt = k_ref[0, ks, :], vt_ref[0, :, ks]
        z = [_dot(k, qtm[h][:, cols]) for h in range(2)]
        sp = [jnp.maximum(z[h], 0.0) + jnp.log(1.0 + jnp.exp2(jnp.abs(z[h]) * -LOG2E)) for h in range(2)]
        c = [_dot(tri, sp[h].astype(BF16)) + s[2 * h] for h in range(2)]
        a = [jnp.exp((z[h] - sp[h]) - c[h]).astype(BF16) for h in range(2)]
        pv = [_dot(vt[h * HEAD_DIM:(h + 1) * HEAD_DIM, :], a[h]) for h in range(2)]
        return (c[0][0:1, :] + sp[0][0:1, :], s[1] + pv[0], c[1][0:1, :] + sp[1][0:1, :], s[3] + pv[1])

    out = []
    for n in range(nq):
        cols = slice(n * kb, (n + 1) * kb)
        zb = (zbound[0][:, cols], zbound[1][:, cols])

        def unfinished(s, zb=zb):
            return (jnp.min(jnp.minimum(s[0] - zb[0], s[2] - zb[1])) <= SB_EXIT).astype(jnp.int32)

        def body(s, n=n, cols=cols, unfinished=unfinished):
            new = strip_step(nq * i + n - s[0], cols, s[2:])
            return (s[0] + 1, unfinished(new)) + new

        s0 = tuple(x[:, cols] for x in st)
        out.append(lax.while_loop(lambda s, n=n: jnp.logical_and(s[0] <= nq * i + n, s[1] > 0), body,
                                  (jnp.int32(2), unfinished(s0)) + s0)[2:])
    acc = [jnp.concatenate([o[1 + 2 * h] for o in out], axis=1) for h in range(2)]
    o_ref[0] = jnp.concatenate(acc, axis=0).T.astype(o_ref.dtype)


def _sb_attention(row, tr, tri, kn):
    B, S, _ = row.shape
    qb = SB_QBLOCK
    return pl.pallas_call(
        _sb_kernel,
        grid=(B, SB_WIDTH // LANES, S // qb),
        in_specs=[
            pl.BlockSpec((1, LANES, qb), lambda b, p, i: (b, TR_SBQ + p, i)),
            pl.BlockSpec((1, S, LANES), lambda b, p, i: (b, 0, ROW_SBK + p)),
            pl.BlockSpec((1, LANES, S), lambda b, p, i: (b, TR_SBV + p, 0)),
            _const_spec((ATT_BLOCK, ATT_BLOCK)),
            pl.BlockSpec((1, 1, LANES), lambda b, p, i: (b, 0, 0)),
        ],
        out_specs=pl.BlockSpec((1, qb, LANES), lambda b, p, i: (b, i, p)),
        out_shape=jax.ShapeDtypeStruct((B, S, SB_WIDTH), BF16),
        compiler_params=_params("parallel", "parallel", "arbitrary"),
        name="sb_attention",
    )(tr, row, tr, tri, kn)


def _bucket_tiles(blk):
    r = np.arange(blk)[None, :] - np.arange(blk)[:, None]
    rel = np.stack([r, r + blk, r + 2 * blk]).astype(np.int64)
    n = np.maximum(rel, 0)
    nf = np.maximum(n, 1).astype(np.float32)
    large = MAX_EXACT + (np.log(nf / np.float32(MAX_EXACT)) / np.float32(math.log(MAX_DISTANCE / MAX_EXACT))
                         * np.float32(NUM_BUCKETS - MAX_EXACT)).astype(np.int32)
    large = np.minimum(large, NUM_BUCKETS - 1)
    idx = np.where(n < MAX_EXACT, n, large).astype(np.int32)
    assert (idx[2] == NUM_BUCKETS - 1).all() and blk >= MAX_DISTANCE
    return idx


def _bias_kernel(table_ref, idx_ref, o_ref):
    h = pl.program_id(0)
    idx = idx_ref[0]
    acc = jnp.zeros(idx.shape, F32)
    for b in range(NUM_BUCKETS):
        acc = jnp.where(idx == b, table_ref[b, h], acc)
    o_ref[0, 0] = acc


def _bias_tiles(rel_bias, blk):
    idx = jnp.asarray(_bucket_tiles(blk))
    return pl.pallas_call(
        _bias_kernel,
        grid=(DIFF_HEADS, 3),
        in_specs=[
            pl.BlockSpec(memory_space=pltpu.SMEM),
            pl.BlockSpec((1, blk, blk), lambda h, d: (d, 0, 0)),
        ],
        out_specs=pl.BlockSpec((1, 1, blk, blk), lambda h, d: (h, d, 0, 0)),
        out_shape=jax.ShapeDtypeStruct((DIFF_HEADS, 3, blk, blk), F32),
        name="bias_tiles",
    )(rel_bias.astype(F32), idx)


def _diff_kernel(qt_ref, k_ref, va_ref, bias_ref, kn_ref, lam_ref, subln_ref, o_ref, *, lam_init):
    kb, qb = ATT_BLOCK, DF_QBLOCK
    nq = qb // kb
    h, i = pl.program_id(1), pl.program_id(2)
    qt = qt_ref[0]
    sub = lax.broadcasted_iota(jnp.int32, qt.shape, 0)
    zero = jnp.zeros_like(qt)
    qtm = (jnp.where(sub < HEAD_DIM, qt, zero), jnp.where(sub >= HEAD_DIM, qt, zero))
    far = bias_ref[0, 2][0:1, 0:1]

    def scores(j, lo):
        k = k_ref[0, pl.ds(pl.multiple_of(j * kb, kb), kb), :]
        return [_dot(k, qtm[m][:, lo:]) for m in range(2)]

    def update(z, j, st, lo, bias, mask, online):
        vta = va_ref[0, :, pl.ds(pl.multiple_of(j * kb, kb), kb)]
        if bias is not None:
            z = [x + bias for x in z]
        if mask is not None:
            z = [jnp.where(mask, x, NEG_INF) for x in z]
        mnew = [st[m][0][:, lo:] for m in range(2)]
        scale = None
        if online:
            top = [jnp.max(x, axis=0, keepdims=True) for x in z]
            if bias is None:
                top = [t + far for t in top]
            mold, mnew = mnew, [jnp.maximum(mnew[m], top[m]) for m in range(2)]
            scale = [jnp.exp(mold[m] - mnew[m]) for m in range(2)]
        shift = [mn - far for mn in mnew] if bias is None else mnew
        p = [jnp.exp(z[m] - shift[m]) for m in range(2)]
        pv = [_dot(vta, p[m].astype(BF16)) for m in range(2)]
        out = []
        for m in range(2):
            l, acc = st[m][1][:, lo:], st[m][2][:, lo:]
            if online:
                l, acc = scale[m] * l, scale[m] * acc
            new = (mnew[m], l + pv[m][LANES:LANES + 1, :], acc + pv[m][:LANES, :])
            if lo:
                new = tuple(jnp.concatenate([old[:, :lo], x], axis=1) for old, x in zip(st[m], new))
            out.append(new)
        return tuple(out)

    def run(stab, online):
        zl, za = jnp.zeros((1, qb), F32), jnp.zeros((LANES, qb), F32)
        st = ((stab[0], zl, za), (stab[1], zl, za))
        for r in reversed(range(nq)):
            n = qb - r * kb
            bias = jnp.concatenate([bias_ref[0, min(c - r, 2)] for c in range(r, nq)], axis=1)
            causal = lax.broadcasted_iota(jnp.int32, (kb, n), 0) <= lax.broadcasted_iota(jnp.int32, (kb, n), 1)
            st = update(scores(nq * i + r, r * kb), nq * i + r, st, r * kb, bias, causal, online)
        def older(st):
            bias = jnp.concatenate([bias_ref[0, min(c + 1, 2)] for c in range(nq)], axis=1)
            st = update(scores(nq * i - 1, 0), nq * i - 1, st, 0, bias, None, online)
            for d in range(2, DF_UNROLL + 1):
                st = update(scores(nq * i - d, 0), nq * i - d, st, 0, None, None, online)

            def body(jj, s):
                for d in range(DF_UNROLL):
                    j = nq * i - DF_UNROLL * (jj + 1) - 1 - d
                    s = update(scores(j, 0), j, s, 0, None, None, online)
                return s

            return lax.fori_loop(0, (nq * i) // DF_UNROLL - 1, body, st)

        return lax.cond(i > 0, older, lambda s: s, st)

    lane = lax.broadcasted_iota(jnp.int32, (1, LANES), 1)
    kn = kn_ref[0]
    allb = bias_ref[0]
    bmax = jnp.max(jnp.max(allb, axis=0), axis=(0, 1), keepdims=True)
    bmin = jnp.min(jnp.min(allb, axis=0), axis=(0, 1), keepdims=True)
    qk = []
    for m in range(2):
        knm = jnp.max(jnp.where(lane == SB_WIDTH // HEAD_DIM + 2 * h + m, kn, 0.0), axis=1, keepdims=True)
        qf = qtm[m].astype(F32)
        qk.append(jnp.sqrt(jnp.sum(qf * qf, axis=0, keepdims=True)) * (knm * NORM_SLACK))
    spread = 2.0 * jnp.maximum(qk[0], qk[1]) + (bmax - bmin)
    bound = [x + bmax for x in qk]
    neg = jnp.full((1, qb), NEG_INF, F32)
    st = lax.cond(jnp.max(spread) <= DF_SPREAD,
                  lambda: run(bound, False), lambda: run([neg, neg], True))

    lp = lam_ref[...]
    lam = (jnp.exp(jnp.sum(lp[0:1] * lp[1:2], axis=1, keepdims=True))
           - jnp.exp(jnp.sum(lp[2:3] * lp[3:4], axis=1, keepdims=True)) + lam_init)
    (_, l0, acc0), (_, l1, acc1) = st
    o = (acc0 / l0 - lam * (acc1 / l1)).T
    o_ref[0] = (_rmsnorm(o, subln_ref[...]) * (1.0 - lam_init)).astype(o_ref.dtype)


def _diff_attention(row, tr, va, bias, kn, lam_p, subln, l, lam_init):
    B, S, _ = row.shape
    kb, qb = ATT_BLOCK, DF_QBLOCK
    return pl.pallas_call(
        functools.partial(_diff_kernel, lam_init=lam_init),
        grid=(B, DIFF_HEADS, S // qb),
        in_specs=[
            pl.BlockSpec((1, LANES, qb), lambda b, h, i: (b, TR_DFQ + h, i)),
            pl.BlockSpec((1, S, LANES), lambda b, h, i: (b, 0, ROW_DFK + h)),
            pl.BlockSpec((1, VA_ROWS, S), lambda b, h, i: (b, h, 0)),
            pl.BlockSpec((1, 3, kb, kb), lambda b, h, i: (h, 0, 0, 0)),
            pl.BlockSpec((1, 1, LANES), lambda b, h, i: (b, 0, 0)),
            _layer_spec(lam_p, l),
            _layer_spec(subln, l),
        ],
        out_specs=pl.BlockSpec((1, qb, LANES), lambda b, h, i: (b, i, h)),
        out_shape=jax.ShapeDtypeStruct((B, S, DIFF_WIDTH), BF16),
        compiler_params=_params("parallel", "parallel", "arbitrary"),
        name="diff_attention",
    )(tr, row, va, bias, kn, lam_p, subln)


def _conv_kernel(u_ref, uprev_ref, cw_ref, cb_ref, lg_ref, lb_ref, o_ref, h_ref, hs_ref, *, tc):
    i = pl.program_id(1)

    def glu(u):
        u = u.astype(F32)
        return u[:, :CONV_CH] * jax.nn.sigmoid(u[:, CONV_CH:])

    h_ref[0:CONV_HALO, :] = jnp.where(i == 0, 0.0, glu(uprev_ref[0]))
    h_ref[CONV_HALO:CONV_HALO + tc, :] = glu(u_ref[0])
    for b in range(1, SUBLANES):
        hs_ref[b - 1] = h_ref[b:b + tc + CONV_HALO - SUBLANES, :]
    first = CONV_HALO - (CONV_WIDTH - 1)
    for c in range(tc // CONV_ROWS):
        acc = jnp.zeros((CONV_ROWS, CONV_CH), F32)
        for w in range(CONV_WIDTH):
            a, b = divmod(first + w, SUBLANES)
            r0 = c * CONV_ROWS + a * SUBLANES
            src = h_ref[r0:r0 + CONV_ROWS, :] if b == 0 else hs_ref[b - 1, r0:r0 + CONV_ROWS, :]
            acc = acc + src * cw_ref[w:w + 1, :]
        y = acc + cb_ref[...]
        mu = jnp.mean(y, axis=-1, keepdims=True)
        var = jnp.mean(jnp.square(y - mu), axis=-1, keepdims=True)
        y = (y - mu) * lax.rsqrt(var + EPS) * lg_ref[...] + lb_ref[...]
        o_ref[0, c * CONV_ROWS:(c + 1) * CONV_ROWS, :] = (y * jax.nn.sigmoid(y)).astype(o_ref.dtype)


def _conv(row, cw, cb, lg, lb, l, tc):
    B, S, _ = row.shape
    ucol = 0
    per = tc // CONV_HALO
    return pl.pallas_call(
        functools.partial(_conv_kernel, tc=tc),
        grid=(B, S // tc),
        in_specs=[
            pl.BlockSpec((1, tc, 2 * CONV_CH), lambda b, i: (b, i, ucol)),
            pl.BlockSpec((1, CONV_HALO, 2 * CONV_CH), lambda b, i: (b, jnp.maximum(i * per - 1, 0), ucol)),
            _layer_spec(cw, l),
            _layer_spec(cb, l),
            _layer_spec(lg, l),
            _layer_spec(lb, l),
        ],
        out_specs=pl.BlockSpec((1, tc, CONV_CH), lambda b, i: (b, i, 0)),
        out_shape=jax.ShapeDtypeStruct((B, S, CONV_CH), BF16),
        scratch_shapes=[pltpu.VMEM((CONV_HALO + tc, CONV_CH), F32),
                        pltpu.VMEM((SUBLANES - 1, CONV_HALO + tc - SUBLANES, CONV_CH), F32)],
        compiler_params=_params("parallel", "parallel"),
        name="conformer_conv",
    )(row, row, cw, cb, lg, lb)


def _merge_kernel(x_ref, g_ref, wg_ref, osb_ref, odf_ref, ocv_ref, wsb_ref, wdf_ref, wcv_ref, wout_ref, o_ref):
    x = x_ref[...]
    D = x.shape[1]
    xb = _rmsnorm(x, g_ref[...]).astype(BF16)
    y = None
    for n, (o_br, w_br) in enumerate(((osb_ref, wsb_ref), (odf_ref, wdf_ref), (ocv_ref, wcv_ref))):
        gate = jax.nn.sigmoid(_dot(xb, wg_ref[:, n * D:(n + 1) * D]))
        t = gate * _dot(o_br[...], w_br[...])
        y = t if y is None else y + t
    o_ref[...] = x + _dot(y.astype(BF16), wout_ref[...])


def _merge(x2, g, l, wg, osb, odf, ocv, wsb, wdf, wcv, wout, tm):
    N, D = x2.shape
    W = osb.shape[1]
    tok = lambda w: pl.BlockSpec((tm, w), lambda i: (i, 0))
    return pl.pallas_call(
        _merge_kernel,
        grid=(N // tm,),
        in_specs=[tok(D), _layer_spec(g, l), _const_spec((D, 3 * D)), tok(W), tok(W), tok(W),
                  _const_spec((W, D)), _const_spec((W, D)), _const_spec((W, D)), _const_spec((D, D))],
        out_specs=tok(D),
        out_shape=jax.ShapeDtypeStruct((N, D), F32),
        compiler_params=_params("parallel"),
        name="gated_merge",
    )(x2, g, wg, osb, odf, ocv, wsb, wdf, wcv, wout)


def _memkv_kernel(mem_ref, g_ref, wkv_ref, o_ref):
    mb = _rmsnorm(mem_ref[0], g_ref[...]).astype(BF16)
    o_ref[0] = _dot(mb, wkv_ref[...]).astype(o_ref.dtype)


def _memkv(mem, g, l, wkv):
    B, M, D = mem.shape
    return pl.pallas_call(
        _memkv_kernel,
        grid=(B,),
        in_specs=[pl.BlockSpec((1, M, D), lambda b: (b, 0, 0)), _layer_spec(g, l), _const_spec((D, 2 * XA_WIDTH))],
        out_specs=pl.BlockSpec((1, M, 2 * XA_WIDTH), lambda b: (b, 0, 0)),
        out_shape=jax.ShapeDtypeStruct((B, M, 2 * XA_WIDTH), BF16),
        compiler_params=_params("parallel"),
        name="mem_kv",
    )(mem, g, wkv)


def _xattn_kernel(x_ref, g_ref, wq_ref, kv_ref, wo_ref, o_ref):
    x = x_ref[0]
    xb = _rmsnorm(x, g_ref[...]).astype(BF16)
    q = _dot(xb, wq_ref[...]).astype(BF16)
    hs = [slice(h * XA_HEAD_DIM, (h + 1) * XA_HEAD_DIM) for h in range(XA_HEADS)]
    s = [_dot_nt(q[:, hs[h]], kv_ref[0, :, hs[h]]) * (XA_HEAD_DIM ** -0.5) for h in range(XA_HEADS)]
    p = [jnp.exp(x - jnp.max(x, axis=1, keepdims=True)) for x in s]
    p = [x / jnp.sum(x, axis=1, keepdims=True) for x in p]
    heads = [_dot(p[h].astype(BF16), kv_ref[0, :, XA_WIDTH + h * XA_HEAD_DIM:XA_WIDTH + (h + 1) * XA_HEAD_DIM])
             .astype(BF16) for h in range(XA_HEADS)]
    o = jnp.concatenate(heads, axis=1)
    o_ref[0] = x + _dot(o, wo_ref[...])


def _xattn(x, g, l, wq, kv, wo, tm):
    B, S, D = x.shape
    M = kv.shape[1]
    return pl.pallas_call(
        _xattn_kernel,
        grid=(B, S // tm),
        in_specs=[
            pl.BlockSpec((1, tm, D), lambda b, i: (b, i, 0)),
            _layer_spec(g, l),
            _const_spec((D, XA_WIDTH)),
            pl.BlockSpec((1, M, 2 * XA_WIDTH), lambda b, i: (b, 0, 0)),
            _const_spec((XA_WIDTH, D)),
        ],
        out_specs=pl.BlockSpec((1, tm, D), lambda b, i: (b, i, 0)),
        out_shape=jax.ShapeDtypeStruct((B, S, D), F32),
        compiler_params=_params("parallel", "parallel"),
        name="cross_attention",
    )(x, g, wq, kv, wo)


def _mlp_kernel(x_ref, g_ref, wup_ref, wdown_ref, gf_ref, o_ref, *, fc, final):
    x = x_ref[...]
    xb = _rmsnorm(x, g_ref[...]).astype(BF16)
    acc = x
    for c in range(wup_ref.shape[1] // fc):
        cs = slice(c * fc, (c + 1) * fc)
        h = jnp.square(jnp.maximum(_dot(xb, wup_ref[:, cs]), 0.0)).astype(BF16)
        acc = acc + _dot(h, wdown_ref[cs, :])
    o_ref[...] = _rmsnorm(acc, gf_ref[...]) if final else acc


def _mlp(x2, g, l, wup, wdown, gf, tm, final):
    N, D = x2.shape
    F = wup.shape[1]
    return pl.pallas_call(
        functools.partial(_mlp_kernel, fc=MLP_CHUNK, final=final),
        grid=(N // tm,),
        in_specs=[pl.BlockSpec((tm, D), lambda i: (i, 0)), _layer_spec(g, l),
                  _const_spec((D, F)), _const_spec((F, D)), _const_spec((1, D))],
        out_specs=pl.BlockSpec((tm, D), lambda i: (i, 0)),
        out_shape=jax.ShapeDtypeStruct((N, D), F32),
        compiler_params=_params("parallel"),
        name="sqrelu_mlp",
    )(x2, g, wup, wdown, gf)


def kernel(x, mem, rel_bias, ln_mix, w_in, diff_lambda, diff_subln, conv_w, conv_b, conv_ln_g, conv_ln_b,
           w_sb_proj, w_diff_proj, w_conv_proj, w_out, ln_xattn, ln_mem, xa_w_q, xa_w_kv, xa_w_o,
           ln_mlp, w_up, w_down, ln_final):
    B, S, D = x.shape
    depth = w_in.shape[0]
    blk = ATT_BLOCK
    tm = min(TOKEN_TILE, S)
    tc = min(CONV_TILE, S)
    assert S % SB_QBLOCK == 0 and S % DF_QBLOCK == 0 and S % tm == 0 and S % tc == 0
    assert w_in.shape[1:] == (D, 3 * SB_WIDTH + 3 * DIFF_WIDTH + 2 * CONV_CH + 3 * D)

    stack = lambda a: a.reshape(depth, 1, -1).astype(F32)
    g_mix, g_xattn, g_mem, g_mlp, subln = (stack(a) for a in (ln_mix, ln_xattn, ln_mem, ln_mlp, diff_subln))
    cb, clg, clb = stack(conv_b), stack(conv_ln_g), stack(conv_ln_b)
    cw = conv_w.reshape(depth, CONV_WIDTH, CONV_CH).astype(F32)
    lam_p = diff_lambda.astype(F32)
    g_final = ln_final.reshape(1, -1).astype(F32)
    tri = jnp.asarray(np.triu(np.ones((blk, blk), np.float32), k=1), BF16)
    bias = _bias_tiles(rel_bias, blk)
    scale = HEAD_DIM ** -0.5

    for l in range(depth):
        w = w_in[l]
        edges = np.cumsum([0] + 3 * [SB_WIDTH] + 3 * [DIFF_WIDTH] + [2 * CONV_CH])
        sbq, sbk, sbv, dfq, dfk, dfv, glu = (w[:, edges[n]:edges[n + 1]] for n in range(7))
        wrow = jnp.concatenate([glu, sbk, dfk], axis=1).astype(BF16)
        wtr = jnp.concatenate([sbq * scale, sbv, dfq * scale, dfv], axis=1).T.astype(BF16)
        wg = w[:, edges[7]:].astype(BF16)

        row, tr, va = _inproj(x, g_mix, l, wrow, wtr, tm)
        kn = _key_norm_max(row)
        o_sb = _sb_attention(row, tr, tri, kn)
        lam_init = 0.8 - 0.6 * math.exp(-0.3 * l)
        o_df = _diff_attention(row, tr, va, bias, kn, lam_p, subln, l, lam_init)
        o_cv = _conv(row, cw, cb, clg, clb, l, tc)

        N = B * S
        x2 = _merge(x.reshape(N, D), g_mix, l, wg,
                    o_sb.reshape(N, -1), o_df.reshape(N, -1), o_cv.reshape(N, -1),
                    w_sb_proj[l].astype(BF16), w_diff_proj[l].astype(BF16), w_conv_proj[l].astype(BF16),
                    w_out[l].astype(BF16), tm)

        kv = _memkv(mem, g_mem, l, xa_w_kv[l].astype(BF16))
        x3 = _xattn(x2.reshape(B, S, D), g_xattn, l, xa_w_q[l].astype(BF16), kv, xa_w_o[l].astype(BF16), tm)

        x = _mlp(x3.reshape(N, D), g_mlp, l, w_up[l].astype(BF16), w_down[l].astype(BF16),
                 g_final, tm, final=(l == depth - 1)).reshape(B, S, D)
    return x
```

```python
import functools
import math

import numpy as np
import jax
import jax.numpy as jnp
from jax import lax
from jax.experimental import pallas as pl
from jax.experimental.pallas import tpu as pltpu

F32 = jnp.float32
BF16 = jnp.bfloat16

HEAD_DIM = 64
SB_WIDTH = 512
DIFF_WIDTH = 512
DIFF_HEADS = 4
CONV_CH = 512
CONV_WIDTH = 31
XA_HEADS = 4
XA_HEAD_DIM = 128
XA_WIDTH = XA_HEADS * XA_HEAD_DIM
NUM_BUCKETS = 32
MAX_EXACT = NUM_BUCKETS // 2
MAX_DISTANCE = 128
EPS = 1e-6
NEG_INF = -1e30

LANES = 128
SUBLANES = 8
ATT_BLOCK = 256
CONV_HALO = 32
CONV_ROWS = 64
VMEM_LIMIT = 56 * 1024 * 1024
TOKEN_TILE = 512
MLP_TOKEN_TILE = 1024
CONV_TILE = 256
PROJ_CHUNK = 512
MLP_CHUNK = 1024
NORM_SLACK = 1.01

SB_QBLOCK = 2048
DF_QBLOCK = 2048
SB_EXIT = 110.0
DF_UNROLL = 4
DF_SPREAD = 80.0
LOG2E = 1.4426950408889634
ONES_ROWS = 16

ROW_COLS = 2 * CONV_CH + SB_WIDTH + DIFF_WIDTH
ROW_SBK, ROW_DFK = 8, 12
TR_ROWS = 2 * SB_WIDTH + DIFF_WIDTH
TR_SBQ, TR_SBV, TR_DFQ = 0, 4, 8
VA_ROWS = LANES + ONES_ROWS


def _params(*sem):
    return pltpu.CompilerParams(dimension_semantics=sem, vmem_limit_bytes=VMEM_LIMIT)


def _const_spec(shape):
    nd = len(shape)
    return pl.BlockSpec(shape, lambda *_: (0,) * nd)


def _layer_spec(a, l):
    return pl.BlockSpec((None,) + a.shape[1:], lambda *_: (l, 0, 0))


def _rmsnorm(x, g):
    return x * lax.rsqrt(jnp.mean(x * x, axis=-1, keepdims=True) + EPS) * g


def _dot(a, b):
    return jnp.dot(a, b, preferred_element_type=F32)


def _dot_nt(a, b):
    return lax.dot_general(a, b, (((1,), (1,)), ((), ())), preferred_element_type=F32)


def _inproj_kernel(x_ref, g_ref, wrow_ref, wtr_ref, row_ref, tr_ref, va_ref):
    xb = _rmsnorm(x_ref[0], g_ref[...]).astype(BF16)
    tm = xb.shape[0]
    for c in range(ROW_COLS // PROJ_CHUNK):
        cs = slice(c * PROJ_CHUNK, (c + 1) * PROJ_CHUNK)
        row_ref[0, :, cs] = _dot(xb, wrow_ref[:, cs]).astype(BF16)
    for c in range(TR_ROWS // PROJ_CHUNK):
        cs = slice(c * PROJ_CHUNK, (c + 1) * PROJ_CHUNK)
        tr_ref[0, cs, :] = _dot_nt(wtr_ref[cs, :], xb).astype(BF16)
    dfv = _dot_nt(wtr_ref[TR_ROWS:, :], xb).astype(BF16)
    for h in range(DIFF_HEADS):
        va_ref[0, h * VA_ROWS:h * VA_ROWS + LANES, :] = dfv[h * LANES:(h + 1) * LANES, :]
        va_ref[0, h * VA_ROWS + LANES:(h + 1) * VA_ROWS, :] = jnp.ones((ONES_ROWS, tm), BF16)


def _inproj(x, g, l, wrow, wtr, tm):
    B, S, D = x.shape
    return pl.pallas_call(
        _inproj_kernel,
        grid=(B, S // tm),
        in_specs=[
            pl.BlockSpec((1, tm, D), lambda b, i: (b, i, 0)),
            _layer_spec(g, l),
            _const_spec((D, ROW_COLS)),
            _const_spec((TR_ROWS + DIFF_WIDTH, D)),
        ],
        out_specs=[
            pl.BlockSpec((1, tm, ROW_COLS), lambda b, i: (b, i, 0)),
            pl.BlockSpec((1, TR_ROWS, tm), lambda b, i: (b, 0, i)),
            pl.BlockSpec((1, DIFF_HEADS * VA_ROWS, tm), lambda b, i: (b, 0, i)),
        ],
        out_shape=[
            jax.ShapeDtypeStruct((B, S, ROW_COLS), BF16),
            jax.ShapeDtypeStruct((B, TR_ROWS, S), BF16),
            jax.ShapeDtypeStruct((B, DIFF_HEADS * VA_ROWS, S), BF16),
        ],
        compiler_params=_params("parallel", "parallel"),
        name="inproj",
    )(x, g, wrow, wtr)


def _knorm_kernel(k_ref, sel_ref, o_ref):
    k = k_ref[0].astype(F32)
    n2 = _dot((k * k).astype(BF16), sel_ref[...])
    o_ref[0] = jnp.sqrt(jnp.max(n2, axis=0, keepdims=True))


def _key_norm_max(row):
    B, S, _ = row.shape
    width = SB_WIDTH + DIFF_WIDTH
    sel = np.zeros((width, LANES), np.float32)
    sel[np.arange(width), np.arange(width) // HEAD_DIM] = 1.0
    return pl.pallas_call(
        _knorm_kernel,
        grid=(B,),
        in_specs=[pl.BlockSpec((1, S, width), lambda b: (b, 0, ROW_SBK * LANES // width)),
                  _const_spec((width, LANES))],
        out_specs=pl.BlockSpec((1, 1, LANES), lambda b: (b, 0, 0)),
        out_shape=jax.ShapeDtypeStruct((B, 1, LANES), F32),
        compiler_params=_params("parallel"),
        name="key_norm_max",
    )(row, jnp.asarray(sel, BF16))


def _sb_kernel(qt_ref, k_ref, vt_ref, tri_ref, kn_ref, o_ref):
    kb, qb = ATT_BLOCK, SB_QBLOCK
    nq = qb // kb
    p, i = pl.program_id(1), pl.program_id(2)
    qt = qt_ref[0]
    sub = lax.broadcasted_iota(jnp.int32, qt.shape, 0)
    zero = jnp.zeros_like(qt)
    qtm = (jnp.where(sub < HEAD_DIM, qt, zero), jnp.where(sub >= HEAD_DIM, qt, zero))
    tri = tri_ref[...]

    lane = lax.broadcasted_iota(jnp.int32, (1, LANES), 1)
    kn = kn_ref[0]
    zbound = []
    for h in range(2):
        knh = jnp.max(jnp.where(lane == 2 * p + h, kn, 0.0), axis=1, keepdims=True)
        qf = qtm[h].astype(F32)
        zbound.append(jnp.sqrt(jnp.sum(qf * qf, axis=0, keepdims=True)) * (knh * NORM_SLACK))

    def splice(old, new, lo, hi):
        parts = ([old[:, :lo]] if lo else []) + [new] + ([old[:, hi:]] if hi < qb else [])
        return parts[0] if len(parts) == 1 else jnp.concatenate(parts, axis=1)

    def tiles(jobs, st):
        carry, acc = [st[0], st[2]], [st[1], st[3]]
        ks = [pl.ds(pl.multiple_of(j * kb, kb), kb) for j, _, _, _ in jobs]
        units = [(n, h) for n in range(len(jobs)) for h in range(2)]
        z = {(n, h): _dot(k_ref[0, ks[n], :], qtm[h][:, jobs[n][1]:jobs[n][2]]) for n, h in units}
        sp = {u: jnp.maximum(z[u], 0.0) + jnp.log(1.0 + jnp.exp2(jnp.abs(z[u]) * -LOG2E)) for u in units}
        own = {u: z[u] - sp[u] for u in units}
        keep = []
        for n, (_, lo, hi, valid) in enumerate(jobs):
            m = (lax.broadcasted_iota(jnp.int32, (kb, hi - lo), 0)
                 < lax.broadcasted_iota(jnp.int32, (kb, hi - lo), 1)) if valid is None else valid
            keep.append(m)
            for h in range(2):
                sp[n, h] = jnp.where(m, sp[n, h], 0.0)
        cs = {u: _dot(tri, sp[u].astype(BF16)) for u in units}
        oldest_row = {u: sp[u][0:1, :] for u in units}
        for n, (_, lo, hi, _) in enumerate(jobs):
            vt = vt_ref[0, :, ks[n]]
            c = [cs[n, h] + carry[h][:, lo:hi] for h in range(2)]
            a = [jnp.where(keep[n], jnp.exp(own[n, h] - c[h]), 0.0) for h in range(2)]
            pv = [_dot(vt[h * HEAD_DIM:(h + 1) * HEAD_DIM, :], a[h].astype(BF16)) for h in range(2)]
            for h in range(2):
                carry[h] = splice(carry[h], c[h][0:1, :] + oldest_row[n, h], lo, hi)
                acc[h] = splice(acc[h], acc[h][:, lo:hi] + pv[h], lo, hi)
        return carry[0], acc[0], carry[1], acc[1]

    zc = jnp.zeros((1, qb), F32)
    za = jnp.zeros((HEAD_DIM, qb), F32)
    newest = [(nq * i + c, c * kb, min(c + 2, nq) * kb, None) for c in reversed(range(nq))]

    st = lax.cond(i > 0, lambda: tiles(newest + [(nq * i - 1, 0, kb, True)], (zc, za, zc, za)),
                  lambda: tiles(newest, (zc, za, zc, za)))

    def strip_step(j, cols, s):
        ks = pl.ds(pl.multiple_of(j * kb, kb), kb)
        k, vt = k_ref[0, ks, :], vt_ref[0, :, ks]
        z = [_dot(k, qtm[h][:, cols]) for h in range(2)]
        sp = [jnp.maximum(z[h], 0.0) + jnp.log(1.0 + jnp.exp2(jnp.abs(z[h]) * -LOG2E)) for h in range(2)]
        c = [_dot(tri, sp[h].astype(BF16)) + s[2 * h] for h in range(2)]
        a = [jnp.exp((z[h] - sp[h]) - c[h]).astype(BF16) for h in range(2)]
        pv = [_dot(vt[h * HEAD_DIM:(h + 1) * HEAD_DIM, :], a[h]) for h in range(2)]
        return (c[0][0:1, :] + sp[0][0:1, :], s[1] + pv[0], c[1][0:1, :] + sp[1][0:1, :], s[3] + pv[1])

    out = []
    for n in range(nq):
        cols = slice(n * kb, (n + 1) * kb)
        zb = (zbound[0][:, cols], zbound[1][:, cols])

        def unfinished(s, zb=zb):
            return (jnp.min(jnp.minimum(s[0] - zb[0], s[2] - zb[1])) <= SB_EXIT).astype(jnp.int32)

        def body(s, n=n, cols=cols, unfinished=unfinished):
            new = strip_step(nq * i + n - s[0], cols, s[2:])
            return (s[0] + 1, unfinished(new)) + new

        s0 = tuple(x[:, cols] for x in st)
        out.append(lax.while_loop(lambda s, n=n: jnp.logical_and(s[0] <= nq * i + n, s[1] > 0), body,
                                  (jnp.int32(2), unfinished(s0)) + s0)[2:])
    acc = [jnp.concatenate([o[1 + 2 * h] for o in out], axis=1) for h in range(2)]
    o_ref[0] = jnp.concatenate(acc, axis=0).T.astype(o_ref.dtype)


def _sb_attention(row, tr, tri, kn):
    B, S, _ = row.shape
    qb = SB_QBLOCK
    return pl.pallas_call(
        _sb_kernel,
        grid=(B, SB_WIDTH // LANES, S // qb),
        in_specs=[
            pl.BlockSpec((1, LANES, qb), lambda b, p, i: (b, TR_SBQ + p, i)),
            pl.BlockSpec((1, S, LANES), lambda b, p, i: (b, 0, ROW_SBK + p)),
            pl.BlockSpec((1, LANES, S), lambda b, p, i: (b, TR_SBV + p, 0)),
            _const_spec((ATT_BLOCK, ATT_BLOCK)),
            pl.BlockSpec((1, 1, LANES), lambda b, p, i: (b, 0, 0)),
        ],
        out_specs=pl.BlockSpec((1, qb, LANES), lambda b, p, i: (b, i, p)),
        out_shape=jax.ShapeDtypeStruct((B, S, SB_WIDTH), BF16),
        compiler_params=_params("parallel", "parallel", "arbitrary"),
        name="sb_attention",
    )(tr, row, tr, tri, kn)


def _bucket_tiles(blk):
    r = np.arange(blk)[None, :] - np.arange(blk)[:, None]
    rel = np.stack([r, r + blk, r + 2 * blk]).astype(np.int64)
    n = np.maximum(rel, 0)
    nf = np.maximum(n, 1).astype(np.float32)
    large = MAX_EXACT + (np.log(nf / np.float32(MAX_EXACT)) / np.float32(math.log(MAX_DISTANCE / MAX_EXACT))
                         * np.float32(NUM_BUCKETS - MAX_EXACT)).astype(np.int32)
    large = np.minimum(large, NUM_BUCKETS - 1)
    idx = np.where(n < MAX_EXACT, n, large).astype(np.int32)
    assert (idx[2] == NUM_BUCKETS - 1).all() and blk >= MAX_DISTANCE
    return idx


def _bias_kernel(table_ref, idx_ref, o_ref):
    h = pl.program_id(0)
    idx = idx_ref[0]
    acc = jnp.zeros(idx.shape, F32)
    for b in range(NUM_BUCKETS):
        acc = jnp.where(idx == b, table_ref[b, h], acc)
    o_ref[0, 0] = acc


def _bias_tiles(rel_bias, blk):
    idx = jnp.asarray(_bucket_tiles(blk))
    return pl.pallas_call(
        _bias_kernel,
        grid=(DIFF_HEADS, 3),
        in_specs=[
            pl.BlockSpec(memory_space=pltpu.SMEM),
            pl.BlockSpec((1, blk, blk), lambda h, d: (d, 0, 0)),
        ],
        out_specs=pl.BlockSpec((1, 1, blk, blk), lambda h, d: (h, d, 0, 0)),
        out_shape=jax.ShapeDtypeStruct((DIFF_HEADS, 3, blk, blk), F32),
        name="bias_tiles",
    )(rel_bias.astype(F32), idx)


def _diff_kernel(qt_ref, k_ref, va_ref, bias_ref, kn_ref, lam_ref, subln_ref, o_ref, *, lam_init):
    kb, qb = ATT_BLOCK, DF_QBLOCK
    nq = qb // kb
    h, i = pl.program_id(1), pl.program_id(2)
    qt = qt_ref[0]
    sub = lax.broadcasted_iota(jnp.int32, qt.shape, 0)
    zero = jnp.zeros_like(qt)
    qtm = (jnp.where(sub < HEAD_DIM, qt, zero), jnp.where(sub >= HEAD_DIM, qt, zero))
    far = bias_ref[0, 2][0:1, 0:1]

    def scores(j, lo):
        k = k_ref[0, pl.ds(pl.multiple_of(j * kb, kb), kb), :]
        return [_dot(k, qtm[m][:, lo:]) for m in range(2)]

    def update(z, j, st, lo, bias, mask, online):
        vta = va_ref[0, :, pl.ds(pl.multiple_of(j * kb, kb), kb)]
        if bias is not None:
            z = [x + bias for x in z]
        if mask is not None:
            z = [jnp.where(mask, x, NEG_INF) for x in z]
        mnew = [st[m][0][:, lo:] for m in range(2)]
        scale = None
        if online:
            top = [jnp.max(x, axis=0, keepdims=True) for x in z]
            if bias is None:
                top = [t + far for t in top]
            mold, mnew = mnew, [jnp.maximum(mnew[m], top[m]) for m in range(2)]
            scale = [jnp.exp(mold[m] - mnew[m]) for m in range(2)]
        shift = [mn - far for mn in mnew] if bias is None else mnew
        p = [jnp.exp(z[m] - shift[m]) for m in range(2)]
        pv = [_dot(vta, p[m].astype(BF16)) for m in range(2)]
        out = []
        for m in range(2):
            l, acc = st[m][1][:, lo:], st[m][2][:, lo:]
            if online:
                l, acc = scale[m] * l, scale[m] * acc
            new = (mnew[m], l + pv[m][LANES:LANES + 1, :], acc + pv[m][:LANES, :])
            if lo:
                new = tuple(jnp.concatenate([old[:, :lo], x], axis=1) for old, x in zip(st[m], new))
            out.append(new)
        return tuple(out)

    def run(stab, online):
        zl, za = jnp.zeros((1, qb), F32), jnp.zeros((LANES, qb), F32)
        st = ((stab[0], zl, za), (stab[1], zl, za))
        for r in reversed(range(nq)):
            n = qb - r * kb
            bias = jnp.concatenate([bias_ref[0, min(c - r, 2)] for c in range(r, nq)], axis=1)
            causal = lax.broadcasted_iota(jnp.int32, (kb, n), 0) <= lax.broadcasted_iota(jnp.int32, (kb, n), 1)
            st = update(scores(nq * i + r, r * kb), nq * i + r, st, r * kb, bias, causal, online)
        def older(st):
            bias = jnp.concatenate([bias_ref[0, min(c + 1, 2)] for c in range(nq)], axis=1)
            st = update(scores(nq * i - 1, 0), nq * i - 1, st, 0, bias, None, online)
            for d in range(2, DF_UNROLL + 1):
                st = update(scores(nq * i - d, 0), nq * i - d, st, 0, None, None, online)

            def body(jj, s):
                for d in range(DF_UNROLL):
                    j = nq * i - DF_UNROLL * (jj + 1) - 1 - d
                    s = update(scores(j, 0), j, s, 0, None, None, online)
                return s

            return lax.fori_loop(0, (nq * i) // DF_UNROLL - 1, body, st)

        return lax.cond(i > 0, older, lambda s: s, st)

    lane = lax.broadcasted_iota(jnp.int32, (1, LANES), 1)
    kn = kn_ref[0]
    allb = bias_ref[0]
    bmax = jnp.max(jnp.max(allb, axis=0), axis=(0, 1), keepdims=True)
    bmin = jnp.min(jnp.min(allb, axis=0), axis=(0, 1), keepdims=True)
    qk = []
    for m in range(2):
        knm = jnp.max(jnp.where(lane == SB_WIDTH // HEAD_DIM + 2 * h + m, kn, 0.0), axis=1, keepdims=True)
        qf = qtm[m].astype(F32)
        qk.append(jnp.sqrt(jnp.sum(qf * qf, axis=0, keepdims=True)) * (knm * NORM_SLACK))
    spread = 2.0 * jnp.maximum(qk[0], qk[1]) + (bmax - bmin)
    bound = [x + bmax for x in qk]
    neg = jnp.full((1, qb), NEG_INF, F32)
    st = lax.cond(jnp.max(spread) <= DF_SPREAD,
                  lambda: run(bound, False), lambda: run([neg, neg], True))

    lp = lam_ref[...]
    lam = (jnp.exp(jnp.sum(lp[0:1] * lp[1:2], axis=1, keepdims=True))
           - jnp.exp(jnp.sum(lp[2:3] * lp[3:4], axis=1, keepdims=True)) + lam_init)
    (_, l0, acc0), (_, l1, acc1) = st
    o = (acc0 / l0 - lam * (acc1 / l1)).T
    o_ref[0] = (_rmsnorm(o, subln_ref[...]) * (1.0 - lam_init)).astype(o_ref.dtype)


def _diff_attention(row, tr, va, bias, kn, lam_p, subln, l, lam_init):
    B, S, _ = row.shape
    kb, qb = ATT_BLOCK, DF_QBLOCK
    return pl.pallas_call(
        functools.partial(_diff_kernel, lam_init=lam_init),
        grid=(B, DIFF_HEADS, S // qb),
        in_specs=[
            pl.BlockSpec((1, LANES, qb), lambda b, h, i: (b, TR_DFQ + h, i)),
            pl.BlockSpec((1, S, LANES), lambda b, h, i: (b, 0, ROW_DFK + h)),
            pl.BlockSpec((1, VA_ROWS, S), lambda b, h, i: (b, h, 0)),
            pl.BlockSpec((1, 3, kb, kb), lambda b, h, i: (h, 0, 0, 0)),
            pl.BlockSpec((1, 1, LANES), lambda b, h, i: (b, 0, 0)),
            _layer_spec(lam_p, l),
            _layer_spec(subln, l),
        ],
        out_specs=pl.BlockSpec((1, qb, LANES), lambda b, h, i: (b, i, h)),
        out_shape=jax.ShapeDtypeStruct((B, S, DIFF_WIDTH), BF16),
        compiler_params=_params("parallel", "parallel", "arbitrary"),
        name="diff_attention",
    )(tr, row, va, bias, kn, lam_p, subln)


def _conv_kernel(u_ref, uprev_ref, cw_ref, cb_ref, lg_ref, lb_ref, o_ref, h_ref, hs_ref, *, tc):
    i = pl.program_id(1)

    def glu(u):
        u = u.astype(F32)
        return u[:, :CONV_CH] * jax.nn.sigmoid(u[:, CONV_CH:])

    h_ref[0:CONV_HALO, :] = jnp.where(i == 0, 0.0, glu(uprev_ref[0]))
    h_ref[CONV_HALO:CONV_HALO + tc, :] = glu(u_ref[0])
    for b in range(1, SUBLANES):
        hs_ref[b - 1] = h_ref[b:b + tc + CONV_HALO - SUBLANES, :]
    first = CONV_HALO - (CONV_WIDTH - 1)
    for c in range(tc // CONV_ROWS):
        acc = jnp.zeros((CONV_ROWS, CONV_CH), F32)
        for w in range(CONV_WIDTH):
            a, b = divmod(first + w, SUBLANES)
            r0 = c * CONV_ROWS + a * SUBLANES
            src = h_ref[r0:r0 + CONV_ROWS, :] if b == 0 else hs_ref[b - 1, r0:r0 + CONV_ROWS, :]
            acc = acc + src * cw_ref[w:w + 1, :]
        y = acc + cb_ref[...]
        mu = jnp.mean(y, axis=-1, keepdims=True)
        var = jnp.mean(jnp.square(y - mu), axis=-1, keepdims=True)
        y = (y - mu) * lax.rsqrt(var + EPS) * lg_ref[...] + lb_ref[...]
        o_ref[0, c * CONV_ROWS:(c + 1) * CONV_ROWS, :] = (y * jax.nn.sigmoid(y)).astype(o_ref.dtype)


def _conv(row, cw, cb, lg, lb, l, tc):
    B, S, _ = row.shape
    ucol = 0
    per = tc // CONV_HALO
    return pl.pallas_call(
        functools.partial(_conv_kernel, tc=tc),
        grid=(B, S // tc),
        in_specs=[
            pl.BlockSpec((1, tc, 2 * CONV_CH), lambda b, i: (b, i, ucol)),
            pl.BlockSpec((1, CONV_HALO, 2 * CONV_CH), lambda b, i: (b, jnp.maximum(i * per - 1, 0), ucol)),
            _layer_spec(cw, l),
            _layer_spec(cb, l),
            _layer_spec(lg, l),
            _layer_spec(lb, l),
        ],
        out_specs=pl.BlockSpec((1, tc, CONV_CH), lambda b, i: (b, i, 0)),
        out_shape=jax.ShapeDtypeStruct((B, S, CONV_CH), BF16),
        scratch_shapes=[pltpu.VMEM((CONV_HALO + tc, CONV_CH), F32),
                        pltpu.VMEM((SUBLANES - 1, CONV_HALO + tc - SUBLANES, CONV_CH), F32)],
        compiler_params=_params("parallel", "parallel"),
        name="conformer_conv",
    )(row, row, cw, cb, lg, lb)


def _merge_kernel(x_ref, g_ref, wg_ref, osb_ref, odf_ref, ocv_ref, wsb_ref, wdf_ref, wcv_ref, wout_ref, o_ref):
    x = x_ref[...]
    D = x.shape[1]
    xb = _rmsnorm(x, g_ref[...]).astype(BF16)
    y = None
    for n, (o_br, w_br) in enumerate(((osb_ref, wsb_ref), (odf_ref, wdf_ref), (ocv_ref, wcv_ref))):
        gate = jax.nn.sigmoid(_dot(xb, wg_ref[:, n * D:(n + 1) * D]))
        t = gate * _dot(o_br[...], w_br[...])
        y = t if y is None else y + t
    o_ref[...] = x + _dot(y.astype(BF16), wout_ref[...])


def _merge(x2, g, l, wg, osb, odf, ocv, wsb, wdf, wcv, wout, tm):
    N, D = x2.shape
    W = osb.shape[1]
    tok = lambda w: pl.BlockSpec((tm, w), lambda i: (i, 0))
    return pl.pallas_call(
        _merge_kernel,
        grid=(N // tm,),
        in_specs=[tok(D), _layer_spec(g, l), _const_spec((D, 3 * D)), tok(W), tok(W), tok(W),
                  _const_spec((W, D)), _const_spec((W, D)), _const_spec((W, D)), _const_spec((D, D))],
        out_specs=tok(D),
        out_shape=jax.ShapeDtypeStruct((N, D), F32),
        compiler_params=_params("parallel"),
        name="gated_merge",
    )(x2, g, wg, osb, odf, ocv, wsb, wdf, wcv, wout)


def _memkv_kernel(mem_ref, g_ref, wkv_ref, o_ref):
    mb = _rmsnorm(mem_ref[0], g_ref[...]).astype(BF16)
    o_ref[0] = _dot(mb, wkv_ref[...]).astype(o_ref.dtype)


def _memkv(mem, g, l, wkv):
    B, M, D = mem.shape
    return pl.pallas_call(
        _memkv_kernel,
        grid=(B,),
        in_specs=[pl.BlockSpec((1, M, D), lambda b: (b, 0, 0)), _layer_spec(g, l), _const_spec((D, 2 * XA_WIDTH))],
        out_specs=pl.BlockSpec((1, M, 2 * XA_WIDTH), lambda b: (b, 0, 0)),
        out_shape=jax.ShapeDtypeStruct((B, M, 2 * XA_WIDTH), BF16),
        compiler_params=_params("parallel"),
        name="mem_kv",
    )(mem, g, wkv)


def _xattn_kernel(x_ref, g_ref, wq_ref, kv_ref, wo_ref, o_ref):
    x = x_ref[0]
    xb = _rmsnorm(x, g_ref[...]).astype(BF16)
    q = _dot(xb, wq_ref[...]).astype(BF16)
    hs = [slice(h * XA_HEAD_DIM, (h + 1) * XA_HEAD_DIM) for h in range(XA_HEADS)]
    s = [_dot_nt(q[:, hs[h]], kv_ref[0, :, hs[h]]) * (XA_HEAD_DIM ** -0.5) for h in range(XA_HEADS)]
    p = [jnp.exp(x - jnp.max(x, axis=1, keepdims=True)) for x in s]
    p = [x / jnp.sum(x, axis=1, keepdims=True) for x in p]
    heads = [_dot(p[h].astype(BF16), kv_ref[0, :, XA_WIDTH + h * XA_HEAD_DIM:XA_WIDTH + (h + 1) * XA_HEAD_DIM])
             .astype(BF16) for h in range(XA_HEADS)]
    o = jnp.concatenate(heads, axis=1)
    o_ref[0] = x + _dot(o, wo_ref[...])


def _xattn(x, g, l, wq, kv, wo, tm):
    B, S, D = x.shape
    M = kv.shape[1]
    return pl.pallas_call(
        _xattn_kernel,
        grid=(B, S // tm),
        in_specs=[
            pl.BlockSpec((1, tm, D), lambda b, i: (b, i, 0)),
            _layer_spec(g, l),
            _const_spec((D, XA_WIDTH)),
            pl.BlockSpec((1, M, 2 * XA_WIDTH), lambda b, i: (b, 0, 0)),
            _const_spec((XA_WIDTH, D)),
        ],
        out_specs=pl.BlockSpec((1, tm, D), lambda b, i: (b, i, 0)),
        out_shape=jax.ShapeDtypeStruct((B, S, D), F32),
        compiler_params=_params("parallel", "parallel"),
        name="cross_attention",
    )(x, g, wq, kv, wo)


def _mlp_kernel(x_ref, g_ref, wup_ref, wdown_ref, gf_ref, o_ref, *, fc, final):
    x = x_ref[...]
    xb = _rmsnorm(x, g_ref[...]).astype(BF16)
    acc = x
    for c in range(wup_ref.shape[1] // fc):
        cs = slice(c * fc, (c + 1) * fc)
        h = jnp.square(jnp.maximum(_dot(xb, wup_ref[:, cs]), 0.0)).astype(BF16)
        acc = acc + _dot(h, wdown_ref[cs, :])
    o_ref[...] = _rmsnorm(acc, gf_ref[...]) if final else acc


def _mlp(x2, g, l, wup, wdown, gf, tm, final):
    N, D = x2.shape
    F = wup.shape[1]
    return pl.pallas_call(
        functools.partial(_mlp_kernel, fc=MLP_CHUNK, final=final),
        grid=(N // tm,),
        in_specs=[pl.BlockSpec((tm, D), lambda i: (i, 0)), _layer_spec(g, l),
                  pl.BlockSpec((D, F), lambda i: (0, 0), pipeline_mode=pl.Buffered(1)),
                  pl.BlockSpec((F, D), lambda i: (0, 0), pipeline_mode=pl.Buffered(1)), _const_spec((1, D))],
        out_specs=pl.BlockSpec((tm, D), lambda i: (i, 0)),
        out_shape=jax.ShapeDtypeStruct((N, D), F32),
        compiler_params=_params("parallel"),
        name="sqrelu_mlp",
    )(x2, g, wup, wdown, gf)


def kernel(x, mem, rel_bias, ln_mix, w_in, diff_lambda, diff_subln, conv_w, conv_b, conv_ln_g, conv_ln_b,
           w_sb_proj, w_diff_proj, w_conv_proj, w_out, ln_xattn, ln_mem, xa_w_q, xa_w_kv, xa_w_o,
           ln_mlp, w_up, w_down, ln_final):
    B, S, D = x.shape
    depth = w_in.shape[0]
    blk = ATT_BLOCK
    tm = min(TOKEN_TILE, S)
    tc = min(CONV_TILE, S)
    assert S % SB_QBLOCK == 0 and S % DF_QBLOCK == 0 and S % tm == 0 and S % tc == 0
    assert w_in.shape[1:] == (D, 3 * SB_WIDTH + 3 * DIFF_WIDTH + 2 * CONV_CH + 3 * D)

    stack = lambda a: a.reshape(depth, 1, -1).astype(F32)
    g_mix, g_xattn, g_mem, g_mlp, subln = (stack(a) for a in (ln_mix, ln_xattn, ln_mem, ln_mlp, diff_subln))
    cb, clg, clb = stack(conv_b), stack(conv_ln_g), stack(conv_ln_b)
    cw = conv_w.reshape(depth, CONV_WIDTH, CONV_CH).astype(F32)
    lam_p = diff_lambda.astype(F32)
    g_final = ln_final.reshape(1, -1).astype(F32)
    tri = jnp.asarray(np.triu(np.ones((blk, blk), np.float32), k=1), BF16)
    bias = _bias_tiles(rel_bias, blk)
    scale = HEAD_DIM ** -0.5

    for l in range(depth):
        w = w_in[l]
        edges = np.cumsum([0] + 3 * [SB_WIDTH] + 3 * [DIFF_WIDTH] + [2 * CONV_CH])
        sbq, sbk, sbv, dfq, dfk, dfv, glu = (w[:, edges[n]:edges[n + 1]] for n in range(7))
        wrow = jnp.concatenate([glu, sbk, dfk], axis=1).astype(BF16)
        wtr = jnp.concatenate([sbq * scale, sbv, dfq * scale, dfv], axis=1).T.astype(BF16)
        wg = w[:, edges[7]:].astype(BF16)

        row, tr, va = _inproj(x, g_mix, l, wrow, wtr, tm)
        kn = _key_norm_max(row)
        o_sb = _sb_attention(row, tr, tri, kn)
        lam_init = 0.8 - 0.6 * math.exp(-0.3 * l)
        o_df = _diff_attention(row, tr, va, bias, kn, lam_p, subln, l, lam_init)
        o_cv = _conv(row, cw, cb, clg, clb, l, tc)

        N = B * S
        x2 = _merge(x.reshape(N, D), g_mix, l, wg,
                    o_sb.reshape(N, -1), o_df.reshape(N, -1), o_cv.reshape(N, -1),
                    w_sb_proj[l].astype(BF16), w_diff_proj[l].astype(BF16), w_conv_proj[l].astype(BF16),
                    w_out[l].astype(BF16), tm)

        kv = _memkv(mem, g_mem, l, xa_w_kv[l].astype(BF16))
        x3 = _xattn(x2.reshape(B, S, D), g_xattn, l, xa_w_q[l].astype(BF16), kv, xa_w_o[l].astype(BF16), tm)

        x = _mlp(x3.reshape(N, D), g_mlp, l, w_up[l].astype(BF16), w_down[l].astype(BF16),
                 g_final, min(MLP_TOKEN_TILE, N), final=(l == depth - 1)).reshape(B, S, D)
    return x
```
